```python
import jax, jax.numpy as jnp
from jax import lax
import numpy as np

D_MODEL = 1024
BATCH = 8
SEQ = 2048
DEPTH = 1
DEC_BATCH = 16
DEC_SEQ = 32
PAST_LEN = 2048

CHUNK = 64
D_MIX = D_MODEL
D_CONV = D_MIX // 2
CONV_WIDTH = 31
CONV_BUF = CONV_WIDTH - 1
D_HGRN = D_MIX - D_CONV
HGRN_HEADS = 4
HGRN_DK = D_HGRN // HGRN_HEADS
HGRN_DV = D_HGRN // HGRN_HEADS
D_IN = 2 * D_CONV + 4 * D_HGRN
SPLITS = (D_CONV, 2 * D_CONV, 2 * D_CONV + D_HGRN, 2 * D_CONV + 2 * D_HGRN, 2 * D_CONV + 3 * D_HGRN)
N_MEM = 256
MEM_HEADS = 4
MEM_HEAD_DIM = D_MODEL // MEM_HEADS
D_FF = 2816
EPS = 1e-6

kernel_name = 'hymba_conformer_hgrn2_streaming_step'


def rms_norm(x, g):
    xf = x.astype(jnp.float32)
    y = xf * lax.rsqrt(jnp.mean(xf * xf, axis=-1, keepdims=True) + EPS)
    return (y * g.astype(jnp.float32)).astype(x.dtype)


def layer_norm(x, g, b):
    xf = x.astype(jnp.float32)
    mu = jnp.mean(xf, axis=-1, keepdims=True)
    var = jnp.mean(jnp.square(xf - mu), axis=-1, keepdims=True)
    y = (xf - mu) * lax.rsqrt(var + EPS)
    return (y * g.astype(jnp.float32) + b.astype(jnp.float32)).astype(x.dtype)


def swiglu(x, w_gate, w_up, w_down):
    return (jax.nn.silu(x @ w_gate) * (x @ w_up)) @ w_down


def hgrn_chunk(s0, q, k, v, logf):
    c = q.shape[1]
    g_cum = jnp.cumsum(logf, axis=1)
    g_last = g_cum[:, -1]
    q_dec = q * jnp.exp(g_cum)
    k_dec = k * jnp.exp(-g_cum)
    a = jnp.einsum('bthk,bshk->bhts', q_dec, k_dec)
    a = jnp.where(jnp.tril(jnp.ones((c, c), dtype=bool)), a, 0.0)
    o = jnp.einsum('bhts,bshv->bthv', a, v) + jnp.einsum('bthk,bhkv->bthv', q_dec, s0)
    k_end = k * jnp.exp(g_last[:, None] - g_cum)
    s1 = s0 * jnp.exp(g_last)[..., None] + jnp.einsum('bshk,bshv->bhkv', k_end, v)
    return s1, o


def hgrn_recurrence(s0, q, k, v, logf):
    b, t = q.shape[:2]
    c = min(CHUNK, t)
    n = t // c

    def to_blocks(a):
        return a.reshape(b, n, c, *a.shape[2:]).swapaxes(0, 1)

    def step(s, xs):
        return hgrn_chunk(s, *xs)

    s1, o = lax.scan(step, s0, (to_blocks(q), to_blocks(k), to_blocks(v), to_blocks(logf)))
    o = o.swapaxes(0, 1).reshape(b, t, HGRN_HEADS, HGRN_DV)
    return s1, o


def conv_group(c_val, c_gate, buf, w_dw, b_dw, ln_g, ln_b):
    u = c_val * jax.nn.sigmoid(c_gate)
    u_ext = jnp.concatenate([buf.astype(u.dtype), u], axis=1)
    y = lax.conv_general_dilated(
        u_ext, w_dw[:, None, :].astype(u.dtype), window_strides=(1,), padding='VALID',
        dimension_numbers=('NWC', 'WIO', 'NWC'), feature_group_count=D_CONV) + b_dw
    y = jax.nn.silu(layer_norm(y, ln_g, ln_b))
    return y, u_ext[:, -CONV_BUF:]


def hgrn_group(hq, hf, hi, hg, s0, lb, g_norm):
    b, t, _ = hq.shape

    def heads(a):
        return a.astype(jnp.float32).reshape(b, t, HGRN_HEADS, -1)

    lbh = lb.reshape(HGRN_HEADS, HGRN_DK)
    f = lbh + (1.0 - lbh) * jax.nn.sigmoid(heads(hf))
    q = jax.nn.silu(heads(hq))
    s1, o = hgrn_recurrence(s0.astype(jnp.float32), q, 1.0 - f, heads(hi), jnp.log(f))
    o = rms_norm(o, g_norm) * jax.nn.silu(heads(hg))
    return o.reshape(b, t, D_HGRN).astype(hq.dtype), s1


def mem_kv(mem, g, wk, wv):
    b = mem.shape[0]
    m = rms_norm(mem, g)
    k = (m @ wk).reshape(b, N_MEM, MEM_HEADS, MEM_HEAD_DIM)
    v = (m @ wv).reshape(b, N_MEM, MEM_HEADS, MEM_HEAD_DIM)
    return k, v


def mem_attend(h, k, v, wq, wo):
    b, t, _ = h.shape
    q = (h @ wq).reshape(b, t, MEM_HEADS, MEM_HEAD_DIM)
    s = jnp.einsum('bthd,bmhd->bhtm', q, k).astype(jnp.float32) * (MEM_HEAD_DIM ** -0.5)
    p = jax.nn.softmax(s, axis=-1).astype(v.dtype)
    o = jnp.einsum('bhtm,bmhd->bthd', p, v).reshape(b, t, D_MODEL)
    return o @ wo


def encoder_layer(x, conv_buf, s0, mk, mv, lb, w):
    x = x + 0.5 * swiglu(rms_norm(x, w['ffn1_norm']), w['ffn1_w_gate'], w['ffn1_w_up'], w['ffn1_w_down'])
    z = rms_norm(x, w['mix_norm']) @ w['w_in']
    c_val, c_gate, hq, hf, hi, hg = jnp.split(z, SPLITS, axis=-1)
    y_conv, new_buf = conv_group(c_val, c_gate, conv_buf, w['conv_dw'], w['conv_dw_b'],
                                 w['conv_ln_g'], w['conv_ln_b'])
    y_hgrn, new_s = hgrn_group(hq, hf, hi, hg, s0, lb, w['hgrn_norm'])
    x = x + jnp.concatenate([y_conv, y_hgrn], axis=-1) @ w['w_out']
    x = x + mem_attend(rms_norm(x, w['xattn_norm']), mk, mv, w['xattn_wq'], w['xattn_wo'])
    x = x + 0.5 * swiglu(rms_norm(x, w['ffn2_norm']), w['ffn2_w_gate'], w['ffn2_w_up'], w['ffn2_w_down'])
    return x, new_buf, new_s


def setup_inputs(seed: int = 0) -> dict:
    key = jax.random.key(seed)
    ks = iter(jax.random.split(key, 40))

    def nrm(shape, scale):
        return jax.random.normal(next(ks), shape, jnp.float32) * scale

    def gain(shape):
        return 1.0 + nrm(shape, 0.02)

    return {
        'x_prompt': nrm((BATCH, SEQ, D_MODEL), 1.0),
        'x_sample': nrm((DEC_BATCH, DEC_SEQ, D_MODEL), 1.0),
        'mem_prompt': nrm((BATCH, N_MEM, D_MODEL), 1.0),
        'state_conv': nrm((DEPTH, DEC_BATCH, CONV_BUF, D_CONV), 0.5),
        'state_hgrn': nrm((DEPTH, DEC_BATCH, HGRN_HEADS, HGRN_DK, HGRN_DV), 0.5),
        'cache_mem_k': nrm((DEPTH, DEC_BATCH, N_MEM, MEM_HEADS, MEM_HEAD_DIM), 1.0),
        'cache_mem_v': nrm((DEPTH, DEC_BATCH, N_MEM, MEM_HEADS, MEM_HEAD_DIM), 1.0),
        'ffn1_norm': gain((DEPTH, D_MODEL)),
        'ffn1_w_gate': nrm((DEPTH, D_MODEL, D_FF), D_MODEL ** -0.5),
        'ffn1_w_up': nrm((DEPTH, D_MODEL, D_FF), D_MODEL ** -0.5),
        'ffn1_w_down': nrm((DEPTH, D_FF, D_MODEL), D_FF ** -0.5),
        'mix_norm': gain((DEPTH, D_MODEL)),
        'w_in': nrm((DEPTH, D_MODEL, D_IN), D_MODEL ** -0.5),
        'conv_dw': nrm((DEPTH, CONV_WIDTH, D_CONV), CONV_WIDTH ** -0.5),
        'conv_dw_b': nrm((DEPTH, D_CONV), 0.02),
        'conv_ln_g': gain((DEPTH, D_CONV)),
        'conv_ln_b': nrm((DEPTH, D_CONV), 0.02),
        'hgrn_lb': nrm((DEPTH + 1, D_HGRN), 0.1),
        'hgrn_norm': gain((DEPTH, HGRN_HEADS, HGRN_DV)),
        'w_out': nrm((DEPTH, D_MIX, D_MODEL), D_MIX ** -0.5),
        'mem_norm': gain((DEPTH, D_MODEL)),
        'mem_wk': nrm((DEPTH, D_MODEL, D_MODEL), D_MODEL ** -0.5),
        'mem_wv': nrm((DEPTH, D_MODEL, D_MODEL), D_MODEL ** -0.5),
        'xattn_norm': gain((DEPTH, D_MODEL)),
        'xattn_wq': nrm((DEPTH, D_MODEL, D_MODEL), D_MODEL ** -0.5),
        'xattn_wo': nrm((DEPTH, D_MODEL, D_MODEL), D_MODEL ** -0.5),
        'ffn2_norm': gain((DEPTH, D_MODEL)),
        'ffn2_w_gate': nrm((DEPTH, D_MODEL, D_FF), D_MODEL ** -0.5),
        'ffn2_w_up': nrm((DEPTH, D_MODEL, D_FF), D_MODEL ** -0.5),
        'ffn2_w_down': nrm((DEPTH, D_FF, D_MODEL), D_FF ** -0.5),
        'final_norm': gain((D_MODEL,)),
    }


def reference(x_prompt, x_sample, mem_prompt, state_conv, state_hgrn, cache_mem_k, cache_mem_v,
              ffn1_norm, ffn1_w_gate, ffn1_w_up, ffn1_w_down, mix_norm, w_in, conv_dw, conv_dw_b,
              conv_ln_g, conv_ln_b, hgrn_lb, hgrn_norm, w_out, mem_norm, mem_wk, mem_wv,
              xattn_norm, xattn_wq, xattn_wo, ffn2_norm, ffn2_w_gate, ffn2_w_up, ffn2_w_down,
              final_norm):
    lb_all = jnp.cumsum(jax.nn.softmax(hgrn_lb.astype(jnp.float32), axis=0), axis=0)
    bp = x_prompt.shape[0]
    xp, xs = x_prompt, x_sample
    conv_p, hgrn_p, mk_p, mv_p, conv_s, hgrn_s = [], [], [], [], [], []
    for l in range(DEPTH):
        w = {
            'ffn1_norm': ffn1_norm[l], 'ffn1_w_gate': ffn1_w_gate[l], 'ffn1_w_up': ffn1_w_up[l],
            'ffn1_w_down': ffn1_w_down[l], 'mix_norm': mix_norm[l], 'w_in': w_in[l],
            'conv_dw': conv_dw[l], 'conv_dw_b': conv_dw_b[l], 'conv_ln_g': conv_ln_g[l],
            'conv_ln_b': conv_ln_b[l], 'hgrn_norm': hgrn_norm[l], 'w_out': w_out[l],
            'xattn_norm': xattn_norm[l], 'xattn_wq': xattn_wq[l], 'xattn_wo': xattn_wo[l],
            'ffn2_norm': ffn2_norm[l], 'ffn2_w_gate': ffn2_w_gate[l], 'ffn2_w_up': ffn2_w_up[l],
            'ffn2_w_down': ffn2_w_down[l],
        }
        mk, mv = mem_kv(mem_prompt, mem_norm[l], mem_wk[l], mem_wv[l])
        buf0 = jnp.zeros((bp, CONV_BUF, D_CONV), xp.dtype)
        s0 = jnp.zeros((bp, HGRN_HEADS, HGRN_DK, HGRN_DV), jnp.float32)
        xp, buf_p, s_p = encoder_layer(xp, buf0, s0, mk, mv, lb_all[l], w)
        xs, buf_s, s_s = encoder_layer(xs, state_conv[l], state_hgrn[l], cache_mem_k[l],
                                       cache_mem_v[l], lb_all[l], w)
        conv_p.append(buf_p)
        hgrn_p.append(s_p.astype(x_prompt.dtype))
        mk_p.append(mk)
        mv_p.append(mv)
        conv_s.append(buf_s.astype(state_conv.dtype))
        hgrn_s.append(s_s.astype(state_hgrn.dtype))
    y_prompt = rms_norm(xp, final_norm)
    y_sample = rms_norm(xs, final_norm)
    return (y_prompt, y_sample, jnp.stack(conv_p), jnp.stack(hgrn_p), jnp.stack(mk_p),
            jnp.stack(mv_p), jnp.stack(conv_s), jnp.stack(hgrn_s))
```

```python
import functools

import jax
import jax.numpy as jnp
from jax import lax
from jax.experimental import pallas as pl
from jax.experimental.pallas import tpu as pltpu

F32 = jnp.float32
BF16 = jnp.bfloat16

EPS = 1e-6
CHUNK = 64
CONV_WIDTH = 31
CONV_BUF = CONV_WIDTH - 1
HGRN_HEADS = 4
MEM_HEADS = 4
HALO = 32
CONV_ROWS = 32
VMEM_LIMIT = 56 * 1024 * 1024


def _dot(a, b):
    return jnp.dot(a, b, preferred_element_type=F32)


def _dot_nt(a, b):
    return lax.dot_general(a, b, (((1,), (1,)), ((), ())), preferred_element_type=F32)


def _dot_tn(a, b):
    return lax.dot_general(a, b, (((0,), (0,)), ((), ())), preferred_element_type=F32)


def _rms(x, g):
    return x * lax.rsqrt(jnp.mean(x * x, axis=-1, keepdims=True) + EPS) * g


def _sigmoid(x):
    return 1.0 / (1.0 + jnp.exp(-x))


def _silu(x):
    return x * _sigmoid(x)


def _ffn(x, ng, wg, wu, wd):
    xn = _rms(x, ng).astype(BF16)
    h = (_silu(_dot(xn, wg)) * _dot(xn, wu)).astype(BF16)
    return x + 0.5 * _dot(h, wd)


def _const_spec(shape):
    nd = len(shape)
    return pl.BlockSpec(shape, lambda *_: (0,) * nd, pipeline_mode=pl.Buffered(1))


def _ffn_kernel(x_ref, ng_ref, wg_ref, wu_ref, wd_ref, o_ref):
    o_ref[...] = _ffn(x_ref[...], ng_ref[...], wg_ref[...], wu_ref[...], wd_ref[...])


def _ffn_call(x, ng, wg, wu, wd, tm):
    m, d = x.shape
    return pl.pallas_call(
        _ffn_kernel,
        grid=(m // tm,),
        in_specs=[pl.BlockSpec((tm, d), lambda i: (i, 0)),
                  _const_spec(ng.shape), _const_spec(wg.shape), _const_spec(wu.shape),
                  _const_spec(wd.shape)],
        out_specs=pl.BlockSpec((tm, d), lambda i: (i, 0)),
        out_shape=jax.ShapeDtypeStruct((m, d), F32),
        compiler_params=pltpu.CompilerParams(dimension_semantics=("arbitrary",),
                                             vmem_limit_bytes=VMEM_LIMIT),
        name="ffn",
    )(x, ng, wg, wu, wd)


def _memkv_kernel(m_ref, g_ref, wk_ref, wv_ref, k_ref, v_ref):
    mn = _rms(m_ref[...], g_ref[...]).astype(BF16)
    k_ref[...] = _dot(mn, wk_ref[...])
    v_ref[...] = _dot(mn, wv_ref[...])


def _memkv_call(mem, g, wk, wv, tm):
    m, d = mem.shape
    row = pl.BlockSpec((tm, d), lambda i: (i, 0))
    return pl.pallas_call(
        _memkv_kernel,
        grid=(m // tm,),
        in_specs=[row, _const_spec(g.shape), _const_spec(wk.shape), _const_spec(wv.shape)],
        out_specs=[row, row],
        out_shape=[jax.ShapeDtypeStruct((m, d), F32)] * 2,
        compiler_params=pltpu.CompilerParams(dimension_semantics=("arbitrary",),
                                             vmem_limit_bytes=VMEM_LIMIT),
        name="memkv",
    )(mem, g, wk, wv)


def _mix_kernel(nb, tt, chunk,
                x_ref, ng_ref, win_ref, cw_ref, cb_ref, lng_ref, lnb_ref, hlb_ref, hn_ref, wout_ref,
                sconv_ref, shgrn_ref,
                o_ref, oconv_ref, ohgrn_ref,
                ubuf, sbuf, cat):
    t = pl.program_id(1)
    d_conv = cw_ref.shape[1]
    d_head = sbuf.shape[-1]
    m = nb * tt

    @pl.when(t == 0)
    def _load_state():
        ubuf[:, 0:HALO - CONV_BUF, :] = jnp.zeros((nb, HALO - CONV_BUF, d_conv), F32)
        ubuf[:, HALO - CONV_BUF:HALO, :] = sconv_ref[...]
        for b in range(nb):
            for h in range(HGRN_HEADS):
                sbuf[b, h] = shgrn_ref[b, h].T

    x = x_ref[...].reshape(m, x_ref.shape[-1])
    xn = _rms(x, ng_ref[...]).astype(BF16)
    z = _dot(xn, win_ref[...])

    u = z[:, 0:d_conv] * _sigmoid(z[:, d_conv:2 * d_conv])
    for b in range(nb):
        ubuf[b, HALO:HALO + tt, :] = u[b * tt:(b + 1) * tt]
    cb = cb_ref[...]
    lng = lng_ref[...]
    lnb = lnb_ref[...]
    first = HALO - CONV_BUF
    for b in range(nb):
        for r0 in range(0, tt, CONV_ROWS):
            acc = jnp.broadcast_to(cb, (CONV_ROWS, d_conv))
            for j in range(CONV_WIDTH):
                acc = acc + cw_ref[j:j + 1, :] * ubuf[b, first + r0 + j:first + r0 + j + CONV_ROWS, :]
            mu = jnp.mean(acc, axis=-1, keepdims=True)
            dev = acc - mu
            var = jnp.mean(dev * dev, axis=-1, keepdims=True)
            yc = _silu(dev * lax.rsqrt(var + EPS) * lng + lnb)
            cat[b * tt + r0:b * tt + r0 + CONV_ROWS, 0:d_conv] = yc.astype(BF16)
    new_rows = ubuf[:, tt + first:tt + HALO, :]
    ubuf[:, first:HALO, :] = new_rows

    hlb = hlb_ref[...]
    e = jnp.exp(hlb - jnp.max(hlb, axis=0, keepdims=True))
    lbv = e[0:1, :] / jnp.sum(e, axis=0, keepdims=True)
    hn = hn_ref[...]
    o0 = 2 * d_conv
    dh = HGRN_HEADS * d_head
    row_i = lax.broadcasted_iota(jnp.int32, (chunk, chunk), 0)
    col_i = lax.broadcasted_iota(jnp.int32, (chunk, chunk), 1)
    causal = row_i >= col_i
    tri = causal.astype(BF16)
    for b in range(nb):
        for c0 in range(0, tt, chunk):
            r = slice(b * tt + c0, b * tt + c0 + chunk)
            zq = z[r, o0:o0 + dh]
            zf = z[r, o0 + dh:o0 + 2 * dh]
            v = z[r, o0 + 2 * dh:o0 + 3 * dh]
            zg = z[r, o0 + 3 * dh:o0 + 4 * dh]
            f = lbv + (1.0 - lbv) * _sigmoid(zf)
            q = _silu(zq)
            k = 1.0 - f
            logf = jnp.log(f)
            l_hi = logf.astype(BF16)
            l_lo = (logf - l_hi.astype(F32)).astype(BF16)
            gc = _dot(tri, l_hi) + _dot(tri, l_lo)
            g_last = gc[chunk - 1:chunk, :]
            q_dec = (q * jnp.exp(gc)).astype(BF16)
            k_dec = (k * jnp.exp(-gc)).astype(BF16)
            k_end = (k * jnp.exp(g_last - gc)).astype(BF16)
            decay = jnp.exp(g_last)
            vb = v.astype(BF16)
            for h in range(HGRN_HEADS):
                s = slice(h * d_head, (h + 1) * d_head)
                a = jnp.where(causal, _dot_nt(q_dec[:, s], k_dec[:, s]), 0.0).astype(BF16)
                st = sbuf[b, h]
                o = _dot(a, vb[:, s]) + _dot_nt(q_dec[:, s], st.astype(BF16))
                sbuf[b, h] = st * decay[:, s] + _dot_tn(vb[:, s], k_end[:, s])
                on = o * lax.rsqrt(jnp.mean(o * o, axis=-1, keepdims=True) + EPS) * hn[:, s]
                yh = on * _silu(zg[:, s])
                cat[r, d_conv + h * d_head:d_conv + (h + 1) * d_head] = yh.astype(BF16)

    o_ref[...] = (x + _dot(cat[...], wout_ref[...])).reshape(o_ref.shape)

    @pl.when(t == pl.num_programs(1) - 1)
    def _store_state():
        oconv_ref[...] = ubuf[:, first:HALO, :]
        for b in range(nb):
            for h in range(HGRN_HEADS):
                ohgrn_ref[b, h] = sbuf[b, h].T


def _mix_call(x, ng, win, cw, cb, lng, lnb, hlb, hn, wout, sconv, shgrn, nb, tt):
    bsz, seq, d = x.shape
    d_conv = cw.shape[1]
    d_head = shgrn.shape[-1]
    chunk = min(CHUNK, seq)
    kern = functools.partial(_mix_kernel, nb, tt, chunk)
    xspec = pl.BlockSpec((nb, tt, d), lambda i, t: (i, t, 0))
    cspec = pl.BlockSpec((nb, CONV_BUF, d_conv), lambda i, t: (i, 0, 0))
    sspec = pl.BlockSpec((nb, HGRN_HEADS, d_head, d_head), lambda i, t: (i, 0, 0, 0))
    return pl.pallas_call(
        kern,
        grid=(bsz // nb, seq // tt),
        in_specs=[xspec] + [_const_spec(a.shape) for a in (ng, win, cw, cb, lng, lnb, hlb, hn, wout)]
                 + [cspec, sspec],
        out_specs=[xspec, cspec, sspec],
        out_shape=[jax.ShapeDtypeStruct(x.shape, F32),
                   jax.ShapeDtypeStruct(sconv.shape, F32),
                   jax.ShapeDtypeStruct(shgrn.shape, F32)],
        scratch_shapes=[pltpu.VMEM((nb, HALO + tt, d_conv), F32),
                        pltpu.VMEM((nb, HGRN_HEADS, d_head, d_head), F32),
                        pltpu.VMEM((nb * tt, d), BF16)],
        compiler_params=pltpu.CompilerParams(dimension_semantics=("arbitrary", "arbitrary"),
                                             vmem_limit_bytes=VMEM_LIMIT),
        name="mix",
    )(x, ng, win, cw, cb, lng, lnb, hlb, hn, wout, sconv, shgrn)


def _tail_kernel(nb, tt,
                 x_ref, k_ref, v_ref, xg_ref, wq_ref, wo_ref, ng_ref, wg_ref, wu_ref, wd_ref, fg_ref,
                 o_ref, att):
    d = x_ref.shape[-1]
    dh = d // MEM_HEADS
    m = nb * tt
    x = x_ref[...].reshape(m, d)
    q = _dot(_rms(x, xg_ref[...]).astype(BF16), wq_ref[...])
    scale = dh ** -0.5
    for b in range(nb):
        kb = k_ref[b].astype(BF16)
        vb = v_ref[b].astype(BF16)
        for h in range(MEM_HEADS):
            s = slice(h * dh, (h + 1) * dh)
            sc = _dot_nt(q[b * tt:(b + 1) * tt, s].astype(BF16), kb[:, s]) * scale
            ex = jnp.exp(sc - jnp.max(sc, axis=-1, keepdims=True))
            p = ex / jnp.sum(ex, axis=-1, keepdims=True)
            att[b * tt:(b + 1) * tt, s] = _dot(p.astype(BF16), vb[:, s]).astype(BF16)
    x = x + _dot(att[...], wo_ref[...])
    x = _ffn(x, ng_ref[...], wg_ref[...], wu_ref[...], wd_ref[...])
    o_ref[...] = _rms(x, fg_ref[...]).reshape(o_ref.shape)


def _tail_call(x, k, v, xg, wq, wo, ng, wg, wu, wd, fg, nb, tt):
    bsz, seq, d = x.shape
    n_mem = k.shape[1]
    kern = functools.partial(_tail_kernel, nb, tt)
    xspec = pl.BlockSpec((nb, tt, d), lambda i, t: (i, t, 0))
    kvspec = pl.BlockSpec((nb, n_mem, d), lambda i, t: (i, 0, 0))
    return pl.pallas_call(
        kern,
        grid=(bsz // nb, seq // tt),
        in_specs=[xspec, kvspec, kvspec]
                 + [_const_spec(a.shape) for a in (xg, wq, wo, ng, wg, wu, wd, fg)],
        out_specs=xspec,
        out_shape=jax.ShapeDtypeStruct(x.shape, F32),
        scratch_shapes=[pltpu.VMEM((nb * tt, d), BF16)],
        compiler_params=pltpu.CompilerParams(dimension_semantics=("arbitrary", "arbitrary"),
                                             vmem_limit_bytes=VMEM_LIMIT),
        name="tail",
    )(x, k, v, xg, wq, wo, ng, wg, wu, wd, fg)


def _layer(x, sconv, shgrn, mk, mv, w, fg, tm, mix_tile, tail_tile):
    bsz, seq, d = x.shape
    x = _ffn_call(x.reshape(bsz * seq, d), w["ffn1_norm"], w["ffn1_w_gate"], w["ffn1_w_up"],
                  w["ffn1_w_down"], tm).reshape(bsz, seq, d)
    x, new_conv, new_hgrn = _mix_call(x, w["mix_norm"], w["w_in"], w["conv_dw"], w["conv_dw_b"],
                                      w["conv_ln_g"], w["conv_ln_b"], w["hgrn_lb"], w["hgrn_norm"],
                                      w["w_out"], sconv, shgrn, *mix_tile)
    y = _tail_call(x, mk, mv, w["xattn_norm"], w["xattn_wq"], w["xattn_wo"], w["ffn2_norm"],
                   w["ffn2_w_gate"], w["ffn2_w_up"], w["ffn2_w_down"], fg, *tail_tile)
    return y, new_conv, new_hgrn


def kernel(x_prompt, x_sample, mem_prompt, state_conv, state_hgrn, cache_mem_k, cache_mem_v, ffn1_norm, ffn1_w_gate, ffn1_w_up, ffn1_w_down, mix_norm, w_in, conv_dw, conv_dw_b, conv_ln_g, conv_ln_b, hgrn_lb, hgrn_norm, w_out, mem_norm, mem_wk, mem_wv, xattn_norm, xattn_wq, xattn_wo, ffn2_norm, ffn2_w_gate, ffn2_w_up, ffn2_w_down, final_norm):
    depth = ffn1_norm.shape[0]
    assert depth == 1, "the per-layer forget-gate bound is taken from row 0 of the cumulative softmax"
    bp, seq, d = x_prompt.shape
    bs = x_sample.shape[0]
    n_mem = mem_prompt.shape[1]
    d_hgrn = hgrn_lb.shape[1]
    d_conv = conv_dw.shape[2]
    d_head = d_hgrn // HGRN_HEADS

    def row(a):
        return a.reshape(1, -1)

    w = {
        "ffn1_norm": row(ffn1_norm[0]), "ffn1_w_gate": ffn1_w_gate[0].astype(BF16),
        "ffn1_w_up": ffn1_w_up[0].astype(BF16), "ffn1_w_down": ffn1_w_down[0].astype(BF16),
        "mix_norm": row(mix_norm[0]), "w_in": w_in[0].astype(BF16),
        "conv_dw": conv_dw[0], "conv_dw_b": row(conv_dw_b[0]),
        "conv_ln_g": row(conv_ln_g[0]), "conv_ln_b": row(conv_ln_b[0]),
        "hgrn_lb": hgrn_lb, "hgrn_norm": row(hgrn_norm[0]), "w_out": w_out[0].astype(BF16),
        "xattn_norm": row(xattn_norm[0]), "xattn_wq": xattn_wq[0].astype(BF16),
        "xattn_wo": xattn_wo[0].astype(BF16),
        "ffn2_norm": row(ffn2_norm[0]), "ffn2_w_gate": ffn2_w_gate[0].astype(BF16),
        "ffn2_w_up": ffn2_w_up[0].astype(BF16), "ffn2_w_down": ffn2_w_down[0].astype(BF16),
    }
    fg = row(final_norm)

    mk, mv = _memkv_call(mem_prompt.reshape(bp * n_mem, d), row(mem_norm[0]),
                         mem_wk[0].astype(BF16), mem_wv[0].astype(BF16), 512)
    mk = mk.reshape(bp, n_mem, d)
    mv = mv.reshape(bp, n_mem, d)

    y_p, conv_p, hgrn_p = _layer(
        x_prompt, jnp.zeros((bp, CONV_BUF, d_conv), F32),
        jnp.zeros((bp, HGRN_HEADS, d_head, d_head), F32), mk, mv, w, fg,
        tm=256, mix_tile=(1, 256), tail_tile=(1, 256))
    y_s, conv_s, hgrn_s = _layer(
        x_sample, state_conv[0], state_hgrn[0],
        cache_mem_k[0].reshape(bs, n_mem, d), cache_mem_v[0].reshape(bs, n_mem, d), w, fg,
        tm=256, mix_tile=(8, x_sample.shape[1]), tail_tile=(4, x_sample.shape[1]))

    kv_shape = (1, bp, n_mem, MEM_HEADS, d // MEM_HEADS)
    return (y_p, y_s, conv_p[None], hgrn_p[None], mk.reshape(kv_shape), mv.reshape(kv_shape),
            conv_s[None], hgrn_s[None])
```

```python
import functools

import jax
import jax.numpy as jnp
from jax import lax
from jax.experimental import pallas as pl
from jax.experimental.pallas import tpu as pltpu

F32 = jnp.float32
BF16 = jnp.bfloat16

EPS = 1e-6
CHUNK = 64
CONV_WIDTH = 31
CONV_BUF = CONV_WIDTH - 1
HGRN_HEADS = 4
MEM_HEADS = 4
HALO = 32
CONV_ROWS = 32
SUBLANES = 8
VMEM_LIMIT = 56 * 1024 * 1024


def _dot(a, b):
    return jnp.dot(a, b, preferred_element_type=F32)


def _dot_nt(a, b):
    return lax.dot_general(a, b, (((1,), (1,)), ((), ())), preferred_element_type=F32)


def _dot_tn(a, b):
    return lax.dot_general(a, b, (((0,), (0,)), ((), ())), preferred_element_type=F32)


def _rms(x, g):
    return x * lax.rsqrt(jnp.mean(x * x, axis=-1, keepdims=True) + EPS) * g


def _sigmoid(x):
    return 0.5 * jnp.tanh(0.5 * x) + 0.5


def _silu(x):
    return x * _sigmoid(x)


def _ffn(x, ng, wg, wu, wd):
    xn = _rms(x, ng).astype(BF16)
    h = (_silu(_dot(xn, wg)) * _dot(xn, wu)).astype(BF16)
    return x + 0.5 * _dot(h, wd)


def _const_spec(shape):
    nd = len(shape)
    return pl.BlockSpec(shape, lambda *_: (0,) * nd, pipeline_mode=pl.Buffered(1))


def _ffn_kernel(x_ref, ng_ref, wg_ref, wu_ref, wd_ref, o_ref):
    o_ref[...] = _ffn(x_ref[...], ng_ref[...], wg_ref[...], wu_ref[...], wd_ref[...])


def _ffn_call(x, ng, wg, wu, wd, tm):
    m, d = x.shape
    return pl.pallas_call(
        _ffn_kernel,
        grid=(m // tm,),
        in_specs=[pl.BlockSpec((tm, d), lambda i: (i, 0)),
                  _const_spec(ng.shape), _const_spec(wg.shape), _const_spec(wu.shape),
                  _const_spec(wd.shape)],
        out_specs=pl.BlockSpec((tm, d), lambda i: (i, 0)),
        out_shape=jax.ShapeDtypeStruct((m, d), F32),
        compiler_params=pltpu.CompilerParams(dimension_semantics=("arbitrary",),
                                             vmem_limit_bytes=VMEM_LIMIT),
        name="ffn",
    )(x, ng, wg, wu, wd)


def _memkv_kernel(m_ref, g_ref, wk_ref, wv_ref, k_ref, v_ref):
    mn = _rms(m_ref[...], g_ref[...]).astype(BF16)
    k_ref[...] = _dot(mn, wk_ref[...])
    v_ref[...] = _dot(mn, wv_ref[...])


def _memkv_call(mem, g, wk, wv, tm):
    m, d = mem.shape
    row = pl.BlockSpec((tm, d), lambda i: (i, 0))
    return pl.pallas_call(
        _memkv_kernel,
        grid=(m // tm,),
        in_specs=[row, _const_spec(g.shape), _const_spec(wk.shape), _const_spec(wv.shape)],
        out_specs=[row, row],
        out_shape=[jax.ShapeDtypeStruct((m, d), F32)] * 2,
        compiler_params=pltpu.CompilerParams(dimension_semantics=("arbitrary",),
                                             vmem_limit_bytes=VMEM_LIMIT),
        name="memkv",
    )(mem, g, wk, wv)


def _mix_kernel(nb, tt, chunk,
                x_ref, ng_ref, win_ref, cw_ref, cb_ref, lng_ref, lnb_ref, hlb_ref, hn_ref, wout_ref,
                sconv_ref, shgrn_ref,
                o_ref, oconv_ref, ohgrn_ref,
                ubuf, ushift, wtap, sbuf, cat):
    t = pl.program_id(1)
    d_conv = cw_ref.shape[1]
    d_head = sbuf.shape[-1]
    m = nb * tt

    @pl.when((pl.program_id(0) == 0) & (t == 0))
    def _spread_taps():
        wtap[...] = jnp.broadcast_to(cw_ref[...][:, None, :], wtap.shape)

    @pl.when(t == 0)
    def _load_state():
        ubuf[:, 0:HALO - CONV_BUF, :] = jnp.zeros((nb, HALO - CONV_BUF, d_conv), F32)
        ubuf[:, HALO - CONV_BUF:HALO, :] = sconv_ref[...]
        for b in range(nb):
            for h in range(HGRN_HEADS):
                sbuf[b, h] = shgrn_ref[b, h].T

    x = x_ref[...].reshape(m, x_ref.shape[-1])
    xn = _rms(x, ng_ref[...]).astype(BF16)
    z = _dot(xn, win_ref[...])

    u = z[:, 0:d_conv] * _sigmoid(z[:, d_conv:2 * d_conv])
    for b in range(nb):
        ubuf[b, HALO:HALO + tt, :] = u[b * tt:(b + 1) * tt]
    cb = cb_ref[...]
    lng = lng_ref[...]
    lnb = lnb_ref[...]
    first = HALO - CONV_BUF
    n_shift = ushift.shape[2]
    for r in range(1, SUBLANES):
        ushift[r - 1] = ubuf[:, r:r + n_shift, :]
    for b in range(nb):
        for r0 in range(0, tt, CONV_ROWS):
            acc = jnp.broadcast_to(cb, (CONV_ROWS // SUBLANES, SUBLANES, d_conv))
            for j in range(CONV_WIDTH):
                a8, r = divmod(first + j, SUBLANES)
                lo = r0 + a8 * SUBLANES
                src = ubuf[b, lo:lo + CONV_ROWS, :] if r == 0 else ushift[r - 1, b, lo:lo + CONV_ROWS, :]
                acc = acc + wtap[j] * src.reshape(acc.shape)
            acc = acc.reshape(CONV_ROWS, d_conv)
            mu = jnp.mean(acc, axis=-1, keepdims=True)
            dev = acc - mu
            var = jnp.mean(dev * dev, axis=-1, keepdims=True)
            yc = _silu(dev * lax.rsqrt(var + EPS) * lng + lnb)
            cat[b * tt + r0:b * tt + r0 + CONV_ROWS, 0:d_conv] = yc.astype(BF16)
    new_rows = ubuf[:, tt + first:tt + HALO, :]
    ubuf[:, first:HALO, :] = new_rows

    hlb = hlb_ref[...]
    e = jnp.exp(hlb - jnp.max(hlb, axis=0, keepdims=True))
    lbv = e[0:1, :] / jnp.sum(e, axis=0, keepdims=True)
    hn = hn_ref[...]
    o0 = 2 * d_conv
    dh = HGRN_HEADS * d_head
    row_i = lax.broadcasted_iota(jnp.int32, (chunk, chunk), 0)
    col_i = lax.broadcasted_iota(jnp.int32, (chunk, chunk), 1)
    causal = row_i >= col_i
    tri = causal.astype(BF16)
    for b in range(nb):
        for c0 in range(0, tt, chunk):
            r = slice(b * tt + c0, b * tt + c0 + chunk)
            zq = z[r, o0:o0 + dh]
            zf = z[r, o0 + dh:o0 + 2 * dh]
            v = z[r, o0 + 2 * dh:o0 + 3 * dh]
            zg = z[r, o0 + 3 * dh:o0 + 4 * dh]
            f = lbv + (1.0 - lbv) * _sigmoid(zf)
            q = _silu(zq)
            k = 1.0 - f
            logf = jnp.log(f)
            l_hi = logf.astype(BF16)
            l_lo = (logf - l_hi.astype(F32)).astype(BF16)
            gc = _dot(tri, l_hi) + _dot(tri, l_lo)
            g_last = gc[chunk - 1:chunk, :]
            q_dec = (q * jnp.exp(gc)).astype(BF16)
            k_dec = (k * jnp.exp(-gc)).astype(BF16)
            k_end = (k * jnp.exp(g_last - gc)).astype(BF16)
            decay = jnp.exp(g_last)
            vb = v.astype(BF16)
            for h in range(HGRN_HEADS):
                s = slice(h * d_head, (h + 1) * d_head)
                a = jnp.where(causal, _dot_nt(q_dec[:, s], k_dec[:, s]), 0.0).astype(BF16)
                st = sbuf[b, h]
                o = _dot(a, vb[:, s]) + _dot_nt(q_dec[:, s], st.astype(BF16))
                sbuf[b, h] = st * decay[:, s] + _dot_tn(vb[:, s], k_end[:, s])
                on = o * lax.rsqrt(jnp.mean(o * o, axis=-1, keepdims=True) + EPS) * hn[:, s]
                yh = on * _silu(zg[:, s])
                cat[r, d_conv + h * d_head:d_conv + (h + 1) * d_head] = yh.astype(BF16)

    o_ref[...] = (x + _dot(cat[...], wout_ref[...])).reshape(o_ref.shape)

    @pl.when(t == pl.num_programs(1) - 1)
    def _store_state():
        oconv_ref[...] = ubuf[:, first:HALO, :]
        for b in range(nb):
            for h in range(HGRN_HEADS):
                ohgrn_ref[b, h] = sbuf[b, h].T


def _mix_call(x, ng, win, cw, cb, lng, lnb, hlb, hn, wout, sconv, shgrn, nb, tt):
    bsz, seq, d = x.shape
    d_conv = cw.shape[1]
    d_head = shgrn.shape[-1]
    chunk = min(CHUNK, seq)
    kern = functools.partial(_mix_kernel, nb, tt, chunk)
    xspec = pl.BlockSpec((nb, tt, d), lambda i, t: (i, t, 0))
    cspec = pl.BlockSpec((nb, CONV_BUF, d_conv), lambda i, t: (i, 0, 0))
    sspec = pl.BlockSpec((nb, HGRN_HEADS, d_head, d_head), lambda i, t: (i, 0, 0, 0))
    return pl.pallas_call(
        kern,
        grid=(bsz // nb, seq // tt),
        in_specs=[xspec] + [_const_spec(a.shape) for a in (ng, win, cw, cb, lng, lnb, hlb, hn, wout)]
                 + [cspec, sspec],
        out_specs=[xspec, cspec, sspec],
        out_shape=[jax.ShapeDtypeStruct(x.shape, F32),
                   jax.ShapeDtypeStruct(sconv.shape, F32),
                   jax.ShapeDtypeStruct(shgrn.shape, F32)],
        scratch_shapes=[pltpu.VMEM((nb, HALO + tt, d_conv), F32),
                        pltpu.VMEM((SUBLANES - 1, nb, HALO + tt - SUBLANES, d_conv), F32),
                        pltpu.VMEM((CONV_WIDTH, SUBLANES, d_conv), F32),
                        pltpu.VMEM((nb, HGRN_HEADS, d_head, d_head), F32),
                        pltpu.VMEM((nb * tt, d), BF16)],
        compiler_params=pltpu.CompilerParams(dimension_semantics=("arbitrary", "arbitrary"),
                                             vmem_limit_bytes=VMEM_LIMIT),
        name="mix",
    )(x, ng, win, cw, cb, lng, lnb, hlb, hn, wout, sconv, shgrn)


def _tail_kernel(nb, tt,
                 x_ref, k_ref, v_ref, xg_ref, wq_ref, wo_ref, ng_ref, wg_ref, wu_ref, wd_ref, fg_ref,
                 o_ref, att):
    d = x_ref.shape[-1]
    dh = d // MEM_HEADS
    m = nb * tt
    x = x_ref[...].reshape(m, d)
    q = _dot(_rms(x, xg_ref[...]).astype(BF16), wq_ref[...])
    scale = dh ** -0.5
    for b in range(nb):
        kb = k_ref[b].astype(BF16)
        vb = v_ref[b].astype(BF16)
        for h in range(MEM_HEADS):
            s = slice(h * dh, (h + 1) * dh)
            sc = _dot_nt(q[b * tt:(b + 1) * tt, s].astype(BF16), kb[:, s]) * scale
            ex = jnp.exp(sc - jnp.max(sc, axis=-1, keepdims=True))
            p = ex / jnp.sum(ex, axis=-1, keepdims=True)
            att[b * tt:(b + 1) * tt, s] = _dot(p.astype(BF16), vb[:, s]).astype(BF16)
    x = x + _dot(att[...], wo_ref[...])
    x = _ffn(x, ng_ref[...], wg_ref[...], wu_ref[...], wd_ref[...])
    o_ref[...] = _rms(x, fg_ref[...]).reshape(o_ref.shape)


def _tail_call(x, k, v, xg, wq, wo, ng, wg, wu, wd, fg, nb, tt):
    bsz, seq, d = x.shape
    n_mem = k.shape[1]
    kern = functools.partial(_tail_kernel, nb, tt)
    xspec = pl.BlockSpec((nb, tt, d), lambda i, t: (i, t, 0))
    kvspec = pl.BlockSpec((nb, n_mem, d), lambda i, t: (i, 0, 0))
    return pl.pallas_call(
        kern,
        grid=(bsz // nb, seq // tt),
        in_specs=[xspec, kvspec, kvspec]
                 + [_const_spec(a.shape) for a in (xg, wq, wo, ng, wg, wu, wd, fg)],
        out_specs=xspec,
        out_shape=jax.ShapeDtypeStruct(x.shape, F32),
        scratch_shapes=[pltpu.VMEM((nb * tt, d), BF16)],
        compiler_params=pltpu.CompilerParams(dimension_semantics=("arbitrary", "arbitrary"),
                                             vmem_limit_bytes=VMEM_LIMIT),
        name="tail",
    )(x, k, v, xg, wq, wo, ng, wg, wu, wd, fg)


def _layer(x, sconv, shgrn, mk, mv, w, fg, tm, mix_tile, tail_tile):
    bsz, seq, d = x.shape
    x = _ffn_call(x.reshape(bsz * seq, d), w["ffn1_norm"], w["ffn1_w_gate"], w["ffn1_w_up"],
                  w["ffn1_w_down"], tm).reshape(bsz, seq, d)
    x, new_conv, new_hgrn = _mix_call(x, w["mix_norm"], w["w_in"], w["conv_dw"], w["conv_dw_b"],
                                      w["conv_ln_g"], w["conv_ln_b"], w["hgrn_lb"], w["hgrn_norm"],
                                      w["w_out"], sconv, shgrn, *mix_tile)
    y = _tail_call(x, mk, mv, w["xattn_norm"], w["xattn_wq"], w["xattn_wo"], w["ffn2_norm"],
                   w["ffn2_w_gate"], w["ffn2_w_up"], w["ffn2_w_down"], fg, *tail_tile)
    return y, new_conv, new_hgrn


def kernel(x_prompt, x_sample, mem_prompt, state_conv, state_hgrn, cache_mem_k, cache_mem_v, ffn1_norm, ffn1_w_gate, ffn1_w_up, ffn1_w_down, mix_norm, w_in, conv_dw, conv_dw_b, conv_ln_g, conv_ln_b, hgrn_lb, hgrn_norm, w_out, mem_norm, mem_wk, mem_wv, xattn_norm, xattn_wq, xattn_wo, ffn2_norm, ffn2_w_gate, ffn2_w_up, ffn2_w_down, final_norm):
    depth = ffn1_norm.shape[0]
    assert depth == 1, "the per-layer forget-gate bound is taken from row 0 of the cumulative softmax"
    bp, seq, d = x_prompt.shape
    bs = x_sample.shape[0]
    n_mem = mem_prompt.shape[1]
    d_hgrn = hgrn_lb.shape[1]
    d_conv = conv_dw.shape[2]
    d_head = d_hgrn // HGRN_HEADS

    def row(a):
        return a.reshape(1, -1)

    w = {
        "ffn1_norm": row(ffn1_norm[0]), "ffn1_w_gate": ffn1_w_gate[0].astype(BF16),
        "ffn1_w_up": ffn1_w_up[0].astype(BF16), "ffn1_w_down": ffn1_w_down[0].astype(BF16),
        "mix_norm": row(mix_norm[0]), "w_in": w_in[0].astype(BF16),
        "conv_dw": conv_dw[0], "conv_dw_b": row(conv_dw_b[0]),
        "conv_ln_g": row(conv_ln_g[0]), "conv_ln_b": row(conv_ln_b[0]),
        "hgrn_lb": hgrn_lb, "hgrn_norm": row(hgrn_norm[0]), "w_out": w_out[0].astype(BF16),
        "xattn_norm": row(xattn_norm[0]), "xattn_wq": xattn_wq[0].astype(BF16),
        "xattn_wo": xattn_wo[0].astype(BF16),
        "ffn2_norm": row(ffn2_norm[0]), "ffn2_w_gate": ffn2_w_gate[0].astype(BF16),
        "ffn2_w_up": ffn2_w_up[0].astype(BF16), "ffn2_w_down": ffn2_w_down[0].astype(BF16),
    }
    fg = row(final_norm)

    mk, mv = _memkv_call(mem_prompt.reshape(bp * n_mem, d), row(mem_norm[0]),
                         mem_wk[0].astype(BF16), mem_wv[0].astype(BF16), 512)
    mk = mk.reshape(bp, n_mem, d)
    mv = mv.reshape(bp, n_mem, d)

    y_p, conv_p, hgrn_p = _layer(
        x_prompt, jnp.zeros((bp, CONV_BUF, d_conv), F32),
        jnp.zeros((bp, HGRN_HEADS, d_head, d_head), F32), mk, mv, w, fg,
        tm=256, mix_tile=(1, 256), tail_tile=(1, 256))
    y_s, conv_s, hgrn_s = _layer(
        x_sample, state_conv[0], state_hgrn[0],
        cache_mem_k[0].reshape(bs, n_mem, d), cache_mem_v[0].reshape(bs, n_mem, d), w, fg,
        tm=256, mix_tile=(8, x_sample.shape[1]), tail_tile=(4, x_sample.shape[1]))

    kv_shape = (1, bp, n_mem, MEM_HEADS, d // MEM_HEADS)
    return (y_p, y_s, conv_p[None], hgrn_p[None], mk.reshape(kv_shape), mv.reshape(kv_shape),
            conv_s[None], hgrn_s[None])
```

```python
import functools

import jax
import jax.numpy as jnp
from jax import lax
from jax.experimental import pallas as pl
from jax.experimental.pallas import tpu as pltpu

F32 = jnp.float32
BF16 = jnp.bfloat16

EPS = 1e-6
CHUNK = 64
CONV_WIDTH = 31
CONV_BUF = CONV_WIDTH - 1
HGRN_HEADS = 4
MEM_HEADS = 4
HALO = 32
CONV_ROWS = 32
SUBLANES = 8
VMEM_LIMIT = 56 * 1024 * 1024


def _dot(a, b):
    return jnp.dot(a, b, preferred_element_type=F32)


def _dot_nt(a, b):
    return lax.dot_general(a, b, (((1,), (1,)), ((), ())), preferred_element_type=F32)


def _dot_tn(a, b):
    return lax.dot_general(a, b, (((0,), (0,)), ((), ())), preferred_element_type=F32)


def _rms(x, g):
    return x * lax.rsqrt(jnp.mean(x * x, axis=-1, keepdims=True) + EPS) * g


def _sigmoid(x):
    return 0.5 * jnp.tanh(0.5 * x) + 0.5


def _silu(x):
    return x * _sigmoid(x)


def _ffn(x, ng, wg, wu, wd):
    xn = _rms(x, ng).astype(BF16)
    h = (_silu(_dot(xn, wg)) * _dot(xn, wu)).astype(BF16)
    return x + 0.5 * _dot(h, wd)


def _const_spec(shape):
    nd = len(shape)
    return pl.BlockSpec(shape, lambda *_: (0,) * nd, pipeline_mode=pl.Buffered(1))


def _row_groups(m, n_split):
    mg = m // n_split
    return [(g * mg, (g + 1) * mg) for g in range(n_split)]


def _ffn_kernel(n_split, x_ref, ng_ref, wg_ref, wu_ref, wd_ref, o_ref):
    for g0, g1 in _row_groups(x_ref.shape[0], n_split):
        o_ref[g0:g1, :] = _ffn(x_ref[g0:g1, :], ng_ref[...], wg_ref[...], wu_ref[...], wd_ref[...])


def _ffn_call(x, ng, wg, wu, wd, tm, n_split):
    m, d = x.shape
    return pl.pallas_call(
        functools.partial(_ffn_kernel, n_split),
        grid=(m // tm,),
        in_specs=[pl.BlockSpec((tm, d), lambda i: (i, 0)),
                  _const_spec(ng.shape), _const_spec(wg.shape), _const_spec(wu.shape),
                  _const_spec(wd.shape)],
        out_specs=pl.BlockSpec((tm, d), lambda i: (i, 0)),
        out_shape=jax.ShapeDtypeStruct((m, d), F32),
        compiler_params=pltpu.CompilerParams(dimension_semantics=("arbitrary",),
                                             vmem_limit_bytes=VMEM_LIMIT),
        name="ffn",
    )(x, ng, wg, wu, wd)


def _memkv_kernel(m_ref, g_ref, wk_ref, wv_ref, k_ref, v_ref):
    mn = _rms(m_ref[...], g_ref[...]).astype(BF16)
    k_ref[...] = _dot(mn, wk_ref[...])
    v_ref[...] = _dot(mn, wv_ref[...])


def _memkv_call(mem, g, wk, wv, tm):
    m, d = mem.shape
    row = pl.BlockSpec((tm, d), lambda i: (i, 0))
    return pl.pallas_call(
        _memkv_kernel,
        grid=(m // tm,),
        in_specs=[row, _const_spec(g.shape), _const_spec(wk.shape), _const_spec(wv.shape)],
        out_specs=[row, row],
        out_shape=[jax.ShapeDtypeStruct((m, d), F32)] * 2,
        compiler_params=pltpu.CompilerParams(dimension_semantics=("arbitrary",),
                                             vmem_limit_bytes=VMEM_LIMIT),
        name="memkv",
    )(mem, g, wk, wv)


def _hgrn_rows(z, lbv, hn, masks, state, chunk, d_head):
    tt = z.shape[0]
    n_c = tt // chunk
    dh = HGRN_HEADS * d_head
    tri, causal = masks
    zq, zf, v, zg = (z[:, i * dh:(i + 1) * dh] for i in range(4))
    f = lbv + (1.0 - lbv) * _sigmoid(zf)
    q = _silu(zq)
    k = 1.0 - f
    logf = jnp.log(f)
    l_hi = logf.astype(BF16)
    l_lo = (logf - l_hi.astype(F32)).astype(BF16)
    gc = _dot(tri, l_hi) + _dot(tri, l_lo)
    g_last = [gc[(c + 1) * chunk - 1:(c + 1) * chunk, :] for c in range(n_c)]
    g_end = jnp.concatenate([jnp.broadcast_to(g, (chunk, dh)) for g in g_last], axis=0)
    q_dec = (q * jnp.exp(gc)).astype(BF16)
    k_dec = (k * jnp.exp(-gc)).astype(BF16)
    k_end = (k * jnp.exp(g_end - gc)).astype(BF16)
    decay = [jnp.exp(g) for g in g_last]
    vb = v.astype(BF16)
    zero = jnp.zeros((chunk, d_head), BF16)
    out, new_state = [], []
    for h in range(HGRN_HEADS):
        hs = slice(h * d_head, (h + 1) * d_head)
        a = jnp.where(causal, _dot_nt(q_dec[:, hs], k_dec[:, hs]), 0.0).astype(BF16)
        o_intra = _dot(a, vb[:, hs])
        k_blocks = jnp.concatenate(
            [jnp.concatenate([k_end[c * chunk:(c + 1) * chunk, hs] if j == c else zero
                              for j in range(n_c)], axis=1) for c in range(n_c)], axis=0)
        kv = _dot_tn(vb[:, hs], k_blocks)
        st = [state[h]]
        for c in range(n_c):
            st.append(st[c] * decay[c][:, hs] + kv[:, c * d_head:(c + 1) * d_head])
        new_state.append(st[n_c])
        stack = jnp.concatenate([s.astype(BF16) for s in st[:n_c]], axis=0)
        o_inter = _dot_nt(q_dec[:, hs], stack)
        o = o_intra + jnp.concatenate(
            [o_inter[c * chunk:(c + 1) * chunk, c * d_head:(c + 1) * d_head] for c in range(n_c)], axis=0)
        on = o * lax.rsqrt(jnp.mean(o * o, axis=-1, keepdims=True) + EPS) * hn[:, hs]
        out.append((on * _silu(zg[:, hs])).astype(BF16))
    return out, new_state


def _mix_kernel(nb, tt, chunk,
                x_ref, ng_ref, win_ref, cw_ref, cb_ref, lng_ref, lnb_ref, hlb_ref, hn_ref, wout_ref,
                sconv_ref, shgrn_ref,
                o_ref, oconv_ref, ohgrn_ref,
                zbuf, ubuf, ushift, wtap, sbuf, cat):
    t = pl.program_id(1)
    d_conv = cw_ref.shape[1]
    d_head = sbuf.shape[-1]
    m = nb * tt
    first = HALO - CONV_BUF

    @pl.when((pl.program_id(0) == 0) & (t == 0))
    def _spread_taps():
        wtap[...] = jnp.broadcast_to(cw_ref[...][:, None, :], wtap.shape)

    @pl.when(t == 0)
    def _load_state():
        ubuf[:, 0:first, :] = jnp.zeros((nb, first, d_conv), F32)
        ubuf[:, first:HALO, :] = sconv_ref[...]
        for b in range(nb):
            for h in range(HGRN_HEADS):
                sbuf[b, h] = shgrn_ref[b, h].T

    x = x_ref[...].reshape(m, x_ref.shape[-1])
    zbuf[...] = _dot(_rms(x, ng_ref[...]).astype(BF16), win_ref[...])

    for b in range(nb):
        r = slice(b * tt, (b + 1) * tt)
        ubuf[b, HALO:HALO + tt, :] = zbuf[r, 0:d_conv] * _sigmoid(zbuf[r, d_conv:2 * d_conv])
    cb = cb_ref[...]
    lng = lng_ref[...]
    lnb = lnb_ref[...]
    n_shift = ushift.shape[2]
    for r in range(1, SUBLANES):
        ushift[r - 1] = ubuf[:, r:r + n_shift, :]
    for b in range(nb):
        for r0 in range(0, tt, CONV_ROWS):
            acc = jnp.broadcast_to(cb, (CONV_ROWS // SUBLANES, SUBLANES, d_conv))
            for j in range(CONV_WIDTH):
                a8, r = divmod(first + j, SUBLANES)
                lo = r0 + a8 * SUBLANES
                src = ubuf[b, lo:lo + CONV_ROWS, :] if r == 0 else ushift[r - 1, b, lo:lo + CONV_ROWS, :]
                acc = acc + wtap[j] * src.reshape(acc.shape)
            acc = acc.reshape(CONV_ROWS, d_conv)
            mu = jnp.mean(acc, axis=-1, keepdims=True)
            dev = acc - mu
            var = jnp.mean(dev * dev, axis=-1, keepdims=True)
            yc = _silu(dev * lax.rsqrt(var + EPS) * lng + lnb)
            cat[b * tt + r0:b * tt + r0 + CONV_ROWS, 0:d_conv] = yc.astype(BF16)
    new_rows = ubuf[:, tt + first:tt + HALO, :]
    ubuf[:, first:HALO, :] = new_rows

    hlb = hlb_ref[...]
    e = jnp.exp(hlb - jnp.max(hlb, axis=0, keepdims=True))
    lbv = e[0:1, :] / jnp.sum(e, axis=0, keepdims=True)
    shift = chunk.bit_length() - 1
    assert chunk == 1 << shift
    row_i = lax.broadcasted_iota(jnp.int32, (tt, tt), 0)
    col_i = lax.broadcasted_iota(jnp.int32, (tt, tt), 1)
    causal = (row_i >= col_i) & (lax.shift_right_logical(row_i, shift) == lax.shift_right_logical(col_i, shift))
    masks = (jnp.where(causal, 1.0, 0.0).astype(BF16), causal)
    for b in range(nb):
        r = slice(b * tt, (b + 1) * tt)
        heads, state = _hgrn_rows(zbuf[r, 2 * d_conv:], lbv, hn_ref[...], masks,
                                  [sbuf[b, h] for h in range(HGRN_HEADS)], chunk, d_head)
        for h in range(HGRN_HEADS):
            sbuf[b, h] = state[h]
            cat[r, d_conv + h * d_head:d_conv + (h + 1) * d_head] = heads[h]

    o_ref[...] = (x + _dot(cat[...], wout_ref[...])).reshape(o_ref.shape)

    @pl.when(t == pl.num_programs(1) - 1)
    def _store_state():
        oconv_ref[...] = ubuf[:, first:HALO, :]
        for b in range(nb):
            for h in range(HGRN_HEADS):
                ohgrn_ref[b, h] = sbuf[b, h].T


def _mix_call(x, ng, win, cw, cb, lng, lnb, hlb, hn, wout, sconv, shgrn, nb, tt):
    bsz, seq, d = x.shape
    d_conv = cw.shape[1]
    d_head = shgrn.shape[-1]
    chunk = min(CHUNK, seq)
    kern = functools.partial(_mix_kernel, nb, tt, chunk)
    xspec = pl.BlockSpec((nb, tt, d), lambda i, t: (i, t, 0))
    cspec = pl.BlockSpec((nb, CONV_BUF, d_conv), lambda i, t: (i, 0, 0))
    sspec = pl.BlockSpec((nb, HGRN_HEADS, d_head, d_head), lambda i, t: (i, 0, 0, 0))
    return pl.pallas_call(
        kern,
        grid=(bsz // nb, seq // tt),
        in_specs=[xspec] + [_const_spec(a.shape) for a in (ng, win, cw, cb, lng, lnb, hlb, hn, wout)]
                 + [cspec, sspec],
        out_specs=[xspec, cspec, sspec],
        out_shape=[jax.ShapeDtypeStruct(x.shape, F32),
                   jax.ShapeDtypeStruct(sconv.shape, F32),
                   jax.ShapeDtypeStruct(shgrn.shape, F32)],
        scratch_shapes=[pltpu.VMEM((nb * tt, win.shape[1]), F32),
                        pltpu.VMEM((nb, HALO + tt, d_conv), F32),
                        pltpu.VMEM((SUBLANES - 1, nb, HALO + tt - SUBLANES, d_conv), F32),
                        pltpu.VMEM((CONV_WIDTH, SUBLANES, d_conv), F32),
                        pltpu.VMEM((nb, HGRN_HEADS, d_head, d_head), F32),
                        pltpu.VMEM((nb * tt, d), BF16)],
        compiler_params=pltpu.CompilerParams(dimension_semantics=("arbitrary", "arbitrary"),
                                             vmem_limit_bytes=VMEM_LIMIT),
        name="mix",
    )(x, ng, win, cw, cb, lng, lnb, hlb, hn, wout, sconv, shgrn)


def _tail_kernel(nb, tt, n_split,
                 x_ref, k_ref, v_ref, xg_ref, wq_ref, wo_ref, ng_ref, wg_ref, wu_ref, wd_ref, fg_ref,
                 o_ref):
    d = x_ref.shape[-1]
    dh = d // MEM_HEADS
    scale = dh ** -0.5
    kb = [k_ref[b].astype(BF16) for b in range(nb)]
    vb = [v_ref[b].astype(BF16) for b in range(nb)]
    for g0, g1 in _row_groups(nb * tt, n_split):
        if nb == 1:
            rows, blk = (0, slice(g0, g1)), (g1 - g0, d)
        else:
            assert g0 % tt == 0 and g1 % tt == 0
            rows, blk = (slice(g0 // tt, g1 // tt),), ((g1 - g0) // tt, tt, d)
        x = x_ref[rows].reshape(g1 - g0, d)
        q = _dot(_rms(x, xg_ref[...]).astype(BF16), wq_ref[...])
        att = []
        for b in range(g0 // tt, -(-g1 // tt)):
            lo, hi = max(b * tt, g0), min((b + 1) * tt, g1)
            heads = []
            for h in range(MEM_HEADS):
                hs = slice(h * dh, (h + 1) * dh)
                sc = _dot_nt(q[lo - g0:hi - g0, hs].astype(BF16), kb[b][:, hs]) * scale
                ex = jnp.exp(sc - jnp.max(sc, axis=-1, keepdims=True))
                p = ex / jnp.sum(ex, axis=-1, keepdims=True)
                heads.append(_dot(p.astype(BF16), vb[b][:, hs]).astype(BF16))
            att.append(jnp.concatenate(heads, axis=1))
        x = x + _dot(jnp.concatenate(att, axis=0), wo_ref[...])
        x = _ffn(x, ng_ref[...], wg_ref[...], wu_ref[...], wd_ref[...])
        o_ref[rows] = _rms(x, fg_ref[...]).reshape(blk)


def _tail_call(x, k, v, xg, wq, wo, ng, wg, wu, wd, fg, nb, tt, n_split):
    bsz, seq, d = x.shape
    n_mem = k.shape[1]
    kern = functools.partial(_tail_kernel, nb, tt, n_split)
    xspec = pl.BlockSpec((nb, tt, d), lambda i, t: (i, t, 0))
    kvspec = pl.BlockSpec((nb, n_mem, d), lambda i, t: (i, 0, 0))
    return pl.pallas_call(
        kern,
        grid=(bsz // nb, seq // tt),
        in_specs=[xspec, kvspec, kvspec]
                 + [_const_spec(a.shape) for a in (xg, wq, wo, ng, wg, wu, wd, fg)],
        out_specs=xspec,
        out_shape=jax.ShapeDtypeStruct(x.shape, F32),
        compiler_params=pltpu.CompilerParams(dimension_semantics=("arbitrary", "arbitrary"),
                                             vmem_limit_bytes=VMEM_LIMIT),
        name="tail",
    )(x, k, v, xg, wq, wo, ng, wg, wu, wd, fg)


def _layer(x, sconv, shgrn, mk, mv, w, fg, ffn_tile, mix_tile, tail_tile):
    bsz, seq, d = x.shape
    x = _ffn_call(x.reshape(bsz * seq, d), w["ffn1_norm"], w["ffn1_w_gate"], w["ffn1_w_up"],
                  w["ffn1_w_down"], *ffn_tile).reshape(bsz, seq, d)
    x, new_conv, new_hgrn = _mix_call(x, w["mix_norm"], w["w_in"], w["conv_dw"], w["conv_dw_b"],
                                      w["conv_ln_g"], w["conv_ln_b"], w["hgrn_lb"], w["hgrn_norm"],
                                      w["w_out"], sconv, shgrn, *mix_tile)
    y = _tail_call(x, mk, mv, w["xattn_norm"], w["xattn_wq"], w["xattn_wo"], w["ffn2_norm"],
                   w["ffn2_w_gate"], w["ffn2_w_up"], w["ffn2_w_down"], fg, *tail_tile)
    return y, new_conv, new_hgrn


def kernel(x_prompt, x_sample, mem_prompt, state_conv, state_hgrn, cache_mem_k, cache_mem_v, ffn1_norm, ffn1_w_gate, ffn1_w_up, ffn1_w_down, mix_norm, w_in, conv_dw, conv_dw_b, conv_ln_g, conv_ln_b, hgrn_lb, hgrn_norm, w_out, mem_norm, mem_wk, mem_wv, xattn_norm, xattn_wq, xattn_wo, ffn2_norm, ffn2_w_gate, ffn2_w_up, ffn2_w_down, final_norm):
    depth = ffn1_norm.shape[0]
    assert depth == 1, "the per-layer forget-gate bound is taken from row 0 of the cumulative softmax"
    bp, seq, d = x_prompt.shape
    bs, dec_seq, _ = x_sample.shape
    n_mem = mem_prompt.shape[1]
    d_hgrn = hgrn_lb.shape[1]
    d_conv = conv_dw.shape[2]
    d_head = d_hgrn // HGRN_HEADS

    def row(a):
        return a.reshape(1, -1)

    w = {
        "ffn1_norm": row(ffn1_norm[0]), "ffn1_w_gate": ffn1_w_gate[0].astype(BF16),
        "ffn1_w_up": ffn1_w_up[0].astype(BF16), "ffn1_w_down": ffn1_w_down[0].astype(BF16),
        "mix_norm": row(mix_norm[0]), "w_in": w_in[0].astype(BF16),
        "conv_dw": conv_dw[0], "conv_dw_b": row(conv_dw_b[0]),
        "conv_ln_g": row(conv_ln_g[0]), "conv_ln_b": row(conv_ln_b[0]),
        "hgrn_lb": hgrn_lb, "hgrn_norm": row(hgrn_norm[0]), "w_out": w_out[0].astype(BF16),
        "xattn_norm": row(xattn_norm[0]), "xattn_wq": xattn_wq[0].astype(BF16),
        "xattn_wo": xattn_wo[0].astype(BF16),
        "ffn2_norm": row(ffn2_norm[0]), "ffn2_w_gate": ffn2_w_gate[0].astype(BF16),
        "ffn2_w_up": ffn2_w_up[0].astype(BF16), "ffn2_w_down": ffn2_w_down[0].astype(BF16),
    }
    fg = row(final_norm)

    mk, mv = _memkv_call(mem_prompt.reshape(bp * n_mem, d), row(mem_norm[0]),
                         mem_wk[0].astype(BF16), mem_wv[0].astype(BF16), 512)
    mk = mk.reshape(bp, n_mem, d)
    mv = mv.reshape(bp, n_mem, d)

    y_p, conv_p, hgrn_p = _layer(
        x_prompt, jnp.zeros((bp, CONV_BUF, d_conv), F32),
        jnp.zeros((bp, HGRN_HEADS, d_head, d_head), F32), mk, mv, w, fg,
        ffn_tile=(512, 2), mix_tile=(1, 256), tail_tile=(1, 512, 2))
    y_s, conv_s, hgrn_s = _layer(
        x_sample, state_conv[0], state_hgrn[0],
        cache_mem_k[0].reshape(bs, n_mem, d), cache_mem_v[0].reshape(bs, n_mem, d), w, fg,
        ffn_tile=(256, 1), mix_tile=(8, dec_seq), tail_tile=(4, dec_seq, 1))

    kv_shape = (1, bp, n_mem, MEM_HEADS, d // MEM_HEADS)
    return (y_p, y_s, conv_p[None], hgrn_p[None], mk.reshape(kv_shape), mv.reshape(kv_shape),
            conv_s[None], hgrn_s[None])
```

```python
import functools

import jax
import jax.numpy as jnp
from jax import lax
from jax.experimental import pallas as pl
from jax.experimental.pallas import tpu as pltpu

F32 = jnp.float32
BF16 = jnp.bfloat16

EPS = 1e-6
CHUNK = 64
CONV_WIDTH = 31
CONV_BUF = CONV_WIDTH - 1
HGRN_HEADS = 4
MEM_HEADS = 4
HALO = 32
CONV_ROWS = 32
SUBLANES = 8
STACK_ROWS = 256
VMEM_LIMIT = 56 * 1024 * 1024


def _dot(a, b):
    return jnp.dot(a, b, preferred_element_type=F32)


def _dot_nt(a, b):
    return lax.dot_general(a, b, (((1,), (1,)), ((), ())), preferred_element_type=F32)


def _dot_tn(a, b):
    return lax.dot_general(a, b, (((0,), (0,)), ((), ())), preferred_element_type=F32)


def _rms(x, g):
    return x * lax.rsqrt(jnp.mean(x * x, axis=-1, keepdims=True) + EPS) * g


def _sigmoid(x):
    return 0.5 * jnp.tanh(0.5 * x) + 0.5


def _silu(x):
    return x * _sigmoid(x)


def _ffn(x, ng, wg, wu, wd):
    xn = _rms(x, ng).astype(BF16)
    h = (_silu(_dot(xn, wg)) * _dot(xn, wu)).astype(BF16)
    return x + 0.5 * _dot(h, wd)


def _const_spec(shape):
    nd = len(shape)
    return pl.BlockSpec(shape, lambda *_: (0,) * nd, pipeline_mode=pl.Buffered(1))


def _row_groups(m, n_split):
    mg = m // n_split
    return [(g * mg, (g + 1) * mg) for g in range(n_split)]


def _ffn_kernel(n_split, x_ref, ng_ref, wg_ref, wu_ref, wd_ref, o_ref):
    for g0, g1 in _row_groups(x_ref.shape[0], n_split):
        o_ref[g0:g1, :] = _ffn(x_ref[g0:g1, :], ng_ref[...], wg_ref[...], wu_ref[...], wd_ref[...])


def _ffn_call(x, ng, wg, wu, wd, tm, n_split):
    m, d = x.shape
    return pl.pallas_call(
        functools.partial(_ffn_kernel, n_split),
        grid=(m // tm,),
        in_specs=[pl.BlockSpec((tm, d), lambda i: (i, 0)),
                  _const_spec(ng.shape), _const_spec(wg.shape), _const_spec(wu.shape),
                  _const_spec(wd.shape)],
        out_specs=pl.BlockSpec((tm, d), lambda i: (i, 0)),
        out_shape=jax.ShapeDtypeStruct((m, d), F32),
        compiler_params=pltpu.CompilerParams(dimension_semantics=("arbitrary",),
                                             vmem_limit_bytes=VMEM_LIMIT),
        name="ffn",
    )(x, ng, wg, wu, wd)


def _memkv_kernel(m_ref, g_ref, wk_ref, wv_ref, k_ref, v_ref, kb_ref, vb_ref):
    nb, n_mem, n_head, dh = k_ref.shape
    mn = _rms(m_ref[...].reshape(nb * n_mem, m_ref.shape[-1]), g_ref[...]).astype(BF16)
    for w_ref, o_ref, ob_ref in ((wk_ref, k_ref, kb_ref), (wv_ref, v_ref, vb_ref)):
        y = _dot(mn, w_ref[...])
        ob_ref[...] = y.astype(BF16).reshape(ob_ref.shape)
        for h in range(n_head):
            o_ref[:, :, h, :] = y[:, h * dh:(h + 1) * dh].reshape(nb, n_mem, dh)


def _memkv_call(mem, g, wk, wv, nb):
    bsz, n_mem, d = mem.shape
    dh = d // MEM_HEADS
    row = pl.BlockSpec((nb, n_mem, d), lambda i: (i, 0, 0))
    kv = pl.BlockSpec((nb, n_mem, MEM_HEADS, dh), lambda i: (i, 0, 0, 0))
    return pl.pallas_call(
        _memkv_kernel,
        grid=(bsz // nb,),
        in_specs=[row, _const_spec(g.shape), _const_spec(wk.shape), _const_spec(wv.shape)],
        out_specs=[kv, kv, row, row],
        out_shape=[jax.ShapeDtypeStruct((bsz, n_mem, MEM_HEADS, dh), F32)] * 2
                  + [jax.ShapeDtypeStruct((bsz, n_mem, d), BF16)] * 2,
        compiler_params=pltpu.CompilerParams(dimension_semantics=("arbitrary",),
                                             vmem_limit_bytes=VMEM_LIMIT),
        name="memkv",
    )(mem, g, wk, wv)


def _hgrn_rows(z, lbv, hn, masks, state, chunk, d_head):
    tt = z.shape[0]
    n_c = tt // chunk
    dh = HGRN_HEADS * d_head
    tri, causal = masks
    zq, zf, v, zg = (z[:, i * dh:(i + 1) * dh] for i in range(4))
    f = lbv + (1.0 - lbv) * _sigmoid(zf)
    q = _silu(zq)
    k = 1.0 - f
    logf = jnp.log(f)
    l_hi = logf.astype(BF16)
    l_lo = (logf - l_hi.astype(F32)).astype(BF16)
    gc = _dot(tri, l_hi) + _dot(tri, l_lo)
    g_last = [gc[(c + 1) * chunk - 1:(c + 1) * chunk, :] for c in range(n_c)]
    g_end = jnp.concatenate([jnp.broadcast_to(g, (chunk, dh)) for g in g_last], axis=0)
    q_dec = (q * jnp.exp(gc)).astype(BF16)
    k_dec = (k * jnp.exp(-gc)).astype(BF16)
    k_end = (k * jnp.exp(g_end - gc)).astype(BF16)
    decay = [jnp.exp(g) for g in g_last]
    vb = v.astype(BF16)
    zero = jnp.zeros((chunk, d_head), BF16)
    out, new_state = [], []
    for h in range(HGRN_HEADS):
        hs = slice(h * d_head, (h + 1) * d_head)
        a = jnp.where(causal, _dot_nt(q_dec[:, hs], k_dec[:, hs]), 0.0).astype(BF16)
        o_intra = _dot(a, vb[:, hs])
        k_blocks = jnp.concatenate(
            [jnp.concatenate([k_end[c * chunk:(c + 1) * chunk, hs] if j == c else zero
                              for j in range(n_c)], axis=1) for c in range(n_c)], axis=0)
        kv = _dot_tn(vb[:, hs], k_blocks)
        st = [state[h]]
        for c in range(n_c):
            st.append(st[c] * decay[c][:, hs] + kv[:, c * d_head:(c + 1) * d_head])
        new_state.append(st[n_c])
        stack = jnp.concatenate([s.astype(BF16) for s in st[:n_c]], axis=0)
        o_inter = _dot_nt(q_dec[:, hs], stack)
        o = o_intra + jnp.concatenate(
            [o_inter[c * chunk:(c + 1) * chunk, c * d_head:(c + 1) * d_head] for c in range(n_c)], axis=0)
        on = o * lax.rsqrt(jnp.mean(o * o, axis=-1, keepdims=True) + EPS) * hn[:, hs]
        out.append((on * _silu(zg[:, hs])).astype(BF16))
    return out, new_state


def _mix_kernel(nb, tt, chunk,
                x_ref, ng_ref, win_ref, cw_ref, cb_ref, lng_ref, lnb_ref, hlb_ref, hn_ref, wout_ref,
                sconv_ref, shgrn_ref,
                o_ref, oconv_ref, ohgrn_ref,
                zbuf, ubuf, ushift, wtap, sbuf, cat):
    t = pl.program_id(1)
    d_conv = cw_ref.shape[1]
    d_head = sbuf.shape[-1]
    m = nb * tt
    first = HALO - CONV_BUF

    @pl.when((pl.program_id(0) == 0) & (t == 0))
    def _spread_taps():
        wtap[...] = jnp.broadcast_to(cw_ref[...][:, None, :], wtap.shape)

    @pl.when(t == 0)
    def _load_state():
        ubuf[:, 0:first, :] = jnp.zeros((nb, first, d_conv), F32)
        ubuf[:, first:HALO, :] = sconv_ref[...]
        for b in range(nb):
            for h in range(HGRN_HEADS):
                sbuf[b, h] = shgrn_ref[b, h].T

    x = x_ref[...].reshape(m, x_ref.shape[-1])
    zbuf[...] = _dot(_rms(x, ng_ref[...]).astype(BF16), win_ref[...])

    for b in range(nb):
        r = slice(b * tt, (b + 1) * tt)
        ubuf[b, HALO:HALO + tt, :] = zbuf[r, 0:d_conv] * _sigmoid(zbuf[r, d_conv:2 * d_conv])
    cb = cb_ref[...]
    lng = lng_ref[...]
    lnb = lnb_ref[...]
    n_shift = ushift.shape[2]
    for r in range(1, SUBLANES):
        ushift[r - 1] = ubuf[:, r:r + n_shift, :]
    for b in range(nb):
        for r0 in range(0, tt, CONV_ROWS):
            acc = jnp.broadcast_to(cb, (CONV_ROWS // SUBLANES, SUBLANES, d_conv))
            for j in range(CONV_WIDTH):
                a8, r = divmod(first + j, SUBLANES)
                lo = r0 + a8 * SUBLANES
                src = ubuf[b, lo:lo + CONV_ROWS, :] if r == 0 else ushift[r - 1, b, lo:lo + CONV_ROWS, :]
                acc = acc + wtap[j] * src.reshape(acc.shape)
            acc = acc.reshape(CONV_ROWS, d_conv)
            mu = jnp.mean(acc, axis=-1, keepdims=True)
            dev = acc - mu
            var = jnp.mean(dev * dev, axis=-1, keepdims=True)
            yc = _silu(dev * lax.rsqrt(var + EPS) * lng + lnb)
            cat[b * tt + r0:b * tt + r0 + CONV_ROWS, 0:d_conv] = yc.astype(BF16)
    new_rows = ubuf[:, tt + first:tt + HALO, :]
    ubuf[:, first:HALO, :] = new_rows

    hlb = hlb_ref[...]
    e = jnp.exp(hlb - jnp.max(hlb, axis=0, keepdims=True))
    lbv = e[0:1, :] / jnp.sum(e, axis=0, keepdims=True)
    shift = chunk.bit_length() - 1
    assert chunk == 1 << shift
    row_i = lax.broadcasted_iota(jnp.int32, (tt, tt), 0)
    col_i = lax.broadcasted_iota(jnp.int32, (tt, tt), 1)
    causal = (row_i >= col_i) & (lax.shift_right_logical(row_i, shift) == lax.shift_right_logical(col_i, shift))
    masks = (jnp.where(causal, 1.0, 0.0).astype(BF16), causal)
    for b in range(nb):
        r = slice(b * tt, (b + 1) * tt)
        heads, state = _hgrn_rows(zbuf[r, 2 * d_conv:], lbv, hn_ref[...], masks,
                                  [sbuf[b, h] for h in range(HGRN_HEADS)], chunk, d_head)
        for h in range(HGRN_HEADS):
            sbuf[b, h] = state[h]
            cat[r, d_conv + h * d_head:d_conv + (h + 1) * d_head] = heads[h]

    o_ref[...] = (x + _dot(cat[...], wout_ref[...])).reshape(o_ref.shape)

    @pl.when(t == pl.num_programs(1) - 1)
    def _store_state():
        oconv_ref[...] = ubuf[:, first:HALO, :]
        for b in range(nb):
            for h in range(HGRN_HEADS):
                ohgrn_ref[b, h] = sbuf[b, h].T


def _mix_call(x, ng, win, cw, cb, lng, lnb, hlb, hn, wout, sconv, shgrn, nb, tt):
    bsz, seq, d = x.shape
    d_conv = cw.shape[1]
    d_head = shgrn.shape[-1]
    chunk = min(CHUNK, seq)
    kern = functools.partial(_mix_kernel, nb, tt, chunk)
    xspec = pl.BlockSpec((nb, tt, d), lambda i, t: (i, t, 0))
    cspec = pl.BlockSpec((nb, CONV_BUF, d_conv), lambda i, t: (i, 0, 0))
    sspec = pl.BlockSpec((nb, HGRN_HEADS, d_head, d_head), lambda i, t: (i, 0, 0, 0))
    return pl.pallas_call(
        kern,
        grid=(bsz // nb, seq // tt),
        in_specs=[xspec] + [_const_spec(a.shape) for a in (ng, win, cw, cb, lng, lnb, hlb, hn, wout)]
                 + [cspec, sspec],
        out_specs=[xspec, cspec, sspec],
        out_shape=[jax.ShapeDtypeStruct(x.shape, F32),
                   jax.ShapeDtypeStruct(sconv.shape, F32),
                   jax.ShapeDtypeStruct(shgrn.shape, F32)],
        scratch_shapes=[pltpu.VMEM((nb * tt, win.shape[1]), F32),
                        pltpu.VMEM((nb, HALO + tt, d_conv), F32),
                        pltpu.VMEM((SUBLANES - 1, nb, HALO + tt - SUBLANES, d_conv), F32),
                        pltpu.VMEM((CONV_WIDTH, SUBLANES, d_conv), F32),
                        pltpu.VMEM((nb, HGRN_HEADS, d_head, d_head), F32),
                        pltpu.VMEM((nb * tt, d), BF16)],
        compiler_params=pltpu.CompilerParams(dimension_semantics=("arbitrary", "arbitrary"),
                                             vmem_limit_bytes=VMEM_LIMIT),
        name="mix",
    )(x, ng, win, cw, cb, lng, lnb, hlb, hn, wout, sconv, shgrn)


def _tail_kernel(nb, tt, n_split,
                 x_ref, k_ref, v_ref, xg_ref, wq_ref, wo_ref, ng_ref, wg_ref, wu_ref, wd_ref, fg_ref,
                 o_ref):
    d = x_ref.shape[-1]
    dh = d // MEM_HEADS
    scale = dh ** -0.5
    kb = [k_ref[b].astype(BF16) for b in range(nb)]
    vb = [v_ref[b].astype(BF16) for b in range(nb)]
    for g0, g1 in _row_groups(nb * tt, n_split):
        if nb == 1:
            rows, blk = (0, slice(g0, g1)), (g1 - g0, d)
        else:
            assert g0 % tt == 0 and g1 % tt == 0
            rows, blk = (slice(g0 // tt, g1 // tt),), ((g1 - g0) // tt, tt, d)
        x = x_ref[rows].reshape(g1 - g0, d)
        q = _dot(_rms(x, xg_ref[...]).astype(BF16), wq_ref[...])
        att = []
        for b in range(g0 // tt, -(-g1 // tt)):
            lo, hi = max(b * tt, g0), min((b + 1) * tt, g1)
            qb = q[lo - g0:hi - g0, :].astype(BF16)
            hsl = [slice(h * dh, (h + 1) * dh) for h in range(MEM_HEADS)]
            if (hi - lo) * MEM_HEADS <= STACK_ROWS:
                zq = jnp.zeros((hi - lo, dh), BF16)
                q_stack = jnp.concatenate(
                    [jnp.concatenate([qb[:, hs] if j == h else zq for j in range(MEM_HEADS)], axis=1)
                     for h, hs in enumerate(hsl)], axis=0)
                sc = _dot_nt(q_stack, kb[b]) * scale
                ex = jnp.exp(sc - jnp.max(sc, axis=-1, keepdims=True))
                p = ex / jnp.sum(ex, axis=-1, keepdims=True)
                pv = _dot(p.astype(BF16), vb[b])
                heads = [pv[h * (hi - lo):(h + 1) * (hi - lo), hs].astype(BF16) for h, hs in enumerate(hsl)]
            else:
                heads = []
                for hs in hsl:
                    sc = _dot_nt(qb[:, hs], kb[b][:, hs]) * scale
                    ex = jnp.exp(sc - jnp.max(sc, axis=-1, keepdims=True))
                    p = ex / jnp.sum(ex, axis=-1, keepdims=True)
                    heads.append(_dot(p.astype(BF16), vb[b][:, hs]).astype(BF16))
            att.append(jnp.concatenate(heads, axis=1))
        x = x + _dot(jnp.concatenate(att, axis=0), wo_ref[...])
        x = _ffn(x, ng_ref[...], wg_ref[...], wu_ref[...], wd_ref[...])
        o_ref[rows] = _rms(x, fg_ref[...]).reshape(blk)


def _tail_call(x, k, v, xg, wq, wo, ng, wg, wu, wd, fg, nb, tt, n_split):
    bsz, seq, d = x.shape
    kern = functools.partial(_tail_kernel, nb, tt, n_split)
    xspec = pl.BlockSpec((nb, tt, d), lambda i, t: (i, t, 0))
    kvspec = pl.BlockSpec((nb,) + k.shape[1:], lambda i, t: (i, 0, 0))
    return pl.pallas_call(
        kern,
        grid=(bsz // nb, seq // tt),
        in_specs=[xspec, kvspec, kvspec]
                 + [_const_spec(a.shape) for a in (xg, wq, wo, ng, wg, wu, wd, fg)],
        out_specs=xspec,
        out_shape=jax.ShapeDtypeStruct(x.shape, F32),
        compiler_params=pltpu.CompilerParams(dimension_semantics=("arbitrary", "arbitrary"),
                                             vmem_limit_bytes=VMEM_LIMIT),
        name="tail",
    )(x, k, v, xg, wq, wo, ng, wg, wu, wd, fg)


def _layer(x, sconv, shgrn, mk, mv, w, fg, ffn_tile, mix_tile, tail_tile):
    bsz, seq, d = x.shape
    x = _ffn_call(x.reshape(bsz * seq, d), w["ffn1_norm"], w["ffn1_w_gate"], w["ffn1_w_up"],
                  w["ffn1_w_down"], *ffn_tile).reshape(bsz, seq, d)
    x, new_conv, new_hgrn = _mix_call(x, w["mix_norm"], w["w_in"], w["conv_dw"], w["conv_dw_b"],
                                      w["conv_ln_g"], w["conv_ln_b"], w["hgrn_lb"], w["hgrn_norm"],
                                      w["w_out"], sconv, shgrn, *mix_tile)
    y = _tail_call(x, mk, mv, w["xattn_norm"], w["xattn_wq"], w["xattn_wo"], w["ffn2_norm"],
                   w["ffn2_w_gate"], w["ffn2_w_up"], w["ffn2_w_down"], fg, *tail_tile)
    return y, new_conv, new_hgrn


def kernel(x_prompt, x_sample, mem_prompt, state_conv, state_hgrn, cache_mem_k, cache_mem_v, ffn1_norm, ffn1_w_gate, ffn1_w_up, ffn1_w_down, mix_norm, w_in, conv_dw, conv_dw_b, conv_ln_g, conv_ln_b, hgrn_lb, hgrn_norm, w_out, mem_norm, mem_wk, mem_wv, xattn_norm, xattn_wq, xattn_wo, ffn2_norm, ffn2_w_gate, ffn2_w_up, ffn2_w_down, final_norm):
    depth = ffn1_norm.shape[0]
    assert depth == 1, "the per-layer forget-gate bound is taken from row 0 of the cumulative softmax"
    bp, seq, d = x_prompt.shape
    bs, dec_seq, _ = x_sample.shape
    n_mem = mem_prompt.shape[1]
    d_hgrn = hgrn_lb.shape[1]
    d_conv = conv_dw.shape[2]
    d_head = d_hgrn // HGRN_HEADS

    def row(a):
        return a.reshape(1, -1)

    w = {
        "ffn1_norm": row(ffn1_norm[0]), "ffn1_w_gate": ffn1_w_gate[0].astype(BF16),
        "ffn1_w_up": ffn1_w_up[0].astype(BF16), "ffn1_w_down": ffn1_w_down[0].astype(BF16),
        "mix_norm": row(mix_norm[0]), "w_in": w_in[0].astype(BF16),
        "conv_dw": conv_dw[0], "conv_dw_b": row(conv_dw_b[0]),
        "conv_ln_g": row(conv_ln_g[0]), "conv_ln_b": row(conv_ln_b[0]),
        "hgrn_lb": hgrn_lb, "hgrn_norm": row(hgrn_norm[0]), "w_out": w_out[0].astype(BF16),
        "xattn_norm": row(xattn_norm[0]), "xattn_wq": xattn_wq[0].astype(BF16),
        "xattn_wo": xattn_wo[0].astype(BF16),
        "ffn2_norm": row(ffn2_norm[0]), "ffn2_w_gate": ffn2_w_gate[0].astype(BF16),
        "ffn2_w_up": ffn2_w_up[0].astype(BF16), "ffn2_w_down": ffn2_w_down[0].astype(BF16),
    }
    fg = row(final_norm)

    mk, mv, mk_flat, mv_flat = _memkv_call(mem_prompt, row(mem_norm[0]),
                                           mem_wk[0].astype(BF16), mem_wv[0].astype(BF16), 2)

    y_p, conv_p, hgrn_p = _layer(
        x_prompt, jnp.zeros((bp, CONV_BUF, d_conv), F32),
        jnp.zeros((bp, HGRN_HEADS, d_head, d_head), F32), mk_flat, mv_flat, w, fg,
        ffn_tile=(512, 2), mix_tile=(1, 256), tail_tile=(1, 512, 2))
    y_s, conv_s, hgrn_s = _layer(
        x_sample, state_conv[0], state_hgrn[0],
        cache_mem_k[0].reshape(bs, n_mem, d).astype(BF16),
        cache_mem_v[0].reshape(bs, n_mem, d).astype(BF16), w, fg,
        ffn_tile=(256, 1), mix_tile=(8, dec_seq), tail_tile=(8, dec_seq, 1))

    return (y_p, y_s, conv_p[None], hgrn_p[None], mk[None], mv[None],
            conv_s[None], hgrn_s[None])
```

```python
import functools

import jax
import jax.numpy as jnp
from jax import lax
from jax.experimental import pallas as pl
from jax.experimental.pallas import tpu as pltpu

F32 = jnp.float32
BF16 = jnp.bfloat16

EPS = 1e-6
CHUNK = 64
CONV_WIDTH = 31
CONV_BUF = CONV_WIDTH - 1
HGRN_HEADS = 4
MEM_HEADS = 4
HALO = 32
CONV_ROWS = 32
SUBLANES = 8
BF16_ROWS = 16
STACK_ROWS = 256
VMEM_LIMIT = 56 * 1024 * 1024


def _dot(a, b):
    return jnp.dot(a, b, preferred_element_type=F32)


def _dot_nt(a, b):
    return lax.dot_general(a, b, (((1,), (1,)), ((), ())), preferred_element_type=F32)


def _dot_tn(a, b):
    return lax.dot_general(a, b, (((0,), (0,)), ((), ())), preferred_element_type=F32)


def _rms(x, g):
    return x * lax.rsqrt(jnp.mean(x * x, axis=-1, keepdims=True) + EPS) * g


def _sigmoid(x):
    return 0.5 * jnp.tanh(0.5 * x) + 0.5


def _silu(x):
    return x * _sigmoid(x)


def _ffn(x, ng, wg, wu, wd):
    xn = _rms(x, ng).astype(BF16)
    h = (_silu(_dot(xn, wg)) * _dot(xn, wu)).astype(BF16)
    return x + 0.5 * _dot(h, wd)


def _const_spec(shape):
    nd = len(shape)
    return pl.BlockSpec(shape, lambda *_: (0,) * nd, pipeline_mode=pl.Buffered(1))


def _row_groups(m, n_split):
    mg = m // n_split
    return [(g * mg, (g + 1) * mg) for g in range(n_split)]


def _ffn_kernel(n_split, n_cast, x_ref, ng_ref, wg_ref, wu_ref, wd_ref, *rest):
    cast_in, o_ref, cast_out = rest[:n_cast], rest[n_cast], rest[n_cast + 1:]
    for g0, g1 in _row_groups(x_ref.shape[0], n_split):
        o_ref[g0:g1, :] = _ffn(x_ref[g0:g1, :], ng_ref[...], wg_ref[...], wu_ref[...], wd_ref[...])
    for src, dst in zip(cast_in, cast_out):
        dst[...] = src[...].astype(BF16)


def _cast_blocks(rows, n_steps):
    return max(n for n in range(1, n_steps + 1) if rows % (n * BF16_ROWS) == 0)


def _ffn_call(x, ng, wg, wu, wd, tm, n_split, cast=()):
    m, d = x.shape
    n_steps = m // tm
    cast_specs = []
    for a in cast:
        n_blk = _cast_blocks(a.shape[0], n_steps)
        cast_specs.append(pl.BlockSpec((a.shape[0] // n_blk, a.shape[1]),
                                       lambda i, n_blk=n_blk: (jnp.minimum(i, n_blk - 1), 0)))
    out = pl.pallas_call(
        functools.partial(_ffn_kernel, n_split, len(cast)),
        grid=(n_steps,),
        in_specs=[pl.BlockSpec((tm, d), lambda i: (i, 0)),
                  _const_spec(ng.shape), _const_spec(wg.shape), _const_spec(wu.shape),
                  _const_spec(wd.shape)] + cast_specs,
        out_specs=[pl.BlockSpec((tm, d), lambda i: (i, 0))] + cast_specs,
        out_shape=[jax.ShapeDtypeStruct((m, d), F32)]
                  + [jax.ShapeDtypeStruct(a.shape, BF16) for a in cast],
        compiler_params=pltpu.CompilerParams(dimension_semantics=("arbitrary",),
                                             vmem_limit_bytes=VMEM_LIMIT),
        name="ffn",
    )(x, ng, wg, wu, wd, *cast)
    return out[0], out[1:]


def _memkv_kernel(m_ref, g_ref, wk_ref, wv_ref, k_ref, v_ref, kb_ref, vb_ref):
    nb, n_mem, n_head, dh = k_ref.shape
    mn = _rms(m_ref[...].reshape(nb * n_mem, m_ref.shape[-1]), g_ref[...]).astype(BF16)
    for w_ref, o_ref, ob_ref in ((wk_ref, k_ref, kb_ref), (wv_ref, v_ref, vb_ref)):
        y = _dot(mn, w_ref[...])
        ob_ref[...] = y.astype(BF16).reshape(ob_ref.shape)
        for h in range(n_head):
            o_ref[:, :, h, :] = y[:, h * dh:(h + 1) * dh].reshape(nb, n_mem, dh)


def _memkv_call(mem, g, wk, wv, nb):
    bsz, n_mem, d = mem.shape
    dh = d // MEM_HEADS
    row = pl.BlockSpec((nb, n_mem, d), lambda i: (i, 0, 0))
    kv = pl.BlockSpec((nb, n_mem, MEM_HEADS, dh), lambda i: (i, 0, 0, 0))
    return pl.pallas_call(
        _memkv_kernel,
        grid=(bsz // nb,),
        in_specs=[row, _const_spec(g.shape), _const_spec(wk.shape), _const_spec(wv.shape)],
        out_specs=[kv, kv, row, row],
        out_shape=[jax.ShapeDtypeStruct((bsz, n_mem, MEM_HEADS, dh), F32)] * 2
                  + [jax.ShapeDtypeStruct((bsz, n_mem, d), BF16)] * 2,
        compiler_params=pltpu.CompilerParams(dimension_semantics=("arbitrary",),
                                             vmem_limit_bytes=VMEM_LIMIT),
        name="memkv",
    )(mem, g, wk, wv)


def _hgrn_rows(z, lbv, hn, masks, state, chunk, d_head):
    tt = z.shape[0]
    n_c = tt // chunk
    dh = HGRN_HEADS * d_head
    tri, causal = masks
    zq, zf, v, zg = (z[:, i * dh:(i + 1) * dh] for i in range(4))
    f = lbv + (1.0 - lbv) * _sigmoid(zf)
    q = _silu(zq)
    k = 1.0 - f
    logf = jnp.log(f)
    l_hi = logf.astype(BF16)
    l_lo = (logf - l_hi.astype(F32)).astype(BF16)
    gc = _dot(tri, l_hi) + _dot(tri, l_lo)
    g_last = [gc[(c + 1) * chunk - 1:(c + 1) * chunk, :] for c in range(n_c)]
    g_end = jnp.concatenate([jnp.broadcast_to(g, (chunk, dh)) for g in g_last], axis=0)
    q_dec = (q * jnp.exp(gc)).astype(BF16)
    k_dec = (k * jnp.exp(-gc)).astype(BF16)
    k_end = (k * jnp.exp(g_end - gc)).astype(BF16)
    decay = [jnp.exp(g) for g in g_last]
    vb = v.astype(BF16)
    zero = jnp.zeros((chunk, d_head), BF16)
    out, new_state = [], []
    for h in range(HGRN_HEADS):
        hs = slice(h * d_head, (h + 1) * d_head)
        a = jnp.where(causal, _dot_nt(q_dec[:, hs], k_dec[:, hs]), 0.0).astype(BF16)
        o_intra = _dot(a, vb[:, hs])
        k_blocks = jnp.concatenate(
            [jnp.concatenate([k_end[c * chunk:(c + 1) * chunk, hs] if j == c else zero
                              for j in range(n_c)], axis=1) for c in range(n_c)], axis=0)
        kv = _dot_tn(vb[:, hs], k_blocks)
        st = [state[h]]
        for c in range(n_c):
            st.append(st[c] * decay[c][:, hs] + kv[:, c * d_head:(c + 1) * d_head])
        new_state.append(st[n_c])
        stack = jnp.concatenate([s.astype(BF16) for s in st[:n_c]], axis=0)
        o_inter = _dot_nt(q_dec[:, hs], stack)
        o = o_intra + jnp.concatenate(
            [o_inter[c * chunk:(c + 1) * chunk, c * d_head:(c + 1) * d_head] for c in range(n_c)], axis=0)
        on = o * lax.rsqrt(jnp.mean(o * o, axis=-1, keepdims=True) + EPS) * hn[:, hs]
        out.append((on * _silu(zg[:, hs])).astype(BF16))
    return out, new_state


def _mix_kernel(nb, tt, chunk,
                x_ref, ng_ref, win_ref, cw_ref, cb_ref, lng_ref, lnb_ref, hlb_ref, hn_ref, wout_ref,
                sconv_ref, shgrn_ref,
                o_ref, oconv_ref, ohgrn_ref,
                zbuf, ubuf, ushift, wtap, sbuf, cat):
    t = pl.program_id(1)
    d_conv = cw_ref.shape[1]
    d_head = sbuf.shape[-1]
    m = nb * tt
    first = HALO - CONV_BUF

    @pl.when((pl.program_id(0) == 0) & (t == 0))
    def _spread_taps():
        wtap[...] = jnp.broadcast_to(cw_ref[...][:, None, :], wtap.shape)

    @pl.when(t == 0)
    def _load_state():
        ubuf[:, 0:first, :] = jnp.zeros((nb, first, d_conv), F32)
        ubuf[:, first:HALO, :] = sconv_ref[...]
        for b in range(nb):
            for h in range(HGRN_HEADS):
                sbuf[b, h] = shgrn_ref[b, h].T

    x = x_ref[...].reshape(m, x_ref.shape[-1])
    zbuf[...] = _dot(_rms(x, ng_ref[...]).astype(BF16), win_ref[...])

    for b in range(nb):
        r = slice(b * tt, (b + 1) * tt)
        ubuf[b, HALO:HALO + tt, :] = zbuf[r, 0:d_conv] * _sigmoid(zbuf[r, d_conv:2 * d_conv])
    cb = cb_ref[...]
    lng = lng_ref[...]
    lnb = lnb_ref[...]
    n_shift = ushift.shape[2]
    for r in range(1, SUBLANES):
        ushift[r - 1] = ubuf[:, r:r + n_shift, :]
    for b in range(nb):
        for r0 in range(0, tt, CONV_ROWS):
            acc = jnp.broadcast_to(cb, (CONV_ROWS // SUBLANES, SUBLANES, d_conv))
            for j in range(CONV_WIDTH):
                a8, r = divmod(first + j, SUBLANES)
                lo = r0 + a8 * SUBLANES
                src = ubuf[b, lo:lo + CONV_ROWS, :] if r == 0 else ushift[r - 1, b, lo:lo + CONV_ROWS, :]
                acc = acc + wtap[j] * src.reshape(acc.shape)
            acc = acc.reshape(CONV_ROWS, d_conv)
            mu = jnp.mean(acc, axis=-1, keepdims=True)
            dev = acc - mu
            var = jnp.mean(dev * dev, axis=-1, keepdims=True)
            yc = _silu(dev * lax.rsqrt(var + EPS) * lng + lnb)
            cat[b * tt + r0:b * tt + r0 + CONV_ROWS, 0:d_conv] = yc.astype(BF16)
    new_rows = ubuf[:, tt + first:tt + HALO, :]
    ubuf[:, first:HALO, :] = new_rows

    hlb = hlb_ref[...]
    e = jnp.exp(hlb - jnp.max(hlb, axis=0, keepdims=True))
    lbv = e[0:1, :] / jnp.sum(e, axis=0, keepdims=True)
    shift = chunk.bit_length() - 1
    assert chunk == 1 << shift
    row_i = lax.broadcasted_iota(jnp.int32, (tt, tt), 0)
    col_i = lax.broadcasted_iota(jnp.int32, (tt, tt), 1)
    causal = (row_i >= col_i) & (lax.shift_right_logical(row_i, shift) == lax.shift_right_logical(col_i, shift))
    masks = (jnp.where(causal, 1.0, 0.0).astype(BF16), causal)
    for b in range(nb):
        r = slice(b * tt, (b + 1) * tt)
        heads, state = _hgrn_rows(zbuf[r, 2 * d_conv:], lbv, hn_ref[...], masks,
                                  [sbuf[b, h] for h in range(HGRN_HEADS)], chunk, d_head)
        for h in range(HGRN_HEADS):
            sbuf[b, h] = state[h]
            cat[r, d_conv + h * d_head:d_conv + (h + 1) * d_head] = heads[h]

    o_ref[...] = (x + _dot(cat[...], wout_ref[...])).reshape(o_ref.shape)

    @pl.when(t == pl.num_programs(1) - 1)
    def _store_state():
        oconv_ref[...] = ubuf[:, first:HALO, :]
        for b in range(nb):
            for h in range(HGRN_HEADS):
                ohgrn_ref[b, h] = sbuf[b, h].T


def _mix_call(x, ng, win, cw, cb, lng, lnb, hlb, hn, wout, sconv, shgrn, nb, tt):
    bsz, seq, d = x.shape
    d_conv = cw.shape[1]
    d_head = shgrn.shape[-1]
    chunk = min(CHUNK, seq)
    kern = functools.partial(_mix_kernel, nb, tt, chunk)
    xspec = pl.BlockSpec((nb, tt, d), lambda i, t: (i, t, 0))
    cspec = pl.BlockSpec((nb, CONV_BUF, d_conv), lambda i, t: (i, 0, 0))
    sspec = pl.BlockSpec((nb, HGRN_HEADS, d_head, d_head), lambda i, t: (i, 0, 0, 0))
    return pl.pallas_call(
        kern,
        grid=(bsz // nb, seq // tt),
        in_specs=[xspec] + [_const_spec(a.shape) for a in (ng, win, cw, cb, lng, lnb, hlb, hn, wout)]
                 + [cspec, sspec],
        out_specs=[xspec, cspec, sspec],
        out_shape=[jax.ShapeDtypeStruct(x.shape, F32),
                   jax.ShapeDtypeStruct(sconv.shape, F32),
                   jax.ShapeDtypeStruct(shgrn.shape, F32)],
        scratch_shapes=[pltpu.VMEM((nb * tt, win.shape[1]), F32),
                        pltpu.VMEM((nb, HALO + tt, d_conv), F32),
                        pltpu.VMEM((SUBLANES - 1, nb, HALO + tt - SUBLANES, d_conv), F32),
                        pltpu.VMEM((CONV_WIDTH, SUBLANES, d_conv), F32),
                        pltpu.VMEM((nb, HGRN_HEADS, d_head, d_head), F32),
                        pltpu.VMEM((nb * tt, d), BF16)],
        compiler_params=pltpu.CompilerParams(dimension_semantics=("arbitrary", "arbitrary"),
                                             vmem_limit_bytes=VMEM_LIMIT),
        name="mix",
    )(x, ng, win, cw, cb, lng, lnb, hlb, hn, wout, sconv, shgrn)


def _tail_kernel(nb, tt, n_split,
                 x_ref, k_ref, v_ref, xg_ref, wq_ref, wo_ref, ng_ref, wg_ref, wu_ref, wd_ref, fg_ref,
                 o_ref):
    d = x_ref.shape[-1]
    dh = d // MEM_HEADS
    scale = dh ** -0.5
    kb = [k_ref[b].astype(BF16) for b in range(nb)]
    vb = [v_ref[b].astype(BF16) for b in range(nb)]
    for g0, g1 in _row_groups(nb * tt, n_split):
        if nb == 1:
            rows, blk = (0, slice(g0, g1)), (g1 - g0, d)
        else:
            assert g0 % tt == 0 and g1 % tt == 0
            rows, blk = (slice(g0 // tt, g1 // tt),), ((g1 - g0) // tt, tt, d)
        x = x_ref[rows].reshape(g1 - g0, d)
        q = _dot(_rms(x, xg_ref[...]).astype(BF16), wq_ref[...])
        att = []
        for b in range(g0 // tt, -(-g1 // tt)):
            lo, hi = max(b * tt, g0), min((b + 1) * tt, g1)
            qb = q[lo - g0:hi - g0, :].astype(BF16)
            hsl = [slice(h * dh, (h + 1) * dh) for h in range(MEM_HEADS)]
            if (hi - lo) * MEM_HEADS <= STACK_ROWS:
                zq = jnp.zeros((hi - lo, dh), BF16)
                q_stack = jnp.concatenate(
                    [jnp.concatenate([qb[:, hs] if j == h else zq for j in range(MEM_HEADS)], axis=1)
                     for h, hs in enumerate(hsl)], axis=0)
                sc = _dot_nt(q_stack, kb[b]) * scale
                ex = jnp.exp(sc - jnp.max(sc, axis=-1, keepdims=True))
                p = ex / jnp.sum(ex, axis=-1, keepdims=True)
                pv = _dot(p.astype(BF16), vb[b])
                heads = [pv[h * (hi - lo):(h + 1) * (hi - lo), hs].astype(BF16) for h, hs in enumerate(hsl)]
            else:
                heads = []
                for hs in hsl:
                    sc = _dot_nt(qb[:, hs], kb[b][:, hs]) * scale
                    ex = jnp.exp(sc - jnp.max(sc, axis=-1, keepdims=True))
                    p = ex / jnp.sum(ex, axis=-1, keepdims=True)
                    heads.append(_dot(p.astype(BF16), vb[b][:, hs]).astype(BF16))
            att.append(jnp.concatenate(heads, axis=1))
        x = x + _dot(jnp.concatenate(att, axis=0), wo_ref[...])
        x = _ffn(x, ng_ref[...], wg_ref[...], wu_ref[...], wd_ref[...])
        o_ref[rows] = _rms(x, fg_ref[...]).reshape(blk)


def _tail_call(x, k, v, xg, wq, wo, ng, wg, wu, wd, fg, nb, tt, n_split):
    bsz, seq, d = x.shape
    kern = functools.partial(_tail_kernel, nb, tt, n_split)
    xspec = pl.BlockSpec((nb, tt, d), lambda i, t: (i, t, 0))
    kvspec = pl.BlockSpec((nb,) + k.shape[1:], lambda i, t: (i, 0, 0))
    return pl.pallas_call(
        kern,
        grid=(bsz // nb, seq // tt),
        in_specs=[xspec, kvspec, kvspec]
                 + [_const_spec(a.shape) for a in (xg, wq, wo, ng, wg, wu, wd, fg)],
        out_specs=xspec,
        out_shape=jax.ShapeDtypeStruct(x.shape, F32),
        compiler_params=pltpu.CompilerParams(dimension_semantics=("arbitrary", "arbitrary"),
                                             vmem_limit_bytes=VMEM_LIMIT),
        name="tail",
    )(x, k, v, xg, wq, wo, ng, wg, wu, wd, fg)


def _mix_tail(x, sconv, shgrn, mk, mv, w, fg, mix_tile, tail_tile):
    x, new_conv, new_hgrn = _mix_call(x, w["mix_norm"], w["w_in"], w["conv_dw"], w["conv_dw_b"],
                                      w["conv_ln_g"], w["conv_ln_b"], w["hgrn_lb"], w["hgrn_norm"],
                                      w["w_out"], sconv, shgrn, *mix_tile)
    y = _tail_call(x, mk, mv, w["xattn_norm"], w["xattn_wq"], w["xattn_wo"], w["ffn2_norm"],
                   w["ffn2_w_gate"], w["ffn2_w_up"], w["ffn2_w_down"], fg, *tail_tile)
    return y, new_conv, new_hgrn


def kernel(x_prompt, x_sample, mem_prompt, state_conv, state_hgrn, cache_mem_k, cache_mem_v, ffn1_norm, ffn1_w_gate, ffn1_w_up, ffn1_w_down, mix_norm, w_in, conv_dw, conv_dw_b, conv_ln_g, conv_ln_b, hgrn_lb, hgrn_norm, w_out, mem_norm, mem_wk, mem_wv, xattn_norm, xattn_wq, xattn_wo, ffn2_norm, ffn2_w_gate, ffn2_w_up, ffn2_w_down, final_norm):
    depth = ffn1_norm.shape[0]
    assert depth == 1, "the per-layer forget-gate bound is taken from row 0 of the cumulative softmax"
    bp, seq, d = x_prompt.shape
    bs, dec_seq, _ = x_sample.shape
    n_mem = mem_prompt.shape[1]
    d_hgrn = hgrn_lb.shape[1]
    d_conv = conv_dw.shape[2]
    d_head = d_hgrn // HGRN_HEADS

    def row(a):
        return a.reshape(1, -1)

    fg = row(final_norm)
    mk, mv, mk_flat, mv_flat = _memkv_call(mem_prompt, row(mem_norm[0]),
                                           mem_wk[0].astype(BF16), mem_wv[0].astype(BF16), 2)

    ffn1 = (row(ffn1_norm[0]), ffn1_w_gate[0].astype(BF16), ffn1_w_up[0].astype(BF16),
            ffn1_w_down[0].astype(BF16))
    later = {"w_in": w_in[0], "w_out": w_out[0], "xattn_wq": xattn_wq[0], "xattn_wo": xattn_wo[0],
             "ffn2_w_gate": ffn2_w_gate[0], "ffn2_w_up": ffn2_w_up[0], "ffn2_w_down": ffn2_w_down[0]}
    xp, later_bf16 = _ffn_call(x_prompt.reshape(bp * seq, d), *ffn1, 512, 2, cast=tuple(later.values()))
    xs, _ = _ffn_call(x_sample.reshape(bs * dec_seq, d), *ffn1, 256, 1)
    w = dict(zip(later, later_bf16))
    w.update({
        "mix_norm": row(mix_norm[0]), "conv_dw": conv_dw[0], "conv_dw_b": row(conv_dw_b[0]),
        "conv_ln_g": row(conv_ln_g[0]), "conv_ln_b": row(conv_ln_b[0]),
        "hgrn_lb": hgrn_lb, "hgrn_norm": row(hgrn_norm[0]),
        "xattn_norm": row(xattn_norm[0]), "ffn2_norm": row(ffn2_norm[0]),
    })

    y_p, conv_p, hgrn_p = _mix_tail(
        xp.reshape(bp, seq, d), jnp.zeros((bp, CONV_BUF, d_conv), F32),
        jnp.zeros((bp, HGRN_HEADS, d_head, d_head), F32), mk_flat, mv_flat, w, fg,
        mix_tile=(1, 256), tail_tile=(1, 512, 2))
    y_s, conv_s, hgrn_s = _mix_tail(
        xs.reshape(bs, dec_seq, d), state_conv[0], state_hgrn[0],
        cache_mem_k[0].reshape(bs, n_mem, d), cache_mem_v[0].reshape(bs, n_mem, d), w, fg,
        mix_tile=(8, dec_seq), tail_tile=(4, dec_seq, 1))

    return (y_p, y_s, conv_p[None], hgrn_p[None], mk[None], mv[None],
            conv_s[None], hgrn_s[None])
```

```python
import functools

import jax
import jax.numpy as jnp
from jax import lax
from jax.experimental import pallas as pl
from jax.experimental.pallas import tpu as pltpu

F32 = jnp.float32
BF16 = jnp.bfloat16

EPS = 1e-6
CHUNK = 64
CONV_WIDTH = 31
CONV_BUF = CONV_WIDTH - 1
HGRN_HEADS = 4
MEM_HEADS = 4
HALO = 32
CONV_ROWS = 32
SUBLANES = 8
BF16_ROWS = 16
HGRN_ROWS = 256
STACK_ROWS = 256
VMEM_LIMIT = 56 * 1024 * 1024


def _dot(a, b):
    return jnp.dot(a, b, preferred_element_type=F32)


def _dot_nt(a, b):
    return lax.dot_general(a, b, (((1,), (1,)), ((), ())), preferred_element_type=F32)


def _dot_tn(a, b):
    return lax.dot_general(a, b, (((0,), (0,)), ((), ())), preferred_element_type=F32)


def _rms(x, g):
    return x * lax.rsqrt(jnp.mean(x * x, axis=-1, keepdims=True) + EPS) * g


def _sigmoid(x):
    return 0.5 * jnp.tanh(0.5 * x) + 0.5


def _silu(x):
    return x * _sigmoid(x)


def _ffn(x, ng, wg, wu, wd):
    xn = _rms(x, ng).astype(BF16)
    h = (_silu(_dot(xn, wg)) * _dot(xn, wu)).astype(BF16)
    return x + 0.5 * _dot(h, wd)


def _const_spec(shape):
    nd = len(shape)
    return pl.BlockSpec(shape, lambda *_: (0,) * nd, pipeline_mode=pl.Buffered(1))


def _row_groups(m, n_split):
    mg = m // n_split
    return [(g * mg, (g + 1) * mg) for g in range(n_split)]


def _cast_specs(arrays, n_steps):
    specs = []
    for a in arrays:
        n_blk = max(n for n in range(1, n_steps + 1) if a.shape[0] % (n * BF16_ROWS) == 0)
        specs.append(pl.BlockSpec((a.shape[0] // n_blk, a.shape[1]),
                                  lambda i, n_blk=n_blk: (jnp.minimum(i, n_blk - 1), 0)))
    return specs


def _cast_rows(src_refs, dst_refs):
    for src, dst in zip(src_refs, dst_refs):
        dst[...] = src[...].astype(BF16)


def _ffn_kernel(n_split, n_cast, x_ref, ng_ref, wg_ref, wu_ref, wd_ref, *rest):
    cast_in, o_ref, cast_out = rest[:n_cast], rest[n_cast], rest[n_cast + 1:]
    for g0, g1 in _row_groups(x_ref.shape[0], n_split):
        o_ref[g0:g1, :] = _ffn(x_ref[g0:g1, :], ng_ref[...], wg_ref[...], wu_ref[...], wd_ref[...])
    _cast_rows(cast_in, cast_out)


def _ffn_call(x, ng, wg, wu, wd, tm, n_split, cast=()):
    m, d = x.shape
    n_steps = m // tm
    cast_specs = _cast_specs(cast, n_steps)
    out = pl.pallas_call(
        functools.partial(_ffn_kernel, n_split, len(cast)),
        grid=(n_steps,),
        in_specs=[pl.BlockSpec((tm, d), lambda i: (i, 0)),
                  _const_spec(ng.shape), _const_spec(wg.shape), _const_spec(wu.shape),
                  _const_spec(wd.shape)] + cast_specs,
        out_specs=[pl.BlockSpec((tm, d), lambda i: (i, 0))] + cast_specs,
        out_shape=[jax.ShapeDtypeStruct((m, d), F32)]
                  + [jax.ShapeDtypeStruct(a.shape, BF16) for a in cast],
        compiler_params=pltpu.CompilerParams(dimension_semantics=("arbitrary",),
                                             vmem_limit_bytes=VMEM_LIMIT),
        name="ffn",
    )(x, ng, wg, wu, wd, *cast)
    return out[0], out[1:]


def _memkv_kernel(n_cast, m_ref, g_ref, wk_ref, wv_ref, *rest):
    cast_in, (k_ref, v_ref, kb_ref, vb_ref), cast_out = rest[:n_cast], rest[n_cast:n_cast + 4], rest[n_cast + 4:]
    _cast_rows(cast_in, cast_out)
    nb, n_mem, n_head, dh = k_ref.shape
    mn = _rms(m_ref[...].reshape(nb * n_mem, m_ref.shape[-1]), g_ref[...]).astype(BF16)
    for w_ref, o_ref, ob_ref in ((wk_ref, k_ref, kb_ref), (wv_ref, v_ref, vb_ref)):
        y = _dot(mn, w_ref[...])
        ob_ref[...] = y.astype(BF16).reshape(ob_ref.shape)
        for h in range(n_head):
            o_ref[:, :, h, :] = y[:, h * dh:(h + 1) * dh].reshape(nb, n_mem, dh)


def _memkv_call(mem, g, wk, wv, nb, cast=()):
    bsz, n_mem, d = mem.shape
    dh = d // MEM_HEADS
    n_steps = bsz // nb
    row = pl.BlockSpec((nb, n_mem, d), lambda i: (i, 0, 0))
    kv = pl.BlockSpec((nb, n_mem, MEM_HEADS, dh), lambda i: (i, 0, 0, 0))
    cast_specs = _cast_specs(cast, n_steps)
    out = pl.pallas_call(
        functools.partial(_memkv_kernel, len(cast)),
        grid=(n_steps,),
        in_specs=[row, _const_spec(g.shape), _const_spec(wk.shape), _const_spec(wv.shape)] + cast_specs,
        out_specs=[kv, kv, row, row] + cast_specs,
        out_shape=[jax.ShapeDtypeStruct((bsz, n_mem, MEM_HEADS, dh), F32)] * 2
                  + [jax.ShapeDtypeStruct((bsz, n_mem, d), BF16)] * 2
                  + [jax.ShapeDtypeStruct(a.shape, BF16) for a in cast],
        compiler_params=pltpu.CompilerParams(dimension_semantics=("arbitrary",),
                                             vmem_limit_bytes=VMEM_LIMIT),
        name="memkv",
    )(mem, g, wk, wv, *cast)
    return out[:4], out[4:]


def _hgrn_rows(z, lbv, hn, masks, state, chunk, d_head):
    tt = z.shape[0]
    n_c = tt // chunk
    dh = HGRN_HEADS * d_head
    tri, causal = masks
    zq, zf, v, zg = (z[:, i * dh:(i + 1) * dh] for i in range(4))
    f = lbv + (1.0 - lbv) * _sigmoid(zf)
    q = _silu(zq)
    k = 1.0 - f
    logf = jnp.log(f)
    l_hi = logf.astype(BF16)
    l_lo = (logf - l_hi.astype(F32)).astype(BF16)
    gc = _dot(tri, l_hi) + _dot(tri, l_lo)
    g_last = [gc[(c + 1) * chunk - 1:(c + 1) * chunk, :] for c in range(n_c)]
    g_end = jnp.concatenate([jnp.broadcast_to(g, (chunk, dh)) for g in g_last], axis=0)
    q_dec = (q * jnp.exp(gc)).astype(BF16)
    k_dec = (k * jnp.exp(-gc)).astype(BF16)
    k_end = (k * jnp.exp(g_end - gc)).astype(BF16)
    decay = [jnp.exp(g) for g in g_last]
    vb = v.astype(BF16)
    zero = jnp.zeros((chunk, d_head), BF16)
    out, new_state = [], []
    for h in range(HGRN_HEADS):
        hs = slice(h * d_head, (h + 1) * d_head)
        a = jnp.where(causal, _dot_nt(q_dec[:, hs], k_dec[:, hs]), 0.0).astype(BF16)
        o_intra = _dot(a, vb[:, hs])
        k_blocks = jnp.concatenate(
            [jnp.concatenate([k_end[c * chunk:(c + 1) * chunk, hs] if j == c else zero
                              for j in range(n_c)], axis=1) for c in range(n_c)], axis=0)
        kv = _dot_tn(vb[:, hs], k_blocks)
        st = [state[h]]
        for c in range(n_c):
            st.append(st[c] * decay[c][:, hs] + kv[:, c * d_head:(c + 1) * d_head])
        new_state.append(st[n_c])
        stack = jnp.concatenate([s.astype(BF16) for s in st[:n_c]], axis=0)
        o_inter = _dot_nt(q_dec[:, hs], stack)
        o = o_intra + jnp.concatenate(
            [o_inter[c * chunk:(c + 1) * chunk, c * d_head:(c + 1) * d_head] for c in range(n_c)], axis=0)
        on = o * lax.rsqrt(jnp.mean(o * o, axis=-1, keepdims=True) + EPS) * hn[:, hs]
        out.append((on * _silu(zg[:, hs])).astype(BF16))
    return out, new_state


def _mix_kernel(nb, tt, chunk,
                x_ref, ng_ref, win_ref, cw_ref, cb_ref, lng_ref, lnb_ref, hlb_ref, hn_ref, wout_ref,
                sconv_ref, shgrn_ref,
                o_ref, oconv_ref, ohgrn_ref,
                zbuf, ubuf, ushift, wtap, sbuf, cat):
    t = pl.program_id(1)
    d_conv = cw_ref.shape[1]
    d_head = sbuf.shape[-1]
    m = nb * tt
    first = HALO - CONV_BUF

    @pl.when((pl.program_id(0) == 0) & (t == 0))
    def _spread_taps():
        wtap[...] = jnp.broadcast_to(cw_ref[...][:, None, :], wtap.shape)

    @pl.when(t == 0)
    def _load_state():
        ubuf[:, 0:first, :] = jnp.zeros((nb, first, d_conv), F32)
        ubuf[:, first:HALO, :] = sconv_ref[...]
        for b in range(nb):
            for h in range(HGRN_HEADS):
                sbuf[b, h] = shgrn_ref[b, h].T

    x = x_ref[...].reshape(m, x_ref.shape[-1])
    zbuf[...] = _dot(_rms(x, ng_ref[...]).astype(BF16), win_ref[...])

    for b in range(nb):
        r = slice(b * tt, (b + 1) * tt)
        ubuf[b, HALO:HALO + tt, :] = zbuf[r, 0:d_conv] * _sigmoid(zbuf[r, d_conv:2 * d_conv])
    cb = cb_ref[...]
    lng = lng_ref[...]
    lnb = lnb_ref[...]
    n_shift = ushift.shape[2]
    for r in range(1, SUBLANES):
        ushift[r - 1] = ubuf[:, r:r + n_shift, :]
    for b in range(nb):
        for r0 in range(0, tt, CONV_ROWS):
            acc = jnp.broadcast_to(cb, (CONV_ROWS // SUBLANES, SUBLANES, d_conv))
            for j in range(CONV_WIDTH):
                a8, r = divmod(first + j, SUBLANES)
                lo = r0 + a8 * SUBLANES
                src = ubuf[b, lo:lo + CONV_ROWS, :] if r == 0 else ushift[r - 1, b, lo:lo + CONV_ROWS, :]
                acc = acc + wtap[j] * src.reshape(acc.shape)
            acc = acc.reshape(CONV_ROWS, d_conv)
            mu = jnp.mean(acc, axis=-1, keepdims=True)
            dev = acc - mu
            var = jnp.mean(dev * dev, axis=-1, keepdims=True)
            yc = _silu(dev * lax.rsqrt(var + EPS) * lng + lnb)
            cat[b * tt + r0:b * tt + r0 + CONV_ROWS, 0:d_conv] = yc.astype(BF16)
    new_rows = ubuf[:, tt + first:tt + HALO, :]
    ubuf[:, first:HALO, :] = new_rows

    hlb = hlb_ref[...]
    e = jnp.exp(hlb - jnp.max(hlb, axis=0, keepdims=True))
    lbv = e[0:1, :] / jnp.sum(e, axis=0, keepdims=True)
    shift = chunk.bit_length() - 1
    assert chunk == 1 << shift
    sub = min(tt, HGRN_ROWS)
    row_i = lax.broadcasted_iota(jnp.int32, (sub, sub), 0)
    col_i = lax.broadcasted_iota(jnp.int32, (sub, sub), 1)
    causal = (row_i >= col_i) & (lax.shift_right_logical(row_i, shift) == lax.shift_right_logical(col_i, shift))
    masks = (jnp.where(causal, 1.0, 0.0).astype(BF16), causal)
    for b in range(nb):
        state = [sbuf[b, h] for h in range(HGRN_HEADS)]
        for r0 in range(b * tt, (b + 1) * tt, sub):
            r = slice(r0, r0 + sub)
            heads, state = _hgrn_rows(zbuf[r, 2 * d_conv:], lbv, hn_ref[...], masks, state, chunk, d_head)
            for h in range(HGRN_HEADS):
                cat[r, d_conv + h * d_head:d_conv + (h + 1) * d_head] = heads[h]
        for h in range(HGRN_HEADS):
            sbuf[b, h] = state[h]

    o_ref[...] = (x + _dot(cat[...], wout_ref[...])).reshape(o_ref.shape)

    @pl.when(t == pl.num_programs(1) - 1)
    def _store_state():
        oconv_ref[...] = ubuf[:, first:HALO, :]
        for b in range(nb):
            for h in range(HGRN_HEADS):
                ohgrn_ref[b, h] = sbuf[b, h].T


def _mix_call(x, ng, win, cw, cb, lng, lnb, hlb, hn, wout, sconv, shgrn, nb, tt):
    bsz, seq, d = x.shape
    d_conv = cw.shape[1]
    d_head = shgrn.shape[-1]
    chunk = min(CHUNK, seq)
    kern = functools.partial(_mix_kernel, nb, tt, chunk)
    xspec = pl.BlockSpec((nb, tt, d), lambda i, t: (i, t, 0))
    cspec = pl.BlockSpec((nb, CONV_BUF, d_conv), lambda i, t: (i, 0, 0))
    sspec = pl.BlockSpec((nb, HGRN_HEADS, d_head, d_head), lambda i, t: (i, 0, 0, 0))
    return pl.pallas_call(
        kern,
        grid=(bsz // nb, seq // tt),
        in_specs=[xspec] + [_const_spec(a.shape) for a in (ng, win, cw, cb, lng, lnb, hlb, hn, wout)]
                 + [cspec, sspec],
        out_specs=[xspec, cspec, sspec],
        out_shape=[jax.ShapeDtypeStruct(x.shape, F32),
                   jax.ShapeDtypeStruct(sconv.shape, F32),
                   jax.ShapeDtypeStruct(shgrn.shape, F32)],
        scratch_shapes=[pltpu.VMEM((nb * tt, win.shape[1]), F32),
                        pltpu.VMEM((nb, HALO + tt, d_conv), F32),
                        pltpu.VMEM((SUBLANES - 1, nb, HALO + tt - SUBLANES, d_conv), F32),
                        pltpu.VMEM((CONV_WIDTH, SUBLANES, d_conv), F32),
                        pltpu.VMEM((nb, HGRN_HEADS, d_head, d_head), F32),
                        pltpu.VMEM((nb * tt, d), BF16)],
        compiler_params=pltpu.CompilerParams(dimension_semantics=("arbitrary", "arbitrary"),
                                             vmem_limit_bytes=VMEM_LIMIT),
        name="mix",
    )(x, ng, win, cw, cb, lng, lnb, hlb, hn, wout, sconv, shgrn)


def _tail_kernel(nb, tt, n_split,
                 x_ref, k_ref, v_ref, xg_ref, wq_ref, wo_ref, ng_ref, wg_ref, wu_ref, wd_ref, fg_ref,
                 o_ref):
    d = x_ref.shape[-1]
    dh = d // MEM_HEADS
    scale = dh ** -0.5
    kb = [k_ref[b].astype(BF16) for b in range(nb)]
    vb = [v_ref[b].astype(BF16) for b in range(nb)]
    for g0, g1 in _row_groups(nb * tt, n_split):
        if nb == 1:
            rows, blk = (0, slice(g0, g1)), (g1 - g0, d)
        else:
            assert g0 % tt == 0 and g1 % tt == 0
            rows, blk = (slice(g0 // tt, g1 // tt),), ((g1 - g0) // tt, tt, d)
        x = x_ref[rows].reshape(g1 - g0, d)
        q = _dot(_rms(x, xg_ref[...]).astype(BF16), wq_ref[...])
        att = []
        for b in range(g0 // tt, -(-g1 // tt)):
            lo, hi = max(b * tt, g0), min((b + 1) * tt, g1)
            qb = q[lo - g0:hi - g0, :].astype(BF16)
            hsl = [slice(h * dh, (h + 1) * dh) for h in range(MEM_HEADS)]
            if (hi - lo) * MEM_HEADS <= STACK_ROWS:
                zq = jnp.zeros((hi - lo, dh), BF16)
                q_stack = jnp.concatenate(
                    [jnp.concatenate([qb[:, hs] if j == h else zq for j in range(MEM_HEADS)], axis=1)
                     for h, hs in enumerate(hsl)], axis=0)
                sc = _dot_nt(q_stack, kb[b]) * scale
                ex = jnp.exp(sc - jnp.max(sc, axis=-1, keepdims=True))
                p = ex / jnp.sum(ex, axis=-1, keepdims=True)
                pv = _dot(p.astype(BF16), vb[b])
                heads = [pv[h * (hi - lo):(h + 1) * (hi - lo), hs].astype(BF16) for h, hs in enumerate(hsl)]
            else:
                heads = []
                for hs in hsl:
                    sc = _dot_nt(qb[:, hs], kb[b][:, hs]) * scale
                    ex = jnp.exp(sc - jnp.max(sc, axis=-1, keepdims=True))
                    p = ex / jnp.sum(ex, axis=-1, keepdims=True)
                    heads.append(_dot(p.astype(BF16), vb[b][:, hs]).astype(BF16))
            att.append(jnp.concatenate(heads, axis=1))
        x = x + _dot(jnp.concatenate(att, axis=0), wo_ref[...])
        x = _ffn(x, ng_ref[...], wg_ref[...], wu_ref[...], wd_ref[...])
        o_ref[rows] = _rms(x, fg_ref[...]).reshape(blk)


def _tail_call(x, k, v, xg, wq, wo, ng, wg, wu, wd, fg, nb, tt, n_split):
    bsz, seq, d = x.shape
    kern = functools.partial(_tail_kernel, nb, tt, n_split)
    xspec = pl.BlockSpec((nb, tt, d), lambda i, t: (i, t, 0))
    kvspec = pl.BlockSpec((nb,) + k.shape[1:], lambda i, t: (i, 0, 0))
    return pl.pallas_call(
        kern,
        grid=(bsz // nb, seq // tt),
        in_specs=[xspec, kvspec, kvspec]
                 + [_const_spec(a.shape) for a in (xg, wq, wo, ng, wg, wu, wd, fg)],
        out_specs=xspec,
        out_shape=jax.ShapeDtypeStruct(x.shape, F32),
        compiler_params=pltpu.CompilerParams(dimension_semantics=("arbitrary", "arbitrary"),
                                             vmem_limit_bytes=VMEM_LIMIT),
        name="tail",
    )(x, k, v, xg, wq, wo, ng, wg, wu, wd, fg)


def _mix_tail(x, sconv, shgrn, mk, mv, w, fg, mix_tile, tail_tile):
    x, new_conv, new_hgrn = _mix_call(x, w["mix_norm"], w["w_in"], w["conv_dw"], w["conv_dw_b"],
                                      w["conv_ln_g"], w["conv_ln_b"], w["hgrn_lb"], w["hgrn_norm"],
                                      w["w_out"], sconv, shgrn, *mix_tile)
    y = _tail_call(x, mk, mv, w["xattn_norm"], w["xattn_wq"], w["xattn_wo"], w["ffn2_norm"],
                   w["ffn2_w_gate"], w["ffn2_w_up"], w["ffn2_w_down"], fg, *tail_tile)
    return y, new_conv, new_hgrn


def kernel(x_prompt, x_sample, mem_prompt, state_conv, state_hgrn, cache_mem_k, cache_mem_v, ffn1_norm, ffn1_w_gate, ffn1_w_up, ffn1_w_down, mix_norm, w_in, conv_dw, conv_dw_b, conv_ln_g, conv_ln_b, hgrn_lb, hgrn_norm, w_out, mem_norm, mem_wk, mem_wv, xattn_norm, xattn_wq, xattn_wo, ffn2_norm, ffn2_w_gate, ffn2_w_up, ffn2_w_down, final_norm):
    depth = ffn1_norm.shape[0]
    assert depth == 1, "the per-layer forget-gate bound is taken from row 0 of the cumulative softmax"
    bp, seq, d = x_prompt.shape
    bs, dec_seq, _ = x_sample.shape
    n_mem = mem_prompt.shape[1]
    d_hgrn = hgrn_lb.shape[1]
    d_conv = conv_dw.shape[2]
    d_head = d_hgrn // HGRN_HEADS

    def row(a):
        return a.reshape(1, -1)

    fg = row(final_norm)
    (mk, mv, mk_flat, mv_flat), ffn1_w = _memkv_call(
        mem_prompt, row(mem_norm[0]), mem_wk[0].astype(BF16), mem_wv[0].astype(BF16), 2,
        cast=(ffn1_w_gate[0], ffn1_w_up[0], ffn1_w_down[0]))

    ffn1 = (row(ffn1_norm[0]), *ffn1_w)
    later = {"w_in": w_in[0], "w_out": w_out[0], "xattn_wq": xattn_wq[0], "xattn_wo": xattn_wo[0],
             "ffn2_w_gate": ffn2_w_gate[0], "ffn2_w_up": ffn2_w_up[0], "ffn2_w_down": ffn2_w_down[0]}
    xp, later_bf16 = _ffn_call(x_prompt.reshape(bp * seq, d), *ffn1, 1024, 4, cast=tuple(later.values()))
    xs, _ = _ffn_call(x_sample.reshape(bs * dec_seq, d), *ffn1, 256, 1)
    w = dict(zip(later, later_bf16))
    w.update({
        "mix_norm": row(mix_norm[0]), "conv_dw": conv_dw[0], "conv_dw_b": row(conv_dw_b[0]),
        "conv_ln_g": row(conv_ln_g[0]), "conv_ln_b": row(conv_ln_b[0]),
        "hgrn_lb": hgrn_lb, "hgrn_norm": row(hgrn_norm[0]),
        "xattn_norm": row(xattn_norm[0]), "ffn2_norm": row(ffn2_norm[0]),
    })

    y_p, conv_p, hgrn_p = _mix_tail(
        xp.reshape(bp, seq, d), jnp.zeros((bp, CONV_BUF, d_conv), F32),
        jnp.zeros((bp, HGRN_HEADS, d_head, d_head), F32), mk_flat, mv_flat, w, fg,
        mix_tile=(1, 512), tail_tile=(1, 1024, 2))
    y_s, conv_s, hgrn_s = _mix_tail(
        xs.reshape(bs, dec_seq, d), state_conv[0], state_hgrn[0],
        cache_mem_k[0].reshape(bs, n_mem, d), cache_mem_v[0].reshape(bs, n_mem, d), w, fg,
        mix_tile=(8, dec_seq), tail_tile=(4, dec_seq, 1))

    return (y_p, y_s, conv_p[None], hgrn_p[None], mk[None], mv[None],
            conv_s[None], hgrn_s[None])
```

```python
import functools

import jax
import jax.numpy as jnp
from jax import lax
from jax.experimental import pallas as pl
from jax.experimental.pallas import tpu as pltpu

F32 = jnp.float32
BF16 = jnp.bfloat16

EPS = 1e-6
CHUNK = 64
CONV_WIDTH = 31
CONV_BUF = CONV_WIDTH - 1
HGRN_HEADS = 4
MEM_HEADS = 4
HALO = 32
CONV_ROWS = 32
SUBLANES = 8
BF16_ROWS = 16
HGRN_ROWS = 256
STACK_ROWS = 256
VMEM_LIMIT = 56 * 1024 * 1024


def _dot(a, b):
    return jnp.dot(a, b, preferred_element_type=F32)


def _dot_nt(a, b):
    return lax.dot_general(a, b, (((1,), (1,)), ((), ())), preferred_element_type=F32)


def _dot_tn(a, b):
    return lax.dot_general(a, b, (((0,), (0,)), ((), ())), preferred_element_type=F32)


def _rms(x, g):
    return x * lax.rsqrt(jnp.mean(x * x, axis=-1, keepdims=True) + EPS) * g


def _sigmoid(x):
    return 0.5 * jnp.tanh(0.5 * x) + 0.5


def _silu(x):
    return x * _sigmoid(x)


def _ffn(x, ng, wg, wu, wd):
    xn = _rms(x, ng).astype(BF16)
    h = (_silu(_dot(xn, wg)) * _dot(xn, wu)).astype(BF16)
    return x + 0.5 * _dot(h, wd)


def _const_spec(shape):
    nd = len(shape)
    return pl.BlockSpec(shape, lambda *_: (0,) * nd, pipeline_mode=pl.Buffered(1))


def _row_groups(m, n_split):
    mg = m // n_split
    return [(g * mg, (g + 1) * mg) for g in range(n_split)]


def _cast_specs(arrays, n_steps):
    specs = []
    for a in arrays:
        n_blk = max(n for n in range(1, n_steps + 1) if a.shape[0] % (n * BF16_ROWS) == 0)
        specs.append(pl.BlockSpec((a.shape[0] // n_blk, a.shape[1]),
                                  lambda i, n_blk=n_blk: (jnp.minimum(i, n_blk - 1), 0)))
    return specs


def _cast_rows(src_refs, dst_refs):
    for src, dst in zip(src_refs, dst_refs):
        dst[...] = src[...].astype(BF16)


def _ffn_kernel(n_split, n_cast, x_ref, ng_ref, wg_ref, wu_ref, wd_ref, *rest):
    cast_in, o_ref, cast_out = rest[:n_cast], rest[n_cast], rest[n_cast + 1:]
    for g0, g1 in _row_groups(x_ref.shape[0], n_split):
        o_ref[g0:g1, :] = _ffn(x_ref[g0:g1, :], ng_ref[...], wg_ref[...], wu_ref[...], wd_ref[...])
    _cast_rows(cast_in, cast_out)


def _ffn_call(x, ng, wg, wu, wd, tm, n_split, cast=()):
    m, d = x.shape
    n_steps = m // tm
    cast_specs = _cast_specs(cast, n_steps)
    out = pl.pallas_call(
        functools.partial(_ffn_kernel, n_split, len(cast)),
        grid=(n_steps,),
        in_specs=[pl.BlockSpec((tm, d), lambda i: (i, 0)),
                  _const_spec(ng.shape), _const_spec(wg.shape), _const_spec(wu.shape),
                  _const_spec(wd.shape)] + cast_specs,
        out_specs=[pl.BlockSpec((tm, d), lambda i: (i, 0))] + cast_specs,
        out_shape=[jax.ShapeDtypeStruct((m, d), F32)]
                  + [jax.ShapeDtypeStruct(a.shape, BF16) for a in cast],
        compiler_params=pltpu.CompilerParams(dimension_semantics=("arbitrary",),
                                             vmem_limit_bytes=VMEM_LIMIT),
        name="ffn",
    )(x, ng, wg, wu, wd, *cast)
    return out[0], out[1:]


def _memkv_kernel(n_cast, m_ref, g_ref, wk_ref, wv_ref, *rest):
    cast_in, (k_ref, v_ref, kb_ref, vb_ref), cast_out = rest[:n_cast], rest[n_cast:n_cast + 4], rest[n_cast + 4:]
    _cast_rows(cast_in, cast_out)
    nb, n_mem, n_head, dh = k_ref.shape
    mn = _rms(m_ref[...].reshape(nb * n_mem, m_ref.shape[-1]), g_ref[...]).astype(BF16)
    for w_ref, o_ref, ob_ref in ((wk_ref, k_ref, kb_ref), (wv_ref, v_ref, vb_ref)):
        y = _dot(mn, w_ref[...])
        ob_ref[...] = y.astype(BF16).reshape(ob_ref.shape)
        for h in range(n_head):
            o_ref[:, :, h, :] = y[:, h * dh:(h + 1) * dh].reshape(nb, n_mem, dh)


def _memkv_call(mem, g, wk, wv, nb, cast=()):
    bsz, n_mem, d = mem.shape
    dh = d // MEM_HEADS
    n_steps = bsz // nb
    row = pl.BlockSpec((nb, n_mem, d), lambda i: (i, 0, 0))
    kv = pl.BlockSpec((nb, n_mem, MEM_HEADS, dh), lambda i: (i, 0, 0, 0))
    cast_specs = _cast_specs(cast, n_steps)
    out = pl.pallas_call(
        functools.partial(_memkv_kernel, len(cast)),
        grid=(n_steps,),
        in_specs=[row, _const_spec(g.shape), _const_spec(wk.shape), _const_spec(wv.shape)] + cast_specs,
        out_specs=[kv, kv, row, row] + cast_specs,
        out_shape=[jax.ShapeDtypeStruct((bsz, n_mem, MEM_HEADS, dh), F32)] * 2
                  + [jax.ShapeDtypeStruct((bsz, n_mem, d), BF16)] * 2
                  + [jax.ShapeDtypeStruct(a.shape, BF16) for a in cast],
        compiler_params=pltpu.CompilerParams(dimension_semantics=("arbitrary",),
                                             vmem_limit_bytes=VMEM_LIMIT),
        name="memkv",
    )(mem, g, wk, wv, *cast)
    return out[:4], out[4:]


def _hgrn_rows(z, lbv, hn, masks, state, chunk, d_head):
    tt = z.shape[0]
    n_c = tt // chunk
    dh = HGRN_HEADS * d_head
    tri, causal = masks
    zq, zf, v, zg = (z[:, i * dh:(i + 1) * dh] for i in range(4))
    f = lbv + (1.0 - lbv) * _sigmoid(zf)
    q = _silu(zq)
    k = 1.0 - f
    logf = jnp.log(f)
    l_hi = logf.astype(BF16)
    l_lo = (logf - l_hi.astype(F32)).astype(BF16)
    gc = _dot(tri, l_hi) + _dot(tri, l_lo)
    g_last = [gc[(c + 1) * chunk - 1:(c + 1) * chunk, :] for c in range(n_c)]
    g_end = jnp.concatenate([jnp.broadcast_to(g, (chunk, dh)) for g in g_last], axis=0)
    q_dec = (q * jnp.exp(gc)).astype(BF16)
    k_dec = (k * jnp.exp(-gc)).astype(BF16)
    k_end = (k * jnp.exp(g_end - gc)).astype(BF16)
    decay = [jnp.exp(g) for g in g_last]
    vb = v.astype(BF16)
    zero = jnp.zeros((chunk, d_head), BF16)
    out, new_state = [], []
    for h in range(HGRN_HEADS):
        hs = slice(h * d_head, (h + 1) * d_head)
        a = jnp.where(causal, _dot_nt(q_dec[:, hs], k_dec[:, hs]), 0.0).astype(BF16)
        o_intra = _dot(a, vb[:, hs])
        k_blocks = jnp.concatenate(
            [jnp.concatenate([k_end[c * chunk:(c + 1) * chunk, hs] if j == c else zero
                              for j in range(n_c)], axis=1) for c in range(n_c)], axis=0)
        kv = _dot_tn(vb[:, hs], k_blocks)
        st = [state[h]]
        for c in range(n_c):
            st.append(st[c] * decay[c][:, hs] + kv[:, c * d_head:(c + 1) * d_head])
        new_state.append(st[n_c])
        stack = jnp.concatenate([s.astype(BF16) for s in st[:n_c]], axis=0)
        o_inter = _dot_nt(q_dec[:, hs], stack)
        o = o_intra + jnp.concatenate(
            [o_inter[c * chunk:(c + 1) * chunk, c * d_head:(c + 1) * d_head] for c in range(n_c)], axis=0)
        on = o * lax.rsqrt(jnp.mean(o * o, axis=-1, keepdims=True) + EPS) * hn[:, hs]
        out.append((on * _silu(zg[:, hs])).astype(BF16))
    return out, new_state


def _mix_kernel(nb, tt, chunk,
                x_ref, ng_ref, win_ref, cw_ref, cb_ref, lng_ref, lnb_ref, hlb_ref, hn_ref, wout_ref,
                sconv_ref, shgrn_ref,
                o_ref, oconv_ref, ohgrn_ref,
                zbuf, ubuf, ushift, wtap, sbuf, cat):
    t = pl.program_id(1)
    d_conv = cw_ref.shape[1]
    d_head = sbuf.shape[-1]
    m = nb * tt
    first = HALO - CONV_BUF

    @pl.when((pl.program_id(0) == 0) & (t == 0))
    def _spread_taps():
        wtap[...] = jnp.broadcast_to(cw_ref[...][:, None, :], wtap.shape)

    @pl.when(t == 0)
    def _load_state():
        ubuf[:, 0:first, :] = jnp.zeros((nb, first, d_conv), F32)
        ubuf[:, first:HALO, :] = sconv_ref[...]
        for b in range(nb):
            for h in range(HGRN_HEADS):
                sbuf[b, h] = shgrn_ref[b, h].T

    x = x_ref[...].reshape(m, x_ref.shape[-1])
    zbuf[...] = _dot(_rms(x, ng_ref[...]).astype(BF16), win_ref[...])

    for b in range(nb):
        r = slice(b * tt, (b + 1) * tt)
        ubuf[b, HALO:HALO + tt, :] = zbuf[r, 0:d_conv] * _sigmoid(zbuf[r, d_conv:2 * d_conv])
    cb = cb_ref[...]
    lng = lng_ref[...]
    lnb = lnb_ref[...]
    n_shift = ushift.shape[2]
    for r in range(1, SUBLANES):
        ushift[r - 1] = ubuf[:, r:r + n_shift, :]
    for b in range(nb):
        for r0 in range(0, tt, CONV_ROWS):
            acc = jnp.broadcast_to(cb, (CONV_ROWS // SUBLANES, SUBLANES, d_conv))
            for j in range(CONV_WIDTH):
                a8, r = divmod(first + j, SUBLANES)
                lo = r0 + a8 * SUBLANES
                src = ubuf[b, lo:lo + CONV_ROWS, :] if r == 0 else ushift[r - 1, b, lo:lo + CONV_ROWS, :]
                acc = acc + wtap[j] * src.reshape(acc.shape)
            acc = acc.reshape(CONV_ROWS, d_conv)
            mu = jnp.mean(acc, axis=-1, keepdims=True)
            dev = acc - mu
            var = jnp.mean(dev * dev, axis=-1, keepdims=True)
            yc = _silu(dev * lax.rsqrt(var + EPS) * lng + lnb)
            cat[b * tt + r0:b * tt + r0 + CONV_ROWS, 0:d_conv] = yc.astype(BF16)
    new_rows = ubuf[:, tt + first:tt + HALO, :]
    ubuf[:, first:HALO, :] = new_rows

    hlb = hlb_ref[...]
    e = jnp.exp(hlb - jnp.max(hlb, axis=0, keepdims=True))
    lbv = e[0:1, :] / jnp.sum(e, axis=0, keepdims=True)
    shift = chunk.bit_length() - 1
    assert chunk == 1 << shift
    sub = min(tt, HGRN_ROWS)
    row_i = lax.broadcasted_iota(jnp.int32, (sub, sub), 0)
    col_i = lax.broadcasted_iota(jnp.int32, (sub, sub), 1)
    causal = (row_i >= col_i) & (lax.shift_right_logical(row_i, shift) == lax.shift_right_logical(col_i, shift))
    masks = (jnp.where(causal, 1.0, 0.0).astype(BF16), causal)
    for b in range(nb):
        state = [sbuf[b, h] for h in range(HGRN_HEADS)]
        for r0 in range(b * tt, (b + 1) * tt, sub):
            r = slice(r0, r0 + sub)
            heads, state = _hgrn_rows(zbuf[r, 2 * d_conv:], lbv, hn_ref[...], masks, state, chunk, d_head)
            for h in range(HGRN_HEADS):
                cat[r, d_conv + h * d_head:d_conv + (h + 1) * d_head] = heads[h]
        for h in range(HGRN_HEADS):
            sbuf[b, h] = state[h]

    o_ref[...] = (x + _dot(cat[...], wout_ref[...])).reshape(o_ref.shape)

    @pl.when(t == pl.num_programs(1) - 1)
    def _store_state():
        oconv_ref[...] = ubuf[:, first:HALO, :]
        for b in range(nb):
            for h in range(HGRN_HEADS):
                ohgrn_ref[b, h] = sbuf[b, h].T


def _mix_call(x, ng, win, cw, cb, lng, lnb, hlb, hn, wout, sconv, shgrn, nb, tt):
    bsz, seq, d = x.shape
    d_conv = cw.shape[1]
    d_head = shgrn.shape[-1]
    chunk = min(CHUNK, seq)
    kern = functools.partial(_mix_kernel, nb, tt, chunk)
    xspec = pl.BlockSpec((nb, tt, d), lambda i, t: (i, t, 0))
    cspec = pl.BlockSpec((nb, CONV_BUF, d_conv), lambda i, t: (i, 0, 0))
    sspec = pl.BlockSpec((nb, HGRN_HEADS, d_head, d_head), lambda i, t: (i, 0, 0, 0))
    return pl.pallas_call(
        kern,
        grid=(bsz // nb, seq // tt),
        in_specs=[xspec] + [_const_spec(a.shape) for a in (ng, win, cw, cb, lng, lnb, hlb, hn, wout)]
                 + [cspec, sspec],
        out_specs=[xspec, cspec, sspec],
        out_shape=[jax.ShapeDtypeStruct(x.shape, F32),
                   jax.ShapeDtypeStruct(sconv.shape, F32),
                   jax.ShapeDtypeStruct(shgrn.shape, F32)],
        scratch_shapes=[pltpu.VMEM((nb * tt, win.shape[1]), F32),
                        pltpu.VMEM((nb, HALO + tt, d_conv), F32),
                        pltpu.VMEM((SUBLANES - 1, nb, HALO + tt - SUBLANES, d_conv), F32),
                        pltpu.VMEM((CONV_WIDTH, SUBLANES, d_conv), F32),
                        pltpu.VMEM((nb, HGRN_HEADS, d_head, d_head), F32),
                        pltpu.VMEM((nb * tt, d), BF16)],
        compiler_params=pltpu.CompilerParams(dimension_semantics=("arbitrary", "arbitrary"),
                                             vmem_limit_bytes=VMEM_LIMIT),
        name="mix",
    )(x, ng, win, cw, cb, lng, lnb, hlb, hn, wout, sconv, shgrn)


def _tail_kernel(nb, tt, n_split,
                 x_ref, k_ref, v_ref, xg_ref, wq_ref, wo_ref, ng_ref, wg_ref, wu_ref, wd_ref, fg_ref,
                 o_ref):
    d = x_ref.shape[-1]
    dh = d // MEM_HEADS
    scale = dh ** -0.5
    kb = [k_ref[b].astype(BF16) for b in range(nb)]
    vb = [v_ref[b].astype(BF16) for b in range(nb)]
    for g0, g1 in _row_groups(nb * tt, n_split):
        if nb == 1:
            rows, blk = (0, slice(g0, g1)), (g1 - g0, d)
        else:
            assert g0 % tt == 0 and g1 % tt == 0
            rows, blk = (slice(g0 // tt, g1 // tt),), ((g1 - g0) // tt, tt, d)
        x = x_ref[rows].reshape(g1 - g0, d)
        q = _dot(_rms(x, xg_ref[...]).astype(BF16), wq_ref[...])
        hsl = [slice(h * dh, (h + 1) * dh) for h in range(MEM_HEADS)]
        segs = [(b, max(b * tt, g0) - g0, min((b + 1) * tt, g1) - g0) for b in range(g0 // tt, -(-g1 // tt))]
        qb = q.astype(BF16)
        stacked = tt * MEM_HEADS <= STACK_ROWS
        scores = []
        for b, lo, hi in segs:
            if stacked:
                zq = jnp.zeros((hi - lo, dh), BF16)
                q_stack = jnp.concatenate(
                    [jnp.concatenate([qb[lo:hi, hs] if j == h else zq for j in range(MEM_HEADS)], axis=1)
                     for h, hs in enumerate(hsl)], axis=0)
                scores.append(_dot_nt(q_stack, kb[b]))
            else:
                scores += [_dot_nt(qb[lo:hi, hs], kb[b][:, hs]) for hs in hsl]
        sc = jnp.concatenate(scores, axis=0) * scale
        ex = jnp.exp(sc - jnp.max(sc, axis=-1, keepdims=True))
        p = (ex / jnp.sum(ex, axis=-1, keepdims=True)).astype(BF16)
        att, r0 = [], 0
        for b, lo, hi in segs:
            n = hi - lo
            if stacked:
                pv = _dot(p[r0:r0 + MEM_HEADS * n], vb[b])
                heads = [pv[h * n:(h + 1) * n, hs] for h, hs in enumerate(hsl)]
            else:
                heads = [_dot(p[r0 + h * n:r0 + (h + 1) * n], vb[b][:, hs]) for h, hs in enumerate(hsl)]
            r0 += MEM_HEADS * n
            att.append(jnp.concatenate(heads, axis=1).astype(BF16))
        x = x + _dot(jnp.concatenate(att, axis=0), wo_ref[...])
        x = _ffn(x, ng_ref[...], wg_ref[...], wu_ref[...], wd_ref[...])
        o_ref[rows] = _rms(x, fg_ref[...]).reshape(blk)


def _tail_call(x, k, v, xg, wq, wo, ng, wg, wu, wd, fg, nb, tt, n_split):
    bsz, seq, d = x.shape
    kern = functools.partial(_tail_kernel, nb, tt, n_split)
    xspec = pl.BlockSpec((nb, tt, d), lambda i, t: (i, t, 0))
    kvspec = pl.BlockSpec((nb,) + k.shape[1:], lambda i, t: (i, 0, 0))
    return pl.pallas_call(
        kern,
        grid=(bsz // nb, seq // tt),
        in_specs=[xspec, kvspec, kvspec]
                 + [_const_spec(a.shape) for a in (xg, wq, wo, ng, wg, wu, wd, fg)],
        out_specs=xspec,
        out_shape=jax.ShapeDtypeStruct(x.shape, F32),
        compiler_params=pltpu.CompilerParams(dimension_semantics=("arbitrary", "arbitrary"),
                                             vmem_limit_bytes=VMEM_LIMIT),
        name="tail",
    )(x, k, v, xg, wq, wo, ng, wg, wu, wd, fg)


def _mix_tail(x, sconv, shgrn, mk, mv, w, fg, mix_tile, tail_tile):
    x, new_conv, new_hgrn = _mix_call(x, w["mix_norm"], w["w_in"], w["conv_dw"], w["conv_dw_b"],
                                      w["conv_ln_g"], w["conv_ln_b"], w["hgrn_lb"], w["hgrn_norm"],
                                      w["w_out"], sconv, shgrn, *mix_tile)
    y = _tail_call(x, mk, mv, w["xattn_norm"], w["xattn_wq"], w["xattn_wo"], w["ffn2_norm"],
                   w["ffn2_w_gate"], w["ffn2_w_up"], w["ffn2_w_down"], fg, *tail_tile)
    return y, new_conv, new_hgrn


def kernel(x_prompt, x_sample, mem_prompt, state_conv, state_hgrn, cache_mem_k, cache_mem_v, ffn1_norm, ffn1_w_gate, ffn1_w_up, ffn1_w_down, mix_norm, w_in, conv_dw, conv_dw_b, conv_ln_g, conv_ln_b, hgrn_lb, hgrn_norm, w_out, mem_norm, mem_wk, mem_wv, xattn_norm, xattn_wq, xattn_wo, ffn2_norm, ffn2_w_gate, ffn2_w_up, ffn2_w_down, final_norm):
    depth = ffn1_norm.shape[0]
    assert depth == 1, "the per-layer forget-gate bound is taken from row 0 of the cumulative softmax"
    bp, seq, d = x_prompt.shape
    bs, dec_seq, _ = x_sample.shape
    n_mem = mem_prompt.shape[1]
    d_hgrn = hgrn_lb.shape[1]
    d_conv = conv_dw.shape[2]
    d_head = d_hgrn // HGRN_HEADS

    def row(a):
        return a.reshape(1, -1)

    fg = row(final_norm)
    (mk, mv, mk_flat, mv_flat), ffn1_w = _memkv_call(
        mem_prompt, row(mem_norm[0]), mem_wk[0].astype(BF16), mem_wv[0].astype(BF16), 2,
        cast=(ffn1_w_gate[0], ffn1_w_up[0], ffn1_w_down[0]))

    ffn1 = (row(ffn1_norm[0]), *ffn1_w)
    later = {"w_in": w_in[0], "w_out": w_out[0], "xattn_wq": xattn_wq[0], "xattn_wo": xattn_wo[0],
             "ffn2_w_gate": ffn2_w_gate[0], "ffn2_w_up": ffn2_w_up[0], "ffn2_w_down": ffn2_w_down[0]}
    xp, later_bf16 = _ffn_call(x_prompt.reshape(bp * seq, d), *ffn1, 1024, 4, cast=tuple(later.values()))
    xs, _ = _ffn_call(x_sample.reshape(bs * dec_seq, d), *ffn1, 256, 1)
    w = dict(zip(later, later_bf16))
    w.update({
        "mix_norm": row(mix_norm[0]), "conv_dw": conv_dw[0], "conv_dw_b": row(conv_dw_b[0]),
        "conv_ln_g": row(conv_ln_g[0]), "conv_ln_b": row(conv_ln_b[0]),
        "hgrn_lb": hgrn_lb, "hgrn_norm": row(hgrn_norm[0]),
        "xattn_norm": row(xattn_norm[0]), "ffn2_norm": row(ffn2_norm[0]),
    })

    y_p, conv_p, hgrn_p = _mix_tail(
        xp.reshape(bp, seq, d), jnp.zeros((bp, CONV_BUF, d_conv), F32),
        jnp.zeros((bp, HGRN_HEADS, d_head, d_head), F32), mk_flat, mv_flat, w, fg,
        mix_tile=(1, 512), tail_tile=(1, 1024, 2))
    y_s, conv_s, hgrn_s = _mix_tail(
        xs.reshape(bs, dec_seq, d), state_conv[0], state_hgrn[0],
        cache_mem_k[0].reshape(bs, n_mem, d), cache_mem_v[0].reshape(bs, n_mem, d), w, fg,
        mix_tile=(8, dec_seq), tail_tile=(4, dec_seq, 1))

    return (y_p, y_s, conv_p[None], hgrn_p[None], mk[None], mv[None],
            conv_s[None], hgrn_s[None])
```

```python
import functools

import jax
import jax.numpy as jnp
from jax import lax
from jax.experimental import pallas as pl
from jax.experimental.pallas import tpu as pltpu

F32 = jnp.float32
BF16 = jnp.bfloat16

EPS = 1e-6
CHUNK = 64
CONV_WIDTH = 31
CONV_BUF = CONV_WIDTH - 1
HGRN_HEADS = 4
MEM_HEADS = 4
HALO = 32
CONV_ROWS = 32
SUBLANES = 8
BF16_ROWS = 16
HGRN_ROWS = 256
STACK_ROWS = 256
VMEM_LIMIT = 56 * 1024 * 1024


def _dot(a, b):
    return jnp.dot(a, b, preferred_element_type=F32)


def _dot_nt(a, b):
    return lax.dot_general(a, b, (((1,), (1,)), ((), ())), preferred_element_type=F32)


def _dot_tn(a, b):
    return lax.dot_general(a, b, (((0,), (0,)), ((), ())), preferred_element_type=F32)


def _rms(x, g):
    return x * lax.rsqrt(jnp.mean(x * x, axis=-1, keepdims=True) + EPS) * g


def _sigmoid(x):
    return 0.5 * jnp.tanh(0.5 * x) + 0.5


def _silu(x):
    return x * _sigmoid(x)


def _ffn(x, ng, wg, wu, wd):
    xn = _rms(x, ng).astype(BF16)
    h = (_silu(_dot(xn, wg)) * _dot(xn, wu)).astype(BF16)
    return x + 0.5 * _dot(h, wd)


def _const_spec(shape):
    nd = len(shape)
    return pl.BlockSpec(shape, lambda *_: (0,) * nd, pipeline_mode=pl.Buffered(1))


def _row_groups(m, n_split):
    mg = m // n_split
    return [(g * mg, (g + 1) * mg) for g in range(n_split)]


def _cast_specs(arrays, n_steps):
    specs = []
    for a in arrays:
        n_blk = max(n for n in range(1, n_steps + 1) if a.shape[0] % (n * BF16_ROWS) == 0)
        specs.append(pl.BlockSpec((a.shape[0] // n_blk, a.shape[1]),
                                  lambda i, n_blk=n_blk: (jnp.minimum(i, n_blk - 1), 0)))
    return specs


def _cast_rows(src_refs, dst_refs):
    for src, dst in zip(src_refs, dst_refs):
        dst[...] = src[...].astype(BF16)


def _ffn_kernel(n_split, n_cast, x_ref, ng_ref, wg_ref, wu_ref, wd_ref, *rest):
    cast_in, o_ref, cast_out = rest[:n_cast], rest[n_cast], rest[n_cast + 1:]
    for g0, g1 in _row_groups(x_ref.shape[0], n_split):
        o_ref[g0:g1, :] = _ffn(x_ref[g0:g1, :], ng_ref[...], wg_ref[...], wu_ref[...], wd_ref[...])
    _cast_rows(cast_in, cast_out)


def _ffn_call(x, ng, wg, wu, wd, tm, n_split, cast=()):
    m, d = x.shape
    n_steps = m // tm
    cast_specs = _cast_specs(cast, n_steps)
    out = pl.pallas_call(
        functools.partial(_ffn_kernel, n_split, len(cast)),
        grid=(n_steps,),
        in_specs=[pl.BlockSpec((tm, d), lambda i: (i, 0)),
                  _const_spec(ng.shape), _const_spec(wg.shape), _const_spec(wu.shape),
                  _const_spec(wd.shape)] + cast_specs,
        out_specs=[pl.BlockSpec((tm, d), lambda i: (i, 0))] + cast_specs,
        out_shape=[jax.ShapeDtypeStruct((m, d), F32)]
                  + [jax.ShapeDtypeStruct(a.shape, BF16) for a in cast],
        compiler_params=pltpu.CompilerParams(dimension_semantics=("arbitrary",),
                                             vmem_limit_bytes=VMEM_LIMIT),
        name="ffn",
    )(x, ng, wg, wu, wd, *cast)
    return out[0], out[1:]


def _memkv_kernel(n_cast, m_ref, g_ref, wk_ref, wv_ref, *rest):
    cast_in, (k_ref, v_ref, kb_ref, vb_ref), cast_out = rest[:n_cast], rest[n_cast:n_cast + 4], rest[n_cast + 4:]
    _cast_rows(cast_in, cast_out)
    nb, n_mem, n_head, dh = k_ref.shape
    mn = _rms(m_ref[...].reshape(nb * n_mem, m_ref.shape[-1]), g_ref[...]).astype(BF16)
    for w_ref, o_ref, ob_ref in ((wk_ref, k_ref, kb_ref), (wv_ref, v_ref, vb_ref)):
        y = _dot(mn, w_ref[...])
        ob_ref[...] = y.astype(BF16).reshape(ob_ref.shape)
        for h in range(n_head):
            o_ref[:, :, h, :] = y[:, h * dh:(h + 1) * dh].reshape(nb, n_mem, dh)


def _memkv_call(mem, g, wk, wv, nb, cast=()):
    bsz, n_mem, d = mem.shape
    dh = d // MEM_HEADS
    n_steps = bsz // nb
    row = pl.BlockSpec((nb, n_mem, d), lambda i: (i, 0, 0))
    kv = pl.BlockSpec((nb, n_mem, MEM_HEADS, dh), lambda i: (i, 0, 0, 0))
    cast_specs = _cast_specs(cast, n_steps)
    out = pl.pallas_call(
        functools.partial(_memkv_kernel, len(cast)),
        grid=(n_steps,),
        in_specs=[row, _const_spec(g.shape), _const_spec(wk.shape), _const_spec(wv.shape)] + cast_specs,
        out_specs=[kv, kv, row, row] + cast_specs,
        out_shape=[jax.ShapeDtypeStruct((bsz, n_mem, MEM_HEADS, dh), F32)] * 2
                  + [jax.ShapeDtypeStruct((bsz, n_mem, d), BF16)] * 2
                  + [jax.ShapeDtypeStruct(a.shape, BF16) for a in cast],
        compiler_params=pltpu.CompilerParams(dimension_semantics=("arbitrary",),
                                             vmem_limit_bytes=VMEM_LIMIT),
        name="memkv",
    )(mem, g, wk, wv, *cast)
    return out[:4], out[4:]


def _hgrn_rows(z, lbv, hn, masks, state, chunk, d_head):
    tt = z.shape[0]
    n_c = tt // chunk
    dh = HGRN_HEADS * d_head
    tri, causal = masks
    zq, zf, v, zg = (z[:, i * dh:(i + 1) * dh] for i in range(4))
    f = lbv + (1.0 - lbv) * _sigmoid(zf)
    q = _silu(zq)
    k = 1.0 - f
    logf = jnp.log(f)
    l_hi = logf.astype(BF16)
    l_lo = (logf - l_hi.astype(F32)).astype(BF16)
    gc = _dot(tri, l_hi) + _dot(tri, l_lo)
    g_last = [gc[(c + 1) * chunk - 1:(c + 1) * chunk, :] for c in range(n_c)]
    g_end = jnp.concatenate([jnp.broadcast_to(g, (chunk, dh)) for g in g_last], axis=0)
    q_dec = (q * jnp.exp(gc)).astype(BF16)
    k_dec = (k * jnp.exp(-gc)).astype(BF16)
    k_end = (k * jnp.exp(g_end - gc)).astype(BF16)
    decay = [jnp.exp(g) for g in g_last]
    vb = v.astype(BF16)
    zero = jnp.zeros((chunk, d_head), BF16)
    out, new_state = [], []
    for h in range(HGRN_HEADS):
        hs = slice(h * d_head, (h + 1) * d_head)
        a = jnp.where(causal, _dot_nt(q_dec[:, hs], k_dec[:, hs]), 0.0).astype(BF16)
        o_intra = _dot(a, vb[:, hs])
        k_blocks = jnp.concatenate(
            [jnp.concatenate([k_end[c * chunk:(c + 1) * chunk, hs] if j == c else zero
                              for j in range(n_c)], axis=1) for c in range(n_c)], axis=0)
        kv = _dot_tn(vb[:, hs], k_blocks)
        st = [state[h]]
        for c in range(n_c):
            st.append(st[c] * decay[c][:, hs] + kv[:, c * d_head:(c + 1) * d_head])
        new_state.append(st[n_c])
        stack = jnp.concatenate([s.astype(BF16) for s in st[:n_c]], axis=0)
        o_inter = _dot_nt(q_dec[:, hs], stack)
        o = o_intra + jnp.concatenate(
            [o_inter[c * chunk:(c + 1) * chunk, c * d_head:(c + 1) * d_head] for c in range(n_c)], axis=0)
        on = o * lax.rsqrt(jnp.mean(o * o, axis=-1, keepdims=True) + EPS) * hn[:, hs]
        out.append((on * _silu(zg[:, hs])).astype(BF16))
    return out, new_state


def _mix_kernel(nb, tt, chunk,
                x_ref, ng_ref, win_ref, cw_ref, cb_ref, lng_ref, lnb_ref, hlb_ref, hn_ref, wout_ref,
                sconv_ref, shgrn_ref,
                o_ref, oconv_ref, ohgrn_ref,
                zbuf, ubuf, ushift, wtap, sbuf, cat):
    t = pl.program_id(1)
    d_conv = cw_ref.shape[1]
    d_head = sbuf.shape[-1]
    m = nb * tt
    first = HALO - CONV_BUF

    @pl.when((pl.program_id(0) == 0) & (t == 0))
    def _spread_taps():
        wtap[...] = jnp.broadcast_to(cw_ref[...][:, None, :], wtap.shape)

    @pl.when(t == 0)
    def _load_state():
        ubuf[:, 0:first, :] = jnp.zeros((nb, first, d_conv), F32)
        ubuf[:, first:HALO, :] = sconv_ref[...]
        for b in range(nb):
            for h in range(HGRN_HEADS):
                sbuf[b, h] = shgrn_ref[b, h].T

    x = x_ref[...].reshape(m, x_ref.shape[-1])
    zbuf[...] = _dot(_rms(x, ng_ref[...]).astype(BF16), win_ref[...])

    for b in range(nb):
        r = slice(b * tt, (b + 1) * tt)
        ubuf[b, HALO:HALO + tt, :] = zbuf[r, 0:d_conv] * _sigmoid(zbuf[r, d_conv:2 * d_conv])
    cb = cb_ref[...]
    lng = lng_ref[...]
    lnb = lnb_ref[...]
    n_shift = ushift.shape[2]
    for r in range(1, SUBLANES):
        ushift[r - 1] = ubuf[:, r:r + n_shift, :]
    for b in range(nb):
        for r0 in range(0, tt, CONV_ROWS):
            acc = jnp.broadcast_to(cb, (CONV_ROWS // SUBLANES, SUBLANES, d_conv))
            for j in range(CONV_WIDTH):
                a8, r = divmod(first + j, SUBLANES)
                lo = r0 + a8 * SUBLANES
                src = ubuf[b, lo:lo + CONV_ROWS, :] if r == 0 else ushift[r - 1, b, lo:lo + CONV_ROWS, :]
                acc = acc + wtap[j] * src.reshape(acc.shape)
            acc = acc.reshape(CONV_ROWS, d_conv)
            mu = jnp.mean(acc, axis=-1, keepdims=True)
            dev = acc - mu
            var = jnp.mean(dev * dev, axis=-1, keepdims=True)
            yc = _silu(dev * lax.rsqrt(var + EPS) * lng + lnb)
            cat[b * tt + r0:b * tt + r0 + CONV_ROWS, 0:d_conv] = yc.astype(BF16)
    new_rows = ubuf[:, tt + first:tt + HALO, :]
    ubuf[:, first:HALO, :] = new_rows

    hlb = hlb_ref[...]
    e = jnp.exp(hlb - jnp.max(hlb, axis=0, keepdims=True))
    lbv = e[0:1, :] / jnp.sum(e, axis=0, keepdims=True)
    shift = chunk.bit_length() - 1
    assert chunk == 1 << shift
    sub = min(tt, HGRN_ROWS)
    row_i = lax.broadcasted_iota(jnp.int32, (sub, sub), 0)
    col_i = lax.broadcasted_iota(jnp.int32, (sub, sub), 1)
    causal = (row_i >= col_i) & (lax.shift_right_logical(row_i, shift) == lax.shift_right_logical(col_i, shift))
    masks = (jnp.where(causal, 1.0, 0.0).astype(BF16), causal)
    for b in range(nb):
        state = [sbuf[b, h] for h in range(HGRN_HEADS)]
        for r0 in range(b * tt, (b + 1) * tt, sub):
            r = slice(r0, r0 + sub)
            heads, state = _hgrn_rows(zbuf[r, 2 * d_conv:], lbv, hn_ref[...], masks, state, chunk, d_head)
            for h in range(HGRN_HEADS):
                cat[r, d_conv + h * d_head:d_conv + (h + 1) * d_head] = heads[h]
        for h in range(HGRN_HEADS):
            sbuf[b, h] = state[h]

    o_ref[...] = (x + _dot(cat[...], wout_ref[...])).reshape(o_ref.shape)

    @pl.when(t == pl.num_programs(1) - 1)
    def _store_state():
        oconv_ref[...] = ubuf[:, first:HALO, :]
        for b in range(nb):
            for h in range(HGRN_HEADS):
                ohgrn_ref[b, h] = sbuf[b, h].T


def _mix_call(x, ng, win, cw, cb, lng, lnb, hlb, hn, wout, sconv, shgrn, nb, tt):
    bsz, seq, d = x.shape
    d_conv = cw.shape[1]
    d_head = shgrn.shape[-1]
    chunk = min(CHUNK, seq)
    kern = functools.partial(_mix_kernel, nb, tt, chunk)
    xspec = pl.BlockSpec((nb, tt, d), lambda i, t: (i, t, 0))
    cspec = pl.BlockSpec((nb, CONV_BUF, d_conv), lambda i, t: (i, 0, 0))
    sspec = pl.BlockSpec((nb, HGRN_HEADS, d_head, d_head), lambda i, t: (i, 0, 0, 0))
    return pl.pallas_call(
        kern,
        grid=(bsz // nb, seq // tt),
        in_specs=[xspec] + [_const_spec(a.shape) for a in (ng, win, cw, cb, lng, lnb, hlb, hn, wout)]
                 + [cspec, sspec],
        out_specs=[xspec, cspec, sspec],
        out_shape=[jax.ShapeDtypeStruct(x.shape, F32),
                   jax.ShapeDtypeStruct(sconv.shape, F32),
                   jax.ShapeDtypeStruct(shgrn.shape, F32)],
        scratch_shapes=[pltpu.VMEM((nb * tt, win.shape[1]), F32),
                        pltpu.VMEM((nb, HALO + tt, d_conv), F32),
                        pltpu.VMEM((SUBLANES - 1, nb, HALO + tt - SUBLANES, d_conv), F32),
                        pltpu.VMEM((CONV_WIDTH, SUBLANES, d_conv), F32),
                        pltpu.VMEM((nb, HGRN_HEADS, d_head, d_head), F32),
                        pltpu.VMEM((nb * tt, d), BF16)],
        compiler_params=pltpu.CompilerParams(dimension_semantics=("arbitrary", "arbitrary"),
                                             vmem_limit_bytes=VMEM_LIMIT),
        name="mix",
    )(x, ng, win, cw, cb, lng, lnb, hlb, hn, wout, sconv, shgrn)


def _tail_kernel(nb, tt, n_split, kv_native,
                 x_ref, k_ref, v_ref, xg_ref, wq_ref, wo_ref, ng_ref, wg_ref, wu_ref, wd_ref, fg_ref,
                 o_ref, *scratch):
    d = x_ref.shape[-1]
    dh = d // MEM_HEADS
    scale = dh ** -0.5
    if kv_native:
        kvbuf, sem = scratch
        i, n = pl.program_id(0), pl.num_programs(0)

        def kv_copies(blk, slot):
            return [pltpu.make_async_copy(src.at[pl.ds(blk * nb, nb), :, h, :],
                                          kvbuf.at[slot, j, :, :, h * dh:(h + 1) * dh],
                                          sem.at[slot, j, h])
                    for j, src in enumerate((k_ref, v_ref)) for h in range(MEM_HEADS)]

        @pl.when(i == 0)
        def _first_block():
            for cp in kv_copies(0, 0):
                cp.start()

        @pl.when(i + 1 < n)
        def _next_block():
            for cp in kv_copies(i + 1, lax.rem(i + 1, 2)):
                cp.start()

        slot = lax.rem(i, 2)
        for cp in kv_copies(i, slot):
            cp.wait()
        k_ref, v_ref = kvbuf.at[slot, 0], kvbuf.at[slot, 1]
    kb = [k_ref[b].astype(BF16) for b in range(nb)]
    vb = [v_ref[b].astype(BF16) for b in range(nb)]
    for g0, g1 in _row_groups(nb * tt, n_split):
        if nb == 1:
            rows, blk = (0, slice(g0, g1)), (g1 - g0, d)
        else:
            assert g0 % tt == 0 and g1 % tt == 0
            rows, blk = (slice(g0 // tt, g1 // tt),), ((g1 - g0) // tt, tt, d)
        x = x_ref[rows].reshape(g1 - g0, d)
        q = _dot(_rms(x, xg_ref[...]).astype(BF16), wq_ref[...])
        hsl = [slice(h * dh, (h + 1) * dh) for h in range(MEM_HEADS)]
        segs = [(b, max(b * tt, g0) - g0, min((b + 1) * tt, g1) - g0) for b in range(g0 // tt, -(-g1 // tt))]
        qb = q.astype(BF16)
        stacked = tt * MEM_HEADS <= STACK_ROWS
        scores = []
        for b, lo, hi in segs:
            if stacked:
                zq = jnp.zeros((hi - lo, dh), BF16)
                q_stack = jnp.concatenate(
                    [jnp.concatenate([qb[lo:hi, hs] if j == h else zq for j in range(MEM_HEADS)], axis=1)
                     for h, hs in enumerate(hsl)], axis=0)
                scores.append(_dot_nt(q_stack, kb[b]))
            else:
                scores += [_dot_nt(qb[lo:hi, hs], kb[b][:, hs]) for hs in hsl]
        sc = jnp.concatenate(scores, axis=0) * scale
        ex = jnp.exp(sc - jnp.max(sc, axis=-1, keepdims=True))
        p = (ex / jnp.sum(ex, axis=-1, keepdims=True)).astype(BF16)
        att, r0 = [], 0
        for b, lo, hi in segs:
            n = hi - lo
            if stacked:
                pv = _dot(p[r0:r0 + MEM_HEADS * n], vb[b])
                heads = [pv[h * n:(h + 1) * n, hs] for h, hs in enumerate(hsl)]
            else:
                heads = [_dot(p[r0 + h * n:r0 + (h + 1) * n], vb[b][:, hs]) for h, hs in enumerate(hsl)]
            r0 += MEM_HEADS * n
            att.append(jnp.concatenate(heads, axis=1).astype(BF16))
        x = x + _dot(jnp.concatenate(att, axis=0), wo_ref[...])
        x = _ffn(x, ng_ref[...], wg_ref[...], wu_ref[...], wd_ref[...])
        o_ref[rows] = _rms(x, fg_ref[...]).reshape(blk)


def _tail_call(x, k, v, xg, wq, wo, ng, wg, wu, wd, fg, nb, tt, n_split):
    bsz, seq, d = x.shape
    kv_native = k.ndim == 4
    kern = functools.partial(_tail_kernel, nb, tt, n_split, kv_native)
    xspec = pl.BlockSpec((nb, tt, d), lambda i, t: (i, t, 0))
    if kv_native:
        assert seq == tt, "the key/value copies are pipelined over the batch axis of the grid"
        kvspec = pl.BlockSpec(memory_space=pl.ANY)
        scratch = [pltpu.VMEM((2, 2, nb, k.shape[1], d), F32), pltpu.SemaphoreType.DMA((2, 2, MEM_HEADS))]
    else:
        kvspec = pl.BlockSpec((nb,) + k.shape[1:], lambda i, t: (i, 0, 0))
        scratch = []
    return pl.pallas_call(
        kern,
        grid=(bsz // nb, seq // tt),
        in_specs=[xspec, kvspec, kvspec]
                 + [_const_spec(a.shape) for a in (xg, wq, wo, ng, wg, wu, wd, fg)],
        out_specs=xspec,
        out_shape=jax.ShapeDtypeStruct(x.shape, F32),
        scratch_shapes=scratch,
        compiler_params=pltpu.CompilerParams(dimension_semantics=("arbitrary", "arbitrary"),
                                             vmem_limit_bytes=VMEM_LIMIT),
        name="tail",
    )(x, k, v, xg, wq, wo, ng, wg, wu, wd, fg)


def _mix_tail(x, sconv, shgrn, mk, mv, w, fg, mix_tile, tail_tile):
    x, new_conv, new_hgrn = _mix_call(x, w["mix_norm"], w["w_in"], w["conv_dw"], w["conv_dw_b"],
                                      w["conv_ln_g"], w["conv_ln_b"], w["hgrn_lb"], w["hgrn_norm"],
                                      w["w_out"], sconv, shgrn, *mix_tile)
    y = _tail_call(x, mk, mv, w["xattn_norm"], w["xattn_wq"], w["xattn_wo"], w["ffn2_norm"],
                   w["ffn2_w_gate"], w["ffn2_w_up"], w["ffn2_w_down"], fg, *tail_tile)
    return y, new_conv, new_hgrn


def kernel(x_prompt, x_sample, mem_prompt, state_conv, state_hgrn, cache_mem_k, cache_mem_v, ffn1_norm, ffn1_w_gate, ffn1_w_up, ffn1_w_down, mix_norm, w_in, conv_dw, conv_dw_b, conv_ln_g, conv_ln_b, hgrn_lb, hgrn_norm, w_out, mem_norm, mem_wk, mem_wv, xattn_norm, xattn_wq, xattn_wo, ffn2_norm, ffn2_w_gate, ffn2_w_up, ffn2_w_down, final_norm):
    depth = ffn1_norm.shape[0]
    assert depth == 1, "the per-layer forget-gate bound is taken from row 0 of the cumulative softmax"
    bp, seq, d = x_prompt.shape
    bs, dec_seq, _ = x_sample.shape
    n_mem = mem_prompt.shape[1]
    d_hgrn = hgrn_lb.shape[1]
    d_conv = conv_dw.shape[2]
    d_head = d_hgrn // HGRN_HEADS

    def row(a):
        return a.reshape(1, -1)

    fg = row(final_norm)
    (mk, mv, mk_flat, mv_flat), ffn1_w = _memkv_call(
        mem_prompt, row(mem_norm[0]), mem_wk[0].astype(BF16), mem_wv[0].astype(BF16), 2,
        cast=(ffn1_w_gate[0], ffn1_w_up[0], ffn1_w_down[0]))

    ffn1 = (row(ffn1_norm[0]), *ffn1_w)
    later = {"w_in": w_in[0], "w_out": w_out[0], "xattn_wq": xattn_wq[0], "xattn_wo": xattn_wo[0],
             "ffn2_w_gate": ffn2_w_gate[0], "ffn2_w_up": ffn2_w_up[0], "ffn2_w_down": ffn2_w_down[0]}
    xp, later_bf16 = _ffn_call(x_prompt.reshape(bp * seq, d), *ffn1, 1024, 4, cast=tuple(later.values()))
    xs, _ = _ffn_call(x_sample.reshape(bs * dec_seq, d), *ffn1, 256, 1)
    w = dict(zip(later, later_bf16))
    w.update({
        "mix_norm": row(mix_norm[0]), "conv_dw": conv_dw[0], "conv_dw_b": row(conv_dw_b[0]),
        "conv_ln_g": row(conv_ln_g[0]), "conv_ln_b": row(conv_ln_b[0]),
        "hgrn_lb": hgrn_lb, "hgrn_norm": row(hgrn_norm[0]),
        "xattn_norm": row(xattn_norm[0]), "ffn2_norm": row(ffn2_norm[0]),
    })

    y_p, conv_p, hgrn_p = _mix_tail(
        xp.reshape(bp, seq, d), jnp.zeros((bp, CONV_BUF, d_conv), F32),
        jnp.zeros((bp, HGRN_HEADS, d_head, d_head), F32), mk_flat, mv_flat, w, fg,
        mix_tile=(1, 512), tail_tile=(1, 1024, 2))
    y_s, conv_s, hgrn_s = _mix_tail(
        xs.reshape(bs, dec_seq, d), state_conv[0], state_hgrn[0],
        cache_mem_k[0], cache_mem_v[0], w, fg,
        mix_tile=(8, dec_seq), tail_tile=(4, dec_seq, 1))

    return (y_p, y_s, conv_p[None], hgrn_p[None], mk[None], mv[None],
            conv_s[None], hgrn_s[None])
```

```python
import functools

import jax
import jax.numpy as jnp
from jax import lax
from jax.experimental import pallas as pl
from jax.experimental.pallas import tpu as pltpu

F32 = jnp.float32
BF16 = jnp.bfloat16

EPS = 1e-6
CHUNK = 64
CONV_WIDTH = 31
CONV_BUF = CONV_WIDTH - 1
HGRN_HEADS = 4
MEM_HEADS = 4
HALO = 32
CONV_ROWS = 32
SUBLANES = 8
BF16_ROWS = 16
HGRN_ROWS = 256
STACK_ROWS = 256
VMEM_LIMIT = 56 * 1024 * 1024


def _dot(a, b):
    return jnp.dot(a, b, preferred_element_type=F32)


def _dot_nt(a, b):
    return lax.dot_general(a, b, (((1,), (1,)), ((), ())), preferred_element_type=F32)


def _dot_tn(a, b):
    return lax.dot_general(a, b, (((0,), (0,)), ((), ())), preferred_element_type=F32)


def _rms(x, g):
    return x * lax.rsqrt(jnp.mean(x * x, axis=-1, keepdims=True) + EPS) * g


def _sigmoid(x):
    return 0.5 * jnp.tanh(0.5 * x) + 0.5


def _silu(x):
    return x * _sigmoid(x)


def _ffn(x, ng, wg, wu, wd):
    xn = _rms(x, ng).astype(BF16)
    h = (_silu(_dot(xn, wg)) * _dot(xn, wu)).astype(BF16)
    return x + 0.5 * _dot(h, wd)


def _const_spec(shape):
    nd = len(shape)
    return pl.BlockSpec(shape, lambda *_: (0,) * nd, pipeline_mode=pl.Buffered(1))


def _row_groups(m, n_split):
    mg = m // n_split
    return [(g * mg, (g + 1) * mg) for g in range(n_split)]


def _cast_specs(arrays, n_steps):
    specs = []
    for a in arrays:
        n_blk = max(n for n in range(1, n_steps + 1) if a.shape[0] % (n * BF16_ROWS) == 0)
        specs.append(pl.BlockSpec((a.shape[0] // n_blk, a.shape[1]),
                                  lambda i, n_blk=n_blk: (jnp.minimum(i, n_blk - 1), 0)))
    return specs


def _cast_rows(src_refs, dst_refs):
    for src, dst in zip(src_refs, dst_refs):
        dst[...] = src[...].astype(BF16)


def _ffn_kernel(n_split, n_cast, x_ref, ng_ref, wg_ref, wu_ref, wd_ref, *rest):
    cast_in, o_ref, cast_out = rest[:n_cast], rest[n_cast], rest[n_cast + 1:]
    for g0, g1 in _row_groups(x_ref.shape[0], n_split):
        o_ref[g0:g1, :] = _ffn(x_ref[g0:g1, :], ng_ref[...], wg_ref[...], wu_ref[...], wd_ref[...])
    _cast_rows(cast_in, cast_out)


def _ffn_call(x, ng, wg, wu, wd, tm, n_split, cast=()):
    m, d = x.shape
    n_steps = m // tm
    cast_specs = _cast_specs(cast, n_steps)
    out = pl.pallas_call(
        functools.partial(_ffn_kernel, n_split, len(cast)),
        grid=(n_steps,),
        in_specs=[pl.BlockSpec((tm, d), lambda i: (i, 0)),
                  _const_spec(ng.shape), _const_spec(wg.shape), _const_spec(wu.shape),
                  _const_spec(wd.shape)] + cast_specs,
        out_specs=[pl.BlockSpec((tm, d), lambda i: (i, 0))] + cast_specs,
        out_shape=[jax.ShapeDtypeStruct((m, d), F32)]
                  + [jax.ShapeDtypeStruct(a.shape, BF16) for a in cast],
        compiler_params=pltpu.CompilerParams(dimension_semantics=("arbitrary",),
                                             vmem_limit_bytes=VMEM_LIMIT),
        name="ffn",
    )(x, ng, wg, wu, wd, *cast)
    return out[0], out[1:]


def _memkv_kernel(n_cast, m_ref, g_ref, wk_ref, wv_ref, *rest):
    cast_in, (k_ref, v_ref, kb_ref, vb_ref), cast_out = rest[:n_cast], rest[n_cast:n_cast + 4], rest[n_cast + 4:]
    _cast_rows(cast_in, cast_out)
    nb, n_mem, n_head, dh = k_ref.shape
    mn = _rms(m_ref[...].reshape(nb * n_mem, m_ref.shape[-1]), g_ref[...]).astype(BF16)
    for w_ref, o_ref, ob_ref in ((wk_ref, k_ref, kb_ref), (wv_ref, v_ref, vb_ref)):
        y = _dot(mn, w_ref[...])
        ob_ref[...] = y.astype(BF16).reshape(ob_ref.shape)
        for h in range(n_head):
            o_ref[:, :, h, :] = y[:, h * dh:(h + 1) * dh].reshape(nb, n_mem, dh)


def _memkv_call(mem, g, wk, wv, nb, cast=()):
    bsz, n_mem, d = mem.shape
    dh = d // MEM_HEADS
    n_steps = bsz // nb
    row = pl.BlockSpec((nb, n_mem, d), lambda i: (i, 0, 0))
    kv = pl.BlockSpec((nb, n_mem, MEM_HEADS, dh), lambda i: (i, 0, 0, 0))
    cast_specs = _cast_specs(cast, n_steps)
    out = pl.pallas_call(
        functools.partial(_memkv_kernel, len(cast)),
        grid=(n_steps,),
        in_specs=[row, _const_spec(g.shape), _const_spec(wk.shape), _const_spec(wv.shape)] + cast_specs,
        out_specs=[kv, kv, row, row] + cast_specs,
        out_shape=[jax.ShapeDtypeStruct((bsz, n_mem, MEM_HEADS, dh), F32)] * 2
                  + [jax.ShapeDtypeStruct((bsz, n_mem, d), BF16)] * 2
                  + [jax.ShapeDtypeStruct(a.shape, BF16) for a in cast],
        compiler_params=pltpu.CompilerParams(dimension_semantics=("arbitrary",),
                                             vmem_limit_bytes=VMEM_LIMIT),
        name="memkv",
    )(mem, g, wk, wv, *cast)
    return out[:4], out[4:]


def _hgrn_tile(z, lbv, hn, masks, states, chunk, chunks_per_seq, sub, d_head):
    m = z.shape[0]
    n_c, per_sub = m // chunk, sub // chunk
    subs = [slice(s0, s0 + sub) for s0 in range(0, m, sub)]
    dh = HGRN_HEADS * d_head
    tri, causal = masks
    zq, zf, v, zg = (z[:, i * dh:(i + 1) * dh] for i in range(4))
    f = lbv + (1.0 - lbv) * _sigmoid(zf)
    q = _silu(zq)
    k = 1.0 - f
    logf = jnp.log(f)
    l_hi = logf.astype(BF16)
    l_lo = (logf - l_hi.astype(F32)).astype(BF16)
    gc = jnp.concatenate([_dot(tri, l_hi[rs]) + _dot(tri, l_lo[rs]) for rs in subs], axis=0)
    g_last = [gc[(c + 1) * chunk - 1:(c + 1) * chunk, :] for c in range(n_c)]
    g_end = jnp.concatenate([jnp.broadcast_to(g, (chunk, dh)) for g in g_last], axis=0)
    q_dec = (q * jnp.exp(gc)).astype(BF16)
    k_dec = (k * jnp.exp(-gc)).astype(BF16)
    k_end = (k * jnp.exp(g_end - gc)).astype(BF16)
    decay = [jnp.exp(g) for g in g_last]
    vb = v.astype(BF16)
    zero = jnp.zeros((chunk, d_head), BF16)
    out, cur = [], [list(s) for s in states]
    for h in range(HGRN_HEADS):
        hs = slice(h * d_head, (h + 1) * d_head)
        o_intra, kv = [], []
        for rs in subs:
            a = jnp.where(causal, _dot_nt(q_dec[rs, hs], k_dec[rs, hs]), 0.0).astype(BF16)
            o_intra.append(_dot(a, vb[rs, hs]))
            ke = k_end[rs, hs]
            k_blocks = jnp.concatenate(
                [jnp.concatenate([ke[j * chunk:(j + 1) * chunk] if i == j else zero
                                  for i in range(per_sub)], axis=1) for j in range(per_sub)], axis=0)
            kv.append(_dot_tn(vb[rs, hs], k_blocks))
        st_in = []
        for c in range(n_c):
            seq, j = c // chunks_per_seq, c % per_sub
            st_in.append(cur[seq][h])
            cur[seq][h] = st_in[c] * decay[c][:, hs] + kv[c // per_sub][:, j * d_head:(j + 1) * d_head]
        o = []
        for si, rs in enumerate(subs):
            stack = jnp.concatenate([s.astype(BF16) for s in st_in[si * per_sub:(si + 1) * per_sub]], axis=0)
            o_inter = _dot_nt(q_dec[rs, hs], stack)
            o.append(o_intra[si] + jnp.concatenate(
                [o_inter[j * chunk:(j + 1) * chunk, j * d_head:(j + 1) * d_head] for j in range(per_sub)],
                axis=0))
        o = jnp.concatenate(o, axis=0)
        on = o * lax.rsqrt(jnp.mean(o * o, axis=-1, keepdims=True) + EPS) * hn[:, hs]
        out.append((on * _silu(zg[:, hs])).astype(BF16))
    return out, cur


def _mix_kernel(nb, tt, chunk,
                x_ref, ng_ref, win_ref, cw_ref, cb_ref, lng_ref, lnb_ref, hlb_ref, hn_ref, wout_ref,
                sconv_ref, shgrn_ref,
                o_ref, oconv_ref, ohgrn_ref,
                zbuf, ubuf, ushift, wtap, sbuf, cat):
    t = pl.program_id(1)
    d_conv = cw_ref.shape[1]
    d_head = sbuf.shape[-1]
    m = nb * tt
    first = HALO - CONV_BUF

    @pl.when((pl.program_id(0) == 0) & (t == 0))
    def _spread_taps():
        wtap[...] = jnp.broadcast_to(cw_ref[...][:, None, :], wtap.shape)

    @pl.when(t == 0)
    def _load_state():
        ubuf[:, 0:first, :] = jnp.zeros((nb, first, d_conv), F32)
        ubuf[:, first:HALO, :] = sconv_ref[...]
        for b in range(nb):
            for h in range(HGRN_HEADS):
                sbuf[b, h] = shgrn_ref[b, h].T

    x = x_ref[...].reshape(m, x_ref.shape[-1])
    zbuf[...] = _dot(_rms(x, ng_ref[...]).astype(BF16), win_ref[...])

    for b in range(nb):
        r = slice(b * tt, (b + 1) * tt)
        ubuf[b, HALO:HALO + tt, :] = zbuf[r, 0:d_conv] * _sigmoid(zbuf[r, d_conv:2 * d_conv])
    cb = cb_ref[...]
    lng = lng_ref[...]
    lnb = lnb_ref[...]
    n_shift = ushift.shape[2]
    for r in range(1, SUBLANES):
        ushift[r - 1] = ubuf[:, r:r + n_shift, :]
    for b in range(nb):
        for r0 in range(0, tt, CONV_ROWS):
            acc = jnp.broadcast_to(cb, (CONV_ROWS // SUBLANES, SUBLANES, d_conv))
            for j in range(CONV_WIDTH):
                a8, r = divmod(first + j, SUBLANES)
                lo = r0 + a8 * SUBLANES
                src = ubuf[b, lo:lo + CONV_ROWS, :] if r == 0 else ushift[r - 1, b, lo:lo + CONV_ROWS, :]
                acc = acc + wtap[j] * src.reshape(acc.shape)
            acc = acc.reshape(CONV_ROWS, d_conv)
            mu = jnp.mean(acc, axis=-1, keepdims=True)
            dev = acc - mu
            var = jnp.mean(dev * dev, axis=-1, keepdims=True)
            yc = _silu(dev * lax.rsqrt(var + EPS) * lng + lnb)
            cat[b * tt + r0:b * tt + r0 + CONV_ROWS, 0:d_conv] = yc.astype(BF16)
    new_rows = ubuf[:, tt + first:tt + HALO, :]
    ubuf[:, first:HALO, :] = new_rows

    hlb = hlb_ref[...]
    e = jnp.exp(hlb - jnp.max(hlb, axis=0, keepdims=True))
    lbv = e[0:1, :] / jnp.sum(e, axis=0, keepdims=True)
    shift = chunk.bit_length() - 1
    assert chunk == 1 << shift
    sub = min(m, HGRN_ROWS)
    row_i = lax.broadcasted_iota(jnp.int32, (sub, sub), 0)
    col_i = lax.broadcasted_iota(jnp.int32, (sub, sub), 1)
    causal = (row_i >= col_i) & (lax.shift_right_logical(row_i, shift) == lax.shift_right_logical(col_i, shift))
    masks = (jnp.where(causal, 1.0, 0.0).astype(BF16), causal)
    heads, state = _hgrn_tile(zbuf[:, 2 * d_conv:], lbv, hn_ref[...], masks,
                              [[sbuf[b, h] for h in range(HGRN_HEADS)] for b in range(nb)],
                              chunk, tt // chunk, sub, d_head)
    for h in range(HGRN_HEADS):
        cat[:, d_conv + h * d_head:d_conv + (h + 1) * d_head] = heads[h]
        for b in range(nb):
            sbuf[b, h] = state[b][h]

    o_ref[...] = (x + _dot(cat[...], wout_ref[...])).reshape(o_ref.shape)

    @pl.when(t == pl.num_programs(1) - 1)
    def _store_state():
        oconv_ref[...] = ubuf[:, first:HALO, :]
        for b in range(nb):
            for h in range(HGRN_HEADS):
                ohgrn_ref[b, h] = sbuf[b, h].T


def _mix_call(x, ng, win, cw, cb, lng, lnb, hlb, hn, wout, sconv, shgrn, nb, tt):
    bsz, seq, d = x.shape
    d_conv = cw.shape[1]
    d_head = shgrn.shape[-1]
    chunk = min(CHUNK, seq)
    kern = functools.partial(_mix_kernel, nb, tt, chunk)
    xspec = pl.BlockSpec((nb, tt, d), lambda i, t: (i, t, 0))
    cspec = pl.BlockSpec((nb, CONV_BUF, d_conv), lambda i, t: (i, 0, 0))
    sspec = pl.BlockSpec((nb, HGRN_HEADS, d_head, d_head), lambda i, t: (i, 0, 0, 0))
    return pl.pallas_call(
        kern,
        grid=(bsz // nb, seq // tt),
        in_specs=[xspec] + [_const_spec(a.shape) for a in (ng, win, cw, cb, lng, lnb, hlb, hn, wout)]
                 + [cspec, sspec],
        out_specs=[xspec, cspec, sspec],
        out_shape=[jax.ShapeDtypeStruct(x.shape, F32),
                   jax.ShapeDtypeStruct(sconv.shape, F32),
                   jax.ShapeDtypeStruct(shgrn.shape, F32)],
        scratch_shapes=[pltpu.VMEM((nb * tt, win.shape[1]), F32),
                        pltpu.VMEM((nb, HALO + tt, d_conv), F32),
                        pltpu.VMEM((SUBLANES - 1, nb, HALO + tt - SUBLANES, d_conv), F32),
                        pltpu.VMEM((CONV_WIDTH, SUBLANES, d_conv), F32),
                        pltpu.VMEM((nb, HGRN_HEADS, d_head, d_head), F32),
                        pltpu.VMEM((nb * tt, d), BF16)],
        compiler_params=pltpu.CompilerParams(dimension_semantics=("arbitrary", "arbitrary"),
                                             vmem_limit_bytes=VMEM_LIMIT),
        name="mix",
    )(x, ng, win, cw, cb, lng, lnb, hlb, hn, wout, sconv, shgrn)


def _tail_kernel(nb, tt, n_split, kv_native,
                 x_ref, k_ref, v_ref, xg_ref, wq_ref, wo_ref, ng_ref, wg_ref, wu_ref, wd_ref, fg_ref,
                 o_ref, *scratch):
    d = x_ref.shape[-1]
    dh = d // MEM_HEADS
    scale = dh ** -0.5
    if kv_native:
        kvbuf, sem = scratch
        i, n = pl.program_id(0), pl.num_programs(0)

        def kv_copies(blk, slot):
            return [pltpu.make_async_copy(src.at[pl.ds(blk * nb, nb), :, h, :],
                                          kvbuf.at[slot, j, :, :, h * dh:(h + 1) * dh],
                                          sem.at[slot, j, h])
                    for j, src in enumerate((k_ref, v_ref)) for h in range(MEM_HEADS)]

        def start_all(copies):
            for c, cp in enumerate(copies):
                cp.start(priority=c % 2)

        @pl.when(i == 0)
        def _first_block():
            start_all(kv_copies(0, 0))

        @pl.when(i + 1 < n)
        def _next_block():
            start_all(kv_copies(i + 1, lax.rem(i + 1, 2)))

        slot = lax.rem(i, 2)
        for cp in kv_copies(i, slot):
            cp.wait()
        k_ref, v_ref = kvbuf.at[slot, 0], kvbuf.at[slot, 1]
    kb = [k_ref[b].astype(BF16) for b in range(nb)]
    vb = [v_ref[b].astype(BF16) for b in range(nb)]
    for g0, g1 in _row_groups(nb * tt, n_split):
        if nb == 1:
            rows, blk = (0, slice(g0, g1)), (g1 - g0, d)
        else:
            assert g0 % tt == 0 and g1 % tt == 0
            rows, blk = (slice(g0 // tt, g1 // tt),), ((g1 - g0) // tt, tt, d)
        x = x_ref[rows].reshape(g1 - g0, d)
        q = _dot(_rms(x, xg_ref[...]).astype(BF16), wq_ref[...])
        hsl = [slice(h * dh, (h + 1) * dh) for h in range(MEM_HEADS)]
        segs = [(b, max(b * tt, g0) - g0, min((b + 1) * tt, g1) - g0) for b in range(g0 // tt, -(-g1 // tt))]
        qb = q.astype(BF16)
        stacked = tt * MEM_HEADS <= STACK_ROWS
        scores = []
        for b, lo, hi in segs:
            if stacked:
                zq = jnp.zeros((hi - lo, dh), BF16)
                q_stack = jnp.concatenate(
                    [jnp.concatenate([qb[lo:hi, hs] if j == h else zq for j in range(MEM_HEADS)], axis=1)
                     for h, hs in enumerate(hsl)], axis=0)
                scores.append(_dot_nt(q_stack, kb[b]))
            else:
                scores += [_dot_nt(qb[lo:hi, hs], kb[b][:, hs]) for hs in hsl]
        sc = jnp.concatenate(scores, axis=0) * scale
        ex = jnp.exp(sc - jnp.max(sc, axis=-1, keepdims=True))
        p = (ex / jnp.sum(ex, axis=-1, keepdims=True)).astype(BF16)
        att, r0 = [], 0
        for b, lo, hi in segs:
            n = hi - lo
            if stacked:
                pv = _dot(p[r0:r0 + MEM_HEADS * n], vb[b])
                heads = [pv[h * n:(h + 1) * n, hs] for h, hs in enumerate(hsl)]
            else:
                heads = [_dot(p[r0 + h * n:r0 + (h + 1) * n], vb[b][:, hs]) for h, hs in enumerate(hsl)]
            r0 += MEM_HEADS * n
            att.append(jnp.concatenate(heads, axis=1).astype(BF16))
        x = x + _dot(jnp.concatenate(att, axis=0), wo_ref[...])
        x = _ffn(x, ng_ref[...], wg_ref[...], wu_ref[...], wd_ref[...])
        o_ref[rows] = _rms(x, fg_ref[...]).reshape(blk)


def _tail_call(x, k, v, xg, wq, wo, ng, wg, wu, wd, fg, nb, tt, n_split):
    bsz, seq, d = x.shape
    kv_native = k.ndim == 4
    kern = functools.partial(_tail_kernel, nb, tt, n_split, kv_native)
    xspec = pl.BlockSpec((nb, tt, d), lambda i, t: (i, t, 0))
    if kv_native:
        assert seq == tt, "the key/value copies are pipelined over the batch axis of the grid"
        kvspec = pl.BlockSpec(memory_space=pl.ANY)
        scratch = [pltpu.VMEM((2, 2, nb, k.shape[1], d), F32), pltpu.SemaphoreType.DMA((2, 2, MEM_HEADS))]
    else:
        kvspec = pl.BlockSpec((nb,) + k.shape[1:], lambda i, t: (i, 0, 0))
        scratch = []
    return pl.pallas_call(
        kern,
        grid=(bsz // nb, seq // tt),
        in_specs=[xspec, kvspec, kvspec]
                 + [_const_spec(a.shape) for a in (xg, wq, wo, ng, wg, wu, wd, fg)],
        out_specs=xspec,
        out_shape=jax.ShapeDtypeStruct(x.shape, F32),
        scratch_shapes=scratch,
        compiler_params=pltpu.CompilerParams(dimension_semantics=("arbitrary", "arbitrary"),
                                             vmem_limit_bytes=VMEM_LIMIT),
        name="tail",
    )(x, k, v, xg, wq, wo, ng, wg, wu, wd, fg)


def _mix_tail(x, sconv, shgrn, mk, mv, w, fg, mix_tile, tail_tile):
    x, new_conv, new_hgrn = _mix_call(x, w["mix_norm"], w["w_in"], w["conv_dw"], w["conv_dw_b"],
                                      w["conv_ln_g"], w["conv_ln_b"], w["hgrn_lb"], w["hgrn_norm"],
                                      w["w_out"], sconv, shgrn, *mix_tile)
    y = _tail_call(x, mk, mv, w["xattn_norm"], w["xattn_wq"], w["xattn_wo"], w["ffn2_norm"],
                   w["ffn2_w_gate"], w["ffn2_w_up"], w["ffn2_w_down"], fg, *tail_tile)
    return y, new_conv, new_hgrn


def kernel(x_prompt, x_sample, mem_prompt, state_conv, state_hgrn, cache_mem_k, cache_mem_v, ffn1_norm, ffn1_w_gate, ffn1_w_up, ffn1_w_down, mix_norm, w_in, conv_dw, conv_dw_b, conv_ln_g, conv_ln_b, hgrn_lb, hgrn_norm, w_out, mem_norm, mem_wk, mem_wv, xattn_norm, xattn_wq, xattn_wo, ffn2_norm, ffn2_w_gate, ffn2_w_up, ffn2_w_down, final_norm):
    depth = ffn1_norm.shape[0]
    assert depth == 1, "the per-layer forget-gate bound is taken from row 0 of the cumulative softmax"
    bp, seq, d = x_prompt.shape
    bs, dec_seq, _ = x_sample.shape
    n_mem = mem_prompt.shape[1]
    d_hgrn = hgrn_lb.shape[1]
    d_conv = conv_dw.shape[2]
    d_head = d_hgrn // HGRN_HEADS

    def row(a):
        return a.reshape(1, -1)

    fg = row(final_norm)
    (mk, mv, mk_flat, mv_flat), ffn1_w = _memkv_call(
        mem_prompt, row(mem_norm[0]), mem_wk[0].astype(BF16), mem_wv[0].astype(BF16), 2,
        cast=(ffn1_w_gate[0], ffn1_w_up[0], ffn1_w_down[0]))

    ffn1 = (row(ffn1_norm[0]), *ffn1_w)
    later = {"w_in": w_in[0], "w_out": w_out[0], "xattn_wq": xattn_wq[0], "xattn_wo": xattn_wo[0],
             "ffn2_w_gate": ffn2_w_gate[0], "ffn2_w_up": ffn2_w_up[0], "ffn2_w_down": ffn2_w_down[0]}
    xp, later_bf16 = _ffn_call(x_prompt.reshape(bp * seq, d), *ffn1, 1024, 4, cast=tuple(later.values()))
    xs, _ = _ffn_call(x_sample.reshape(bs * dec_seq, d), *ffn1, 256, 1)
    w = dict(zip(later, later_bf16))
    w.update({
        "mix_norm": row(mix_norm[0]), "conv_dw": conv_dw[0], "conv_dw_b": row(conv_dw_b[0]),
        "conv_ln_g": row(conv_ln_g[0]), "conv_ln_b": row(conv_ln_b[0]),
        "hgrn_lb": hgrn_lb, "hgrn_norm": row(hgrn_norm[0]),
        "xattn_norm": row(xattn_norm[0]), "ffn2_norm": row(ffn2_norm[0]),
    })

    y_p, conv_p, hgrn_p = _mix_tail(
        xp.reshape(bp, seq, d), jnp.zeros((bp, CONV_BUF, d_conv), F32),
        jnp.zeros((bp, HGRN_HEADS, d_head, d_head), F32), mk_flat, mv_flat, w, fg,
        mix_tile=(1, 512), tail_tile=(1, 1024, 2))
    y_s, conv_s, hgrn_s = _mix_tail(
        xs.reshape(bs, dec_seq, d), state_conv[0], state_hgrn[0],
        cache_mem_k[0], cache_mem_v[0], w, fg,
        mix_tile=(8, dec_seq), tail_tile=(4, dec_seq, 1))

    return (y_p, y_s, conv_p[None], hgrn_p[None], mk[None], mv[None],
            conv_s[None], hgrn_s[None])
```

```python
import functools

import jax
import jax.numpy as jnp
from jax import lax
from jax.experimental import pallas as pl
from jax.experimental.pallas import tpu as pltpu

F32 = jnp.float32
BF16 = jnp.bfloat16

EPS = 1e-6
CHUNK = 64
CONV_WIDTH = 31
CONV_BUF = CONV_WIDTH - 1
HGRN_HEADS = 4
MEM_HEADS = 4
HALO = 32
CONV_ROWS = 32
SUBLANES = 8
BF16_ROWS = 16
HGRN_ROWS = 256
STACK_ROWS = 256
VMEM_LIMIT = 56 * 1024 * 1024


def _dot(a, b):
    return jnp.dot(a, b, preferred_element_type=F32)


def _dot_nt(a, b):
    return lax.dot_general(a, b, (((1,), (1,)), ((), ())), preferred_element_type=F32)


def _dot_tn(a, b):
    return lax.dot_general(a, b, (((0,), (0,)), ((), ())), preferred_element_type=F32)


def _rms(x, g):
    return x * lax.rsqrt(jnp.mean(x * x, axis=-1, keepdims=True) + EPS) * g


def _gated(a, x):
    ha = 0.5 * a
    return ha + ha * jnp.tanh(0.5 * x)


def _silu(x):
    hx = 0.5 * x
    return hx + hx * jnp.tanh(hx)


def _ffn(x, ng, wg, wu, wd):
    xn = _rms(x, ng).astype(BF16)
    h = (_silu(_dot(xn, wg)) * _dot(xn, wu)).astype(BF16)
    return x + 0.5 * _dot(h, wd)


def _const_spec(shape):
    nd = len(shape)
    return pl.BlockSpec(shape, lambda *_: (0,) * nd, pipeline_mode=pl.Buffered(1))


def _row_groups(m, n_split):
    mg = m // n_split
    return [(g * mg, (g + 1) * mg) for g in range(n_split)]


def _cast_specs(arrays, n_steps):
    specs = []
    for a in arrays:
        n_blk = max(n for n in range(1, n_steps + 1) if a.shape[0] % (n * BF16_ROWS) == 0)
        specs.append(pl.BlockSpec((a.shape[0] // n_blk, a.shape[1]),
                                  lambda i, n_blk=n_blk: (jnp.minimum(i, n_blk - 1), 0)))
    return specs


def _cast_rows(src_refs, dst_refs):
    for src, dst in zip(src_refs, dst_refs):
        dst[...] = src[...].astype(BF16)


def _ffn_kernel(n_split, n_cast, x_ref, ng_ref, wg_ref, wu_ref, wd_ref, *rest):
    cast_in, o_ref, cast_out = rest[:n_cast], rest[n_cast], rest[n_cast + 1:]
    for g0, g1 in _row_groups(x_ref.shape[0], n_split):
        o_ref[g0:g1, :] = _ffn(x_ref[g0:g1, :], ng_ref[...], wg_ref[...], wu_ref[...], wd_ref[...])
    _cast_rows(cast_in, cast_out)


def _ffn_call(x, ng, wg, wu, wd, tm, n_split, cast=()):
    m, d = x.shape
    n_steps = m // tm
    cast_specs = _cast_specs(cast, n_steps)
    out = pl.pallas_call(
        functools.partial(_ffn_kernel, n_split, len(cast)),
        grid=(n_steps,),
        in_specs=[pl.BlockSpec((tm, d), lambda i: (i, 0)),
                  _const_spec(ng.shape), _const_spec(wg.shape), _const_spec(wu.shape),
                  _const_spec(wd.shape)] + cast_specs,
        out_specs=[pl.BlockSpec((tm, d), lambda i: (i, 0))] + cast_specs,
        out_shape=[jax.ShapeDtypeStruct((m, d), F32)]
                  + [jax.ShapeDtypeStruct(a.shape, BF16) for a in cast],
        compiler_params=pltpu.CompilerParams(dimension_semantics=("arbitrary",),
                                             vmem_limit_bytes=VMEM_LIMIT),
        name="ffn",
    )(x, ng, wg, wu, wd, *cast)
    return out[0], out[1:]


def _memkv_kernel(n_cast, m_ref, g_ref, wk_ref, wv_ref, *rest):
    cast_in, (k_ref, v_ref, kb_ref, vb_ref), cast_out = rest[:n_cast], rest[n_cast:n_cast + 4], rest[n_cast + 4:]
    _cast_rows(cast_in, cast_out)
    nb, n_mem, n_head, dh = k_ref.shape
    mn = _rms(m_ref[...].reshape(nb * n_mem, m_ref.shape[-1]), g_ref[...]).astype(BF16)
    for w_ref, o_ref, ob_ref in ((wk_ref, k_ref, kb_ref), (wv_ref, v_ref, vb_ref)):
        y = _dot(mn, w_ref[...])
        ob_ref[...] = y.astype(BF16).reshape(ob_ref.shape)
        for h in range(n_head):
            o_ref[:, :, h, :] = y[:, h * dh:(h + 1) * dh].reshape(nb, n_mem, dh)


def _memkv_call(mem, g, wk, wv, nb, cast=()):
    bsz, n_mem, d = mem.shape
    dh = d // MEM_HEADS
    n_steps = bsz // nb
    row = pl.BlockSpec((nb, n_mem, d), lambda i: (i, 0, 0))
    kv = pl.BlockSpec((nb, n_mem, MEM_HEADS, dh), lambda i: (i, 0, 0, 0))
    cast_specs = _cast_specs(cast, n_steps)
    out = pl.pallas_call(
        functools.partial(_memkv_kernel, len(cast)),
        grid=(n_steps,),
        in_specs=[row, _const_spec(g.shape), _const_spec(wk.shape), _const_spec(wv.shape)] + cast_specs,
        out_specs=[kv, kv, row, row] + cast_specs,
        out_shape=[jax.ShapeDtypeStruct((bsz, n_mem, MEM_HEADS, dh), F32)] * 2
                  + [jax.ShapeDtypeStruct((bsz, n_mem, d), BF16)] * 2
                  + [jax.ShapeDtypeStruct(a.shape, BF16) for a in cast],
        compiler_params=pltpu.CompilerParams(dimension_semantics=("arbitrary",),
                                             vmem_limit_bytes=VMEM_LIMIT),
        name="memkv",
    )(mem, g, wk, wv, *cast)
    return out[:4], out[4:]


def _hgrn_tile(z, lbv, hn, masks, states, chunk, chunks_per_seq, sub, d_head):
    m = z.shape[0]
    n_c, per_sub = m // chunk, sub // chunk
    subs = [slice(s0, s0 + sub) for s0 in range(0, m, sub)]
    dh = HGRN_HEADS * d_head
    tri, causal = masks
    zq, zf, v, zg = (z[:, i * dh:(i + 1) * dh] for i in range(4))
    c0, c1 = 0.5 * (1.0 + lbv), 0.5 * (1.0 - lbv)
    ct = c1 * jnp.tanh(0.5 * zf)
    f = c0 + ct
    k = (1.0 - c0) - ct
    q = _silu(zq)
    logf = jnp.log(f)
    l_hi = logf.astype(BF16)
    l_lo = (logf - l_hi.astype(F32)).astype(BF16)
    gc = jnp.concatenate([_dot(tri, l_hi[rs]) + _dot(tri, l_lo[rs]) for rs in subs], axis=0)
    g_last = [gc[(c + 1) * chunk - 1:(c + 1) * chunk, :] for c in range(n_c)]
    g_end = jnp.concatenate([jnp.broadcast_to(g, (chunk, dh)) for g in g_last], axis=0)
    q_dec = (q * jnp.exp(gc)).astype(BF16)
    k_dec = (k * jnp.exp(-gc)).astype(BF16)
    k_end = (k * jnp.exp(g_end - gc)).astype(BF16)
    decay = [jnp.exp(g) for g in g_last]
    vb = v.astype(BF16)
    zero = jnp.zeros((chunk, d_head), BF16)
    out, cur = [], [list(s) for s in states]
    for h in range(HGRN_HEADS):
        hs = slice(h * d_head, (h + 1) * d_head)
        o_intra, kv = [], []
        for rs in subs:
            a = jnp.where(causal, _dot_nt(q_dec[rs, hs], k_dec[rs, hs]), 0.0).astype(BF16)
            o_intra.append(_dot(a, vb[rs, hs]))
            ke = k_end[rs, hs]
            k_blocks = jnp.concatenate(
                [jnp.concatenate([ke[j * chunk:(j + 1) * chunk] if i == j else zero
                                  for i in range(per_sub)], axis=1) for j in range(per_sub)], axis=0)
            kv.append(_dot_tn(vb[rs, hs], k_blocks))
        st_in = []
        for c in range(n_c):
            seq, j = c // chunks_per_seq, c % per_sub
            st_in.append(cur[seq][h])
            cur[seq][h] = st_in[c] * decay[c][:, hs] + kv[c // per_sub][:, j * d_head:(j + 1) * d_head]
        o = []
        for si, rs in enumerate(subs):
            stack = jnp.concatenate([s.astype(BF16) for s in st_in[si * per_sub:(si + 1) * per_sub]], axis=0)
            o_inter = _dot_nt(q_dec[rs, hs], stack)
            o.append(o_intra[si] + jnp.concatenate(
                [o_inter[j * chunk:(j + 1) * chunk, j * d_head:(j + 1) * d_head] for j in range(per_sub)],
                axis=0))
        o = jnp.concatenate(o, axis=0)
        on = o * lax.rsqrt(jnp.mean(o * o, axis=-1, keepdims=True) + EPS) * hn[:, hs]
        out.append((on * _silu(zg[:, hs])).astype(BF16))
    return out, cur


def _mix_kernel(nb, tt, chunk,
                x_ref, ng_ref, win_ref, cw_ref, cb_ref, lng_ref, lnb_ref, hlb_ref, hn_ref, wout_ref,
                sconv_ref, shgrn_ref,
                o_ref, oconv_ref, ohgrn_ref,
                zbuf, ubuf, ushift, wtap, sbuf, cat):
    t = pl.program_id(1)
    d_conv = cw_ref.shape[1]
    d_head = sbuf.shape[-1]
    m = nb * tt
    first = HALO - CONV_BUF

    @pl.when((pl.program_id(0) == 0) & (t == 0))
    def _spread_taps():
        wtap[...] = jnp.broadcast_to(cw_ref[...][:, None, :], wtap.shape)

    @pl.when(t == 0)
    def _load_state():
        ubuf[:, 0:first, :] = jnp.zeros((nb, first, d_conv), F32)
        ubuf[:, first:HALO, :] = sconv_ref[...]
        for b in range(nb):
            for h in range(HGRN_HEADS):
                sbuf[b, h] = shgrn_ref[b, h].T

    x = x_ref[...].reshape(m, x_ref.shape[-1])
    zbuf[...] = _dot(_rms(x, ng_ref[...]).astype(BF16), win_ref[...])

    for b in range(nb):
        r = slice(b * tt, (b + 1) * tt)
        ubuf[b, HALO:HALO + tt, :] = _gated(zbuf[r, 0:d_conv], zbuf[r, d_conv:2 * d_conv])
    cb = cb_ref[...]
    lng = lng_ref[...]
    lnb = lnb_ref[...]
    n_shift = ushift.shape[2]
    for r in range(1, SUBLANES):
        ushift[r - 1] = ubuf[:, r:r + n_shift, :]
    for b in range(nb):
        for r0 in range(0, tt, CONV_ROWS):
            acc = jnp.broadcast_to(cb, (CONV_ROWS // SUBLANES, SUBLANES, d_conv))
            for j in range(CONV_WIDTH):
                a8, r = divmod(first + j, SUBLANES)
                lo = r0 + a8 * SUBLANES
                src = ubuf[b, lo:lo + CONV_ROWS, :] if r == 0 else ushift[r - 1, b, lo:lo + CONV_ROWS, :]
                acc = acc + wtap[j] * src.reshape(acc.shape)
            acc = acc.reshape(CONV_ROWS, d_conv)
            mu = jnp.mean(acc, axis=-1, keepdims=True)
            dev = acc - mu
            var = jnp.mean(dev * dev, axis=-1, keepdims=True)
            yc = _silu(dev * lax.rsqrt(var + EPS) * lng + lnb)
            cat[b * tt + r0:b * tt + r0 + CONV_ROWS, 0:d_conv] = yc.astype(BF16)
    new_rows = ubuf[:, tt + first:tt + HALO, :]
    ubuf[:, first:HALO, :] = new_rows

    hlb = hlb_ref[...]
    e = jnp.exp(hlb - jnp.max(hlb, axis=0, keepdims=True))
    lbv = e[0:1, :] / jnp.sum(e, axis=0, keepdims=True)
    shift = chunk.bit_length() - 1
    assert chunk == 1 << shift
    sub = min(m, HGRN_ROWS)
    row_i = lax.broadcasted_iota(jnp.int32, (sub, sub), 0)
    col_i = lax.broadcasted_iota(jnp.int32, (sub, sub), 1)
    causal = (row_i >= col_i) & (lax.shift_right_logical(row_i, shift) == lax.shift_right_logical(col_i, shift))
    masks = (jnp.where(causal, 1.0, 0.0).astype(BF16), causal)
    heads, state = _hgrn_tile(zbuf[:, 2 * d_conv:], lbv, hn_ref[...], masks,
                              [[sbuf[b, h] for h in range(HGRN_HEADS)] for b in range(nb)],
                              chunk, tt // chunk, sub, d_head)
    for h in range(HGRN_HEADS):
        cat[:, d_conv + h * d_head:d_conv + (h + 1) * d_head] = heads[h]
        for b in range(nb):
            sbuf[b, h] = state[b][h]

    o_ref[...] = (x + _dot(cat[...], wout_ref[...])).reshape(o_ref.shape)

    @pl.when(t == pl.num_programs(1) - 1)
    def _store_state():
        oconv_ref[...] = ubuf[:, first:HALO, :]
        for b in range(nb):
            for h in range(HGRN_HEADS):
                ohgrn_ref[b, h] = sbuf[b, h].T


def _mix_call(x, ng, win, cw, cb, lng, lnb, hlb, hn, wout, sconv, shgrn, nb, tt):
    bsz, seq, d = x.shape
    d_conv = cw.shape[1]
    d_head = shgrn.shape[-1]
    chunk = min(CHUNK, seq)
    kern = functools.partial(_mix_kernel, nb, tt, chunk)
    xspec = pl.BlockSpec((nb, tt, d), lambda i, t: (i, t, 0))
    cspec = pl.BlockSpec((nb, CONV_BUF, d_conv), lambda i, t: (i, 0, 0))
    sspec = pl.BlockSpec((nb, HGRN_HEADS, d_head, d_head), lambda i, t: (i, 0, 0, 0))
    return pl.pallas_call(
        kern,
        grid=(bsz // nb, seq // tt),
        in_specs=[xspec] + [_const_spec(a.shape) for a in (ng, win, cw, cb, lng, lnb, hlb, hn, wout)]
                 + [cspec, sspec],
        out_specs=[xspec, cspec, sspec],
        out_shape=[jax.ShapeDtypeStruct(x.shape, F32),
                   jax.ShapeDtypeStruct(sconv.shape, F32),
                   jax.ShapeDtypeStruct(shgrn.shape, F32)],
        scratch_shapes=[pltpu.VMEM((nb * tt, win.shape[1]), F32),
                        pltpu.VMEM((nb, HALO + tt, d_conv), F32),
                        pltpu.VMEM((SUBLANES - 1, nb, HALO + tt - SUBLANES, d_conv), F32),
                        pltpu.VMEM((CONV_WIDTH, SUBLANES, d_conv), F32),
                        pltpu.VMEM((nb, HGRN_HEADS, d_head, d_head), F32),
                        pltpu.VMEM((nb * tt, d), BF16)],
        compiler_params=pltpu.CompilerParams(dimension_semantics=("arbitrary", "arbitrary"),
                                             vmem_limit_bytes=VMEM_LIMIT),
        name="mix",
    )(x, ng, win, cw, cb, lng, lnb, hlb, hn, wout, sconv, shgrn)


def _tail_kernel(nb, tt, n_split, kv_native,
                 x_ref, k_ref, v_ref, xg_ref, wq_ref, wo_ref, ng_ref, wg_ref, wu_ref, wd_ref, fg_ref,
                 o_ref, *scratch):
    d = x_ref.shape[-1]
    dh = d // MEM_HEADS
    scale = dh ** -0.5
    if kv_native:
        kvbuf, sem = scratch
        i, n = pl.program_id(0), pl.num_programs(0)

        def kv_copies(blk, slot):
            return [pltpu.make_async_copy(src.at[pl.ds(blk * nb, nb), :, h, :],
                                          kvbuf.at[slot, j, :, :, h * dh:(h + 1) * dh],
                                          sem.at[slot, j, h])
                    for j, src in enumerate((k_ref, v_ref)) for h in range(MEM_HEADS)]

        def start_all(copies):
            for c, cp in enumerate(copies):
                cp.start(priority=c % 2)

        @pl.when(i == 0)
        def _first_block():
            start_all(kv_copies(0, 0))

        @pl.when(i + 1 < n)
        def _next_block():
            start_all(kv_copies(i + 1, lax.rem(i + 1, 2)))

        slot = lax.rem(i, 2)
        for cp in kv_copies(i, slot):
            cp.wait()
        k_ref, v_ref = kvbuf.at[slot, 0], kvbuf.at[slot, 1]
    kb = [k_ref[b].astype(BF16) for b in range(nb)]
    vb = [v_ref[b].astype(BF16) for b in range(nb)]
    for g0, g1 in _row_groups(nb * tt, n_split):
        if nb == 1:
            rows, blk = (0, slice(g0, g1)), (g1 - g0, d)
        else:
            assert g0 % tt == 0 and g1 % tt == 0
            rows, blk = (slice(g0 // tt, g1 // tt),), ((g1 - g0) // tt, tt, d)
        x = x_ref[rows].reshape(g1 - g0, d)
        q = _dot(_rms(x, xg_ref[...]).astype(BF16), wq_ref[...])
        hsl = [slice(h * dh, (h + 1) * dh) for h in range(MEM_HEADS)]
        segs = [(b, max(b * tt, g0) - g0, min((b + 1) * tt, g1) - g0) for b in range(g0 // tt, -(-g1 // tt))]
        qb = (q * scale).astype(BF16)
        stacked = tt * MEM_HEADS <= STACK_ROWS
        scores = []
        for b, lo, hi in segs:
            if stacked:
                zq = jnp.zeros((hi - lo, dh), BF16)
                q_stack = jnp.concatenate(
                    [jnp.concatenate([qb[lo:hi, hs] if j == h else zq for j in range(MEM_HEADS)], axis=1)
                     for h, hs in enumerate(hsl)], axis=0)
                scores.append(_dot_nt(q_stack, kb[b]))
            else:
                scores += [_dot_nt(qb[lo:hi, hs], kb[b][:, hs]) for hs in hsl]
        sc = jnp.concatenate(scores, axis=0)
        ex = jnp.exp(sc - jnp.max(sc, axis=-1, keepdims=True))
        p = (ex * (1.0 / jnp.sum(ex, axis=-1, keepdims=True))).astype(BF16)
        att, r0 = [], 0
        for b, lo, hi in segs:
            n = hi - lo
            if stacked:
                pv = _dot(p[r0:r0 + MEM_HEADS * n], vb[b])
                heads = [pv[h * n:(h + 1) * n, hs] for h, hs in enumerate(hsl)]
            else:
                heads = [_dot(p[r0 + h * n:r0 + (h + 1) * n], vb[b][:, hs]) for h, hs in enumerate(hsl)]
            r0 += MEM_HEADS * n
            att.append(jnp.concatenate(heads, axis=1).astype(BF16))
        x = x + _dot(jnp.concatenate(att, axis=0), wo_ref[...])
        x = _ffn(x, ng_ref[...], wg_ref[...], wu_ref[...], wd_ref[...])
        o_ref[rows] = _rms(x, fg_ref[...]).reshape(blk)


def _tail_call(x, k, v, xg, wq, wo, ng, wg, wu, wd, fg, nb, tt, n_split):
    bsz, seq, d = x.shape
    kv_native = k.ndim == 4
    kern = functools.partial(_tail_kernel, nb, tt, n_split, kv_native)
    xspec = pl.BlockSpec((nb, tt, d), lambda i, t: (i, t, 0))
    if kv_native:
        assert seq == tt, "the key/value copies are pipelined over the batch axis of the grid"
        kvspec = pl.BlockSpec(memory_space=pl.ANY)
        scratch = [pltpu.VMEM((2, 2, nb, k.shape[1], d), F32), pltpu.SemaphoreType.DMA((2, 2, MEM_HEADS))]
    else:
        kvspec = pl.BlockSpec((nb,) + k.shape[1:], lambda i, t: (i, 0, 0))
        scratch = []
    return pl.pallas_call(
        kern,
        grid=(bsz // nb, seq // tt),
        in_specs=[xspec, kvspec, kvspec]
                 + [_const_spec(a.shape) for a in (xg, wq, wo, ng, wg, wu, wd, fg)],
        out_specs=xspec,
        out_shape=jax.ShapeDtypeStruct(x.shape, F32),
        scratch_shapes=scratch,
        compiler_params=pltpu.CompilerParams(dimension_semantics=("arbitrary", "arbitrary"),
                                             vmem_limit_bytes=VMEM_LIMIT),
        name="tail",
    )(x, k, v, xg, wq, wo, ng, wg, wu, wd, fg)


def _mix_tail(x, sconv, shgrn, mk, mv, w, fg, mix_tile, tail_tile):
    x, new_conv, new_hgrn = _mix_call(x, w["mix_norm"], w["w_in"], w["conv_dw"], w["conv_dw_b"],
                                      w["conv_ln_g"], w["conv_ln_b"], w["hgrn_lb"], w["hgrn_norm"],
                                      w["w_out"], sconv, shgrn, *mix_tile)
    y = _tail_call(x, mk, mv, w["xattn_norm"], w["xattn_wq"], w["xattn_wo"], w["ffn2_norm"],
                   w["ffn2_w_gate"], w["ffn2_w_up"], w["ffn2_w_down"], fg, *tail_tile)
    return y, new_conv, new_hgrn


def kernel(x_prompt, x_sample, mem_prompt, state_conv, state_hgrn, cache_mem_k, cache_mem_v, ffn1_norm, ffn1_w_gate, ffn1_w_up, ffn1_w_down, mix_norm, w_in, conv_dw, conv_dw_b, conv_ln_g, conv_ln_b, hgrn_lb, hgrn_norm, w_out, mem_norm, mem_wk, mem_wv, xattn_norm, xattn_wq, xattn_wo, ffn2_norm, ffn2_w_gate, ffn2_w_up, ffn2_w_down, final_norm):
    depth = ffn1_norm.shape[0]
    assert depth == 1, "the per-layer forget-gate bound is taken from row 0 of the cumulative softmax"
    bp, seq, d = x_prompt.shape
    bs, dec_seq, _ = x_sample.shape
    n_mem = mem_prompt.shape[1]
    d_hgrn = hgrn_lb.shape[1]
    d_conv = conv_dw.shape[2]
    d_head = d_hgrn // HGRN_HEADS

    def row(a):
        return a.reshape(1, -1)

    fg = row(final_norm)
    (mk, mv, mk_flat, mv_flat), ffn1_w = _memkv_call(
        mem_prompt, row(mem_norm[0]), mem_wk[0].astype(BF16), mem_wv[0].astype(BF16), 1,
        cast=(ffn1_w_gate[0], ffn1_w_up[0], ffn1_w_down[0]))

    ffn1 = (row(ffn1_norm[0]), *ffn1_w)
    later = {"w_in": w_in[0], "w_out": w_out[0], "xattn_wq": xattn_wq[0], "xattn_wo": xattn_wo[0],
             "ffn2_w_gate": ffn2_w_gate[0], "ffn2_w_up": ffn2_w_up[0], "ffn2_w_down": ffn2_w_down[0]}
    xp, later_bf16 = _ffn_call(x_prompt.reshape(bp * seq, d), *ffn1, 1024, 4, cast=tuple(later.values()))
    xs, _ = _ffn_call(x_sample.reshape(bs * dec_seq, d), *ffn1, 256, 1)
    w = dict(zip(later, later_bf16))
    w.update({
        "mix_norm": row(mix_norm[0]), "conv_dw": conv_dw[0], "conv_dw_b": row(conv_dw_b[0]),
        "conv_ln_g": row(conv_ln_g[0]), "conv_ln_b": row(conv_ln_b[0]),
        "hgrn_lb": hgrn_lb, "hgrn_norm": row(hgrn_norm[0]),
        "xattn_norm": row(xattn_norm[0]), "ffn2_norm": row(ffn2_norm[0]),
    })

    y_p, conv_p, hgrn_p = _mix_tail(
        xp.reshape(bp, seq, d), jnp.zeros((bp, CONV_BUF, d_conv), F32),
        jnp.zeros((bp, HGRN_HEADS, d_head, d_head), F32), mk_flat, mv_flat, w, fg,
        mix_tile=(1, 512), tail_tile=(1, 1024, 2))
    y_s, conv_s, hgrn_s = _mix_tail(
        xs.reshape(bs, dec_seq, d), state_conv[0], state_hgrn[0],
        cache_mem_k[0], cache_mem_v[0], w, fg,
        mix_tile=(8, dec_seq), tail_tile=(4, dec_seq, 1))

    return (y_p, y_s, conv_p[None], hgrn_p[None], mk[None], mv[None],
            conv_s[None], hgrn_s[None])
```

```python
import functools

import jax
import jax.numpy as jnp
from jax import lax
from jax.experimental import pallas as pl
from jax.experimental.pallas import tpu as pltpu

F32 = jnp.float32
BF16 = jnp.bfloat16

EPS = 1e-6
CHUNK = 64
CONV_WIDTH = 31
CONV_BUF = CONV_WIDTH - 1
HGRN_HEADS = 4
MEM_HEADS = 4
HALO = 32
CONV_ROWS = 32
SUBLANES = 8
BF16_ROWS = 16
HGRN_ROWS = 256
STACK_ROWS = 256
VMEM_LIMIT = 56 * 1024 * 1024


def _dot(a, b):
    return jnp.dot(a, b, preferred_element_type=F32)


def _dot_nt(a, b):
    return lax.dot_general(a, b, (((1,), (1,)), ((), ())), preferred_element_type=F32)


def _dot_tn(a, b):
    return lax.dot_general(a, b, (((0,), (0,)), ((), ())), preferred_element_type=F32)


def _rms(x, g):
    return x * lax.rsqrt(jnp.mean(x * x, axis=-1, keepdims=True) + EPS) * g


def _gated(a, x):
    ha = 0.5 * a
    return ha + ha * jnp.tanh(0.5 * x)


def _silu(x):
    hx = 0.5 * x
    return hx + hx * jnp.tanh(hx)


def _ffn(x, ng, wg, wu, wd):
    xn = _rms(x, ng).astype(BF16)
    h = (_silu(_dot(xn, wg)) * _dot(xn, wu)).astype(BF16)
    return x + 0.5 * _dot(h, wd)


def _const_spec(shape):
    nd = len(shape)
    return pl.BlockSpec(shape, lambda *_: (0,) * nd, pipeline_mode=pl.Buffered(1))


def _row_groups(m, n_split):
    mg = m // n_split
    return [(g * mg, (g + 1) * mg) for g in range(n_split)]


def _cast_specs(arrays, n_steps):
    specs = []
    for a in arrays:
        n_blk = max(n for n in range(1, n_steps + 1) if a.shape[0] % (n * BF16_ROWS) == 0)
        specs.append(pl.BlockSpec((a.shape[0] // n_blk, a.shape[1]),
                                  lambda i, n_blk=n_blk: (jnp.minimum(i, n_blk - 1), 0)))
    return specs


def _cast_rows(src_refs, dst_refs):
    for src, dst in zip(src_refs, dst_refs):
        dst[...] = src[...].astype(BF16)


def _ffn_kernel(n_split, n_cast, x_ref, ng_ref, wg_ref, wu_ref, wd_ref, *rest):
    cast_in, o_ref, cast_out = rest[:n_cast], rest[n_cast], rest[n_cast + 1:]
    for g0, g1 in _row_groups(x_ref.shape[0], n_split):
        o_ref[g0:g1, :] = _ffn(x_ref[g0:g1, :], ng_ref[...], wg_ref[...], wu_ref[...], wd_ref[...])
    _cast_rows(cast_in, cast_out)


def _ffn_call(x, ng, wg, wu, wd, tm, n_split, cast=()):
    m, d = x.shape
    n_steps = m // tm
    cast_specs = _cast_specs(cast, n_steps)
    out = pl.pallas_call(
        functools.partial(_ffn_kernel, n_split, len(cast)),
        grid=(n_steps,),
        in_specs=[pl.BlockSpec((tm, d), lambda i: (i, 0)),
                  _const_spec(ng.shape), _const_spec(wg.shape), _const_spec(wu.shape),
                  _const_spec(wd.shape)] + cast_specs,
        out_specs=[pl.BlockSpec((tm, d), lambda i: (i, 0))] + cast_specs,
        out_shape=[jax.ShapeDtypeStruct((m, d), F32)]
                  + [jax.ShapeDtypeStruct(a.shape, BF16) for a in cast],
        compiler_params=pltpu.CompilerParams(dimension_semantics=("arbitrary",),
                                             vmem_limit_bytes=VMEM_LIMIT),
        name="ffn",
    )(x, ng, wg, wu, wd, *cast)
    return out[0], out[1:]


def _memkv_kernel(n_cast, m_ref, g_ref, wk_ref, wv_ref, *rest):
    cast_in, (k_ref, v_ref, kb_ref, vb_ref), cast_out = rest[:n_cast], rest[n_cast:n_cast + 4], rest[n_cast + 4:]
    _cast_rows(cast_in, cast_out)
    nb, n_mem, n_head, dh = k_ref.shape
    mn = _rms(m_ref[...].reshape(nb * n_mem, m_ref.shape[-1]), g_ref[...]).astype(BF16)
    for w_ref, o_ref, ob_ref in ((wk_ref, k_ref, kb_ref), (wv_ref, v_ref, vb_ref)):
        y = _dot(mn, w_ref[...])
        ob_ref[...] = y.astype(BF16).reshape(ob_ref.shape)
        for h in range(n_head):
            o_ref[:, :, h, :] = y[:, h * dh:(h + 1) * dh].reshape(nb, n_mem, dh)


def _memkv_call(mem, g, wk, wv, nb, cast=()):
    bsz, n_mem, d = mem.shape
    dh = d // MEM_HEADS
    n_steps = bsz // nb
    row = pl.BlockSpec((nb, n_mem, d), lambda i: (i, 0, 0))
    kv = pl.BlockSpec((nb, n_mem, MEM_HEADS, dh), lambda i: (i, 0, 0, 0))
    cast_specs = _cast_specs(cast, n_steps)
    out = pl.pallas_call(
        functools.partial(_memkv_kernel, len(cast)),
        grid=(n_steps,),
        in_specs=[row, _const_spec(g.shape), _const_spec(wk.shape), _const_spec(wv.shape)] + cast_specs,
        out_specs=[kv, kv, row, row] + cast_specs,
        out_shape=[jax.ShapeDtypeStruct((bsz, n_mem, MEM_HEADS, dh), F32)] * 2
                  + [jax.ShapeDtypeStruct((bsz, n_mem, d), BF16)] * 2
                  + [jax.ShapeDtypeStruct(a.shape, BF16) for a in cast],
        compiler_params=pltpu.CompilerParams(dimension_semantics=("arbitrary",),
                                             vmem_limit_bytes=VMEM_LIMIT),
        name="memkv",
    )(mem, g, wk, wv, *cast)
    return out[:4], out[4:]


def _hgrn_tile(z, lbv, hn, masks, states, chunk, chunks_per_seq, sub, d_head):
    m = z.shape[0]
    n_c, per_sub = m // chunk, sub // chunk
    subs = [slice(s0, s0 + sub) for s0 in range(0, m, sub)]
    dh = HGRN_HEADS * d_head
    tri, causal = masks
    zq, zf, v, zg = (z[:, i * dh:(i + 1) * dh] for i in range(4))
    c0, c1 = 0.5 * (1.0 + lbv), 0.5 * (1.0 - lbv)
    ct = c1 * jnp.tanh(0.5 * zf)
    f = c0 + ct
    k = (1.0 - c0) - ct
    q = _silu(zq)
    logf = jnp.log(f)
    l_hi = logf.astype(BF16)
    l_lo = (logf - l_hi.astype(F32)).astype(BF16)
    gc = jnp.concatenate([_dot(tri, l_hi[rs]) + _dot(tri, l_lo[rs]) for rs in subs], axis=0)
    g_last = [gc[(c + 1) * chunk - 1:(c + 1) * chunk, :] for c in range(n_c)]
    g_end = jnp.concatenate([jnp.broadcast_to(g, (chunk, dh)) for g in g_last], axis=0)
    q_dec = (q * jnp.exp(gc)).astype(BF16)
    k_dec = (k * jnp.exp(-gc)).astype(BF16)
    k_end = (k * jnp.exp(g_end - gc)).astype(BF16)
    decay = [jnp.exp(g) for g in g_last]
    vb = v.astype(BF16)
    zero = jnp.zeros((chunk, d_head), BF16)
    out, cur = [], [list(s) for s in states]
    for h in range(HGRN_HEADS):
        hs = slice(h * d_head, (h + 1) * d_head)
        o_intra, kv = [], []
        for rs in subs:
            a = jnp.where(causal, _dot_nt(q_dec[rs, hs], k_dec[rs, hs]), 0.0).astype(BF16)
            o_intra.append(_dot(a, vb[rs, hs]))
            ke = k_end[rs, hs]
            k_blocks = jnp.concatenate(
                [jnp.concatenate([ke[j * chunk:(j + 1) * chunk] if i == j else zero
                                  for i in range(per_sub)], axis=1) for j in range(per_sub)], axis=0)
            kv.append(_dot_tn(vb[rs, hs], k_blocks))
        st_in = []
        for c in range(n_c):
            seq, j = c // chunks_per_seq, c % per_sub
            st_in.append(cur[seq][h])
            cur[seq][h] = st_in[c] * decay[c][:, hs] + kv[c // per_sub][:, j * d_head:(j + 1) * d_head]
        o = []
        for si, rs in enumerate(subs):
            stack = jnp.concatenate([s.astype(BF16) for s in st_in[si * per_sub:(si + 1) * per_sub]], axis=0)
            o_inter = _dot_nt(q_dec[rs, hs], stack)
            o.append(o_intra[si] + jnp.concatenate(
                [o_inter[j * chunk:(j + 1) * chunk, j * d_head:(j + 1) * d_head] for j in range(per_sub)],
                axis=0))
        o = jnp.concatenate(o, axis=0)
        on = o * lax.rsqrt(jnp.mean(o * o, axis=-1, keepdims=True) + EPS) * hn[:, hs]
        out.append((on * _silu(zg[:, hs])).astype(BF16))
    return out, cur


def _mix_kernel(nb, tt, chunk,
                x_ref, ng_ref, win_ref, cw_ref, cb_ref, lng_ref, lnb_ref, hlb_ref, hn_ref, wout_ref,
                sconv_ref, shgrn_ref,
                o_ref, oconv_ref, ohgrn_ref,
                zbuf, ubuf, ushift, wtap, sbuf, cat):
    t = pl.program_id(1)
    d_conv = cw_ref.shape[1]
    d_head = sbuf.shape[-1]
    m = nb * tt
    first = HALO - CONV_BUF

    @pl.when((pl.program_id(0) == 0) & (t == 0))
    def _spread_taps():
        wtap[...] = jnp.broadcast_to(cw_ref[...][:, None, :], wtap.shape)

    @pl.when(t == 0)
    def _load_state():
        ubuf[:, 0:first, :] = jnp.zeros((nb, first, d_conv), F32)
        ubuf[:, first:HALO, :] = sconv_ref[...]
        for b in range(nb):
            for h in range(HGRN_HEADS):
                sbuf[b, h] = shgrn_ref[b, h].T

    x = x_ref[...].reshape(m, x_ref.shape[-1])
    zbuf[...] = _dot(_rms(x, ng_ref[...]).astype(BF16), win_ref[...])

    for b in range(nb):
        r = slice(b * tt, (b + 1) * tt)
        ubuf[b, HALO:HALO + tt, :] = _gated(zbuf[r, 0:d_conv], zbuf[r, d_conv:2 * d_conv])
    cb = cb_ref[...]
    lng = lng_ref[...]
    lnb = lnb_ref[...]
    n_shift = ushift.shape[2]
    for r in range(1, SUBLANES):
        ushift[r - 1] = ubuf[:, r:r + n_shift, :]
    for b in range(nb):
        for r0 in range(0, tt, CONV_ROWS):
            acc = jnp.broadcast_to(cb, (CONV_ROWS // SUBLANES, SUBLANES, d_conv))
            for j in range(CONV_WIDTH):
                a8, r = divmod(first + j, SUBLANES)
                lo = r0 + a8 * SUBLANES
                src = ubuf[b, lo:lo + CONV_ROWS, :] if r == 0 else ushift[r - 1, b, lo:lo + CONV_ROWS, :]
                acc = acc + wtap[j] * src.reshape(acc.shape)
            acc = acc.reshape(CONV_ROWS, d_conv)
            mu = jnp.mean(acc, axis=-1, keepdims=True)
            dev = acc - mu
            var = jnp.mean(dev * dev, axis=-1, keepdims=True)
            yc = _silu(dev * lax.rsqrt(var + EPS) * lng + lnb)
            cat[b * tt + r0:b * tt + r0 + CONV_ROWS, 0:d_conv] = yc.astype(BF16)
    new_rows = ubuf[:, tt + first:tt + HALO, :]
    ubuf[:, first:HALO, :] = new_rows

    hlb = hlb_ref[...]
    e = jnp.exp(hlb - jnp.max(hlb, axis=0, keepdims=True))
    lbv = e[0:1, :] / jnp.sum(e, axis=0, keepdims=True)
    shift = chunk.bit_length() - 1
    assert chunk == 1 << shift
    sub = min(m, HGRN_ROWS)
    row_i = lax.broadcasted_iota(jnp.int32, (sub, sub), 0)
    col_i = lax.broadcasted_iota(jnp.int32, (sub, sub), 1)
    causal = (row_i >= col_i) & (lax.shift_right_logical(row_i, shift) == lax.shift_right_logical(col_i, shift))
    masks = (jnp.where(causal, 1.0, 0.0).astype(BF16), causal)
    heads, state = _hgrn_tile(zbuf[:, 2 * d_conv:], lbv, hn_ref[...], masks,
                              [[sbuf[b, h] for h in range(HGRN_HEADS)] for b in range(nb)],
                              chunk, tt // chunk, sub, d_head)
    for h in range(HGRN_HEADS):
        cat[:, d_conv + h * d_head:d_conv + (h + 1) * d_head] = heads[h]
        for b in range(nb):
            sbuf[b, h] = state[b][h]

    o_ref[...] = (x + _dot(cat[...], wout_ref[...])).reshape(o_ref.shape)

    @pl.when(t == pl.num_programs(1) - 1)
    def _store_state():
        oconv_ref[...] = ubuf[:, first:HALO, :]
        for b in range(nb):
            for h in range(HGRN_HEADS):
                ohgrn_ref[b, h] = sbuf[b, h].T


def _mix_call(x, ng, win, cw, cb, lng, lnb, hlb, hn, wout, sconv, shgrn, nb, tt):
    bsz, seq, d = x.shape
    d_conv = cw.shape[1]
    d_head = shgrn.shape[-1]
    chunk = min(CHUNK, seq)
    kern = functools.partial(_mix_kernel, nb, tt, chunk)
    xspec = pl.BlockSpec((nb, tt, d), lambda i, t: (i, t, 0))
    cspec = pl.BlockSpec((nb, CONV_BUF, d_conv), lambda i, t: (i, 0, 0))
    sspec = pl.BlockSpec((nb, HGRN_HEADS, d_head, d_head), lambda i, t: (i, 0, 0, 0))
    return pl.pallas_call(
        kern,
        grid=(bsz // nb, seq // tt),
        in_specs=[xspec] + [_const_spec(a.shape) for a in (ng, win, cw, cb, lng, lnb, hlb, hn, wout)]
                 + [cspec, sspec],
        out_specs=[xspec, cspec, sspec],
        out_shape=[jax.ShapeDtypeStruct(x.shape, F32),
                   jax.ShapeDtypeStruct(sconv.shape, F32),
                   jax.ShapeDtypeStruct(shgrn.shape, F32)],
        scratch_shapes=[pltpu.VMEM((nb * tt, win.shape[1]), F32),
                        pltpu.VMEM((nb, HALO + tt, d_conv), F32),
                        pltpu.VMEM((SUBLANES - 1, nb, HALO + tt - SUBLANES, d_conv), F32),
                        pltpu.VMEM((CONV_WIDTH, SUBLANES, d_conv), F32),
                        pltpu.VMEM((nb, HGRN_HEADS, d_head, d_head), F32),
                        pltpu.VMEM((nb * tt, d), BF16)],
        compiler_params=pltpu.CompilerParams(dimension_semantics=("arbitrary", "arbitrary"),
                                             vmem_limit_bytes=VMEM_LIMIT),
        name="mix",
    )(x, ng, win, cw, cb, lng, lnb, hlb, hn, wout, sconv, shgrn)


def _tail_kernel(nb, tt, n_split, n_flat,
                 x_ref, k_ref, v_ref, xg_ref, wq_ref, wo_ref, ng_ref, wg_ref, wu_ref, wd_ref, fg_ref,
                 *rest):
    d = x_ref.shape[-1]
    dh = d // MEM_HEADS
    scale = dh ** -0.5
    flat_in, o_ref, flat_out = rest[:n_flat], rest[n_flat], rest[n_flat + 1:2 * n_flat + 1]
    step = pl.program_id(0) * pl.num_programs(1) + pl.program_id(1)

    def flat_copies():
        sem = rest[-1]
        return [pltpu.make_async_copy(src.at[:, :, h, :], dst.at[:, :, h * dh:(h + 1) * dh], sem.at[j, h])
                for j, (src, dst) in enumerate(zip(flat_in, flat_out)) for h in range(MEM_HEADS)]

    if n_flat:
        @pl.when(step == 0)
        def _start_flat():
            for c, cp in enumerate(flat_copies()):
                cp.start(priority=c % 2)

    kb = [k_ref[b].astype(BF16) for b in range(nb)]
    vb = [v_ref[b].astype(BF16) for b in range(nb)]
    m = nb * tt
    x_all = x_ref[...].reshape(m, d)
    q = _dot(_rms(x_all, xg_ref[...]).astype(BF16), wq_ref[...])
    hsl = [slice(h * dh, (h + 1) * dh) for h in range(MEM_HEADS)]
    qb = (q * scale).astype(BF16)
    stacked = tt * MEM_HEADS <= STACK_ROWS
    scores = []
    for b in range(nb):
        qr = qb[b * tt:(b + 1) * tt]
        if stacked:
            zq = jnp.zeros((tt, dh), BF16)
            q_stack = jnp.concatenate(
                [jnp.concatenate([qr[:, hs] if j == h else zq for j in range(MEM_HEADS)], axis=1)
                 for h, hs in enumerate(hsl)], axis=0)
            scores.append(_dot_nt(q_stack, kb[b]))
        else:
            scores += [_dot_nt(qr[:, hs], kb[b][:, hs]) for hs in hsl]
    sc = jnp.concatenate(scores, axis=0)
    ex = jnp.exp(sc - jnp.max(sc, axis=-1, keepdims=True))
    p = (ex * (1.0 / jnp.sum(ex, axis=-1, keepdims=True))).astype(BF16)
    att = []
    for b in range(nb):
        pb = p[b * MEM_HEADS * tt:(b + 1) * MEM_HEADS * tt]
        if stacked:
            pv = _dot(pb, vb[b])
            heads = [pv[h * tt:(h + 1) * tt, hs] for h, hs in enumerate(hsl)]
        else:
            heads = [_dot(pb[h * tt:(h + 1) * tt], vb[b][:, hs]) for h, hs in enumerate(hsl)]
        att.append(jnp.concatenate(heads, axis=1).astype(BF16))
    xa = x_all + _dot(jnp.concatenate(att, axis=0), wo_ref[...])
    for g0, g1 in _row_groups(m, n_split):
        if nb == 1:
            rows, blk = (0, slice(g0, g1)), (g1 - g0, d)
        else:
            assert g0 % tt == 0 and g1 % tt == 0
            rows, blk = (slice(g0 // tt, g1 // tt),), ((g1 - g0) // tt, tt, d)
        x = _ffn(xa[g0:g1], ng_ref[...], wg_ref[...], wu_ref[...], wd_ref[...])
        o_ref[rows] = _rms(x, fg_ref[...]).reshape(blk)

    if n_flat:
        @pl.when(step == pl.num_programs(0) * pl.num_programs(1) - 1)
        def _finish_flat():
            for cp in flat_copies():
                cp.wait()


def _tail_call(x, k, v, xg, wq, wo, ng, wg, wu, wd, fg, nb, tt, n_split, flatten=()):
    bsz, seq, d = x.shape
    kern = functools.partial(_tail_kernel, nb, tt, n_split, len(flatten))
    xspec = pl.BlockSpec((nb, tt, d), lambda i, t: (i, t, 0))
    kvspec = pl.BlockSpec((nb,) + k.shape[1:], lambda i, t: (i, 0, 0))
    hbm = pl.BlockSpec(memory_space=pl.ANY)
    out = pl.pallas_call(
        kern,
        grid=(bsz // nb, seq // tt),
        in_specs=[xspec, kvspec, kvspec]
                 + [_const_spec(a.shape) for a in (xg, wq, wo, ng, wg, wu, wd, fg)] + [hbm] * len(flatten),
        out_specs=[xspec] + [hbm] * len(flatten),
        out_shape=[jax.ShapeDtypeStruct(x.shape, F32)]
                  + [jax.ShapeDtypeStruct(a.shape[:2] + (a.shape[2] * a.shape[3],), a.dtype) for a in flatten],
        scratch_shapes=[pltpu.SemaphoreType.DMA((len(flatten), MEM_HEADS))] if flatten else [],
        compiler_params=pltpu.CompilerParams(dimension_semantics=("arbitrary", "arbitrary"),
                                             vmem_limit_bytes=VMEM_LIMIT),
        name="tail",
    )(x, k, v, xg, wq, wo, ng, wg, wu, wd, fg, *flatten)
    return out[0], out[1:]


def _mix_tail(x, sconv, shgrn, mk, mv, w, fg, mix_tile, tail_tile, flatten=()):
    x, new_conv, new_hgrn = _mix_call(x, w["mix_norm"], w["w_in"], w["conv_dw"], w["conv_dw_b"],
                                      w["conv_ln_g"], w["conv_ln_b"], w["hgrn_lb"], w["hgrn_norm"],
                                      w["w_out"], sconv, shgrn, *mix_tile)
    y, flat = _tail_call(x, mk, mv, w["xattn_norm"], w["xattn_wq"], w["xattn_wo"], w["ffn2_norm"],
                         w["ffn2_w_gate"], w["ffn2_w_up"], w["ffn2_w_down"], fg, *tail_tile, flatten=flatten)
    return y, new_conv, new_hgrn, flat


def kernel(x_prompt, x_sample, mem_prompt, state_conv, state_hgrn, cache_mem_k, cache_mem_v, ffn1_norm, ffn1_w_gate, ffn1_w_up, ffn1_w_down, mix_norm, w_in, conv_dw, conv_dw_b, conv_ln_g, conv_ln_b, hgrn_lb, hgrn_norm, w_out, mem_norm, mem_wk, mem_wv, xattn_norm, xattn_wq, xattn_wo, ffn2_norm, ffn2_w_gate, ffn2_w_up, ffn2_w_down, final_norm):
    depth = ffn1_norm.shape[0]
    assert depth == 1, "the per-layer forget-gate bound is taken from row 0 of the cumulative softmax"
    bp, seq, d = x_prompt.shape
    bs, dec_seq, _ = x_sample.shape
    n_mem = mem_prompt.shape[1]
    d_hgrn = hgrn_lb.shape[1]
    d_conv = conv_dw.shape[2]
    d_head = d_hgrn // HGRN_HEADS

    def row(a):
        return a.reshape(1, -1)

    fg = row(final_norm)
    (mk, mv, mk_flat, mv_flat), ffn1_w = _memkv_call(
        mem_prompt, row(mem_norm[0]), mem_wk[0].astype(BF16), mem_wv[0].astype(BF16), 1,
        cast=(ffn1_w_gate[0], ffn1_w_up[0], ffn1_w_down[0]))

    ffn1 = (row(ffn1_norm[0]), *ffn1_w)
    later = {"w_in": w_in[0], "w_out": w_out[0], "xattn_wq": xattn_wq[0], "xattn_wo": xattn_wo[0],
             "ffn2_w_gate": ffn2_w_gate[0], "ffn2_w_up": ffn2_w_up[0], "ffn2_w_down": ffn2_w_down[0]}
    xp, later_bf16 = _ffn_call(x_prompt.reshape(bp * seq, d), *ffn1, 1024, 4, cast=tuple(later.values()))
    xs, _ = _ffn_call(x_sample.reshape(bs * dec_seq, d), *ffn1, 256, 1)
    w = dict(zip(later, later_bf16))
    w.update({
        "mix_norm": row(mix_norm[0]), "conv_dw": conv_dw[0], "conv_dw_b": row(conv_dw_b[0]),
        "conv_ln_g": row(conv_ln_g[0]), "conv_ln_b": row(conv_ln_b[0]),
        "hgrn_lb": hgrn_lb, "hgrn_norm": row(hgrn_norm[0]),
        "xattn_norm": row(xattn_norm[0]), "ffn2_norm": row(ffn2_norm[0]),
    })

    y_p, conv_p, hgrn_p, (ks_flat, vs_flat) = _mix_tail(
        xp.reshape(bp, seq, d), jnp.zeros((bp, CONV_BUF, d_conv), F32),
        jnp.zeros((bp, HGRN_HEADS, d_head, d_head), F32), mk_flat, mv_flat, w, fg,
        mix_tile=(1, 512), tail_tile=(1, 1024, 4), flatten=(cache_mem_k[0], cache_mem_v[0]))
    y_s, conv_s, hgrn_s, _ = _mix_tail(
        xs.reshape(bs, dec_seq, d), state_conv[0], state_hgrn[0], ks_flat, vs_flat, w, fg,
        mix_tile=(8, dec_seq), tail_tile=(4, dec_seq, 1))

    return (y_p, y_s, conv_p[None], hgrn_p[None], mk[None], mv[None],
            conv_s[None], hgrn_s[None])
```

```python
import functools

import jax
import jax.numpy as jnp
from jax import lax
from jax.experimental import pallas as pl
from jax.experimental.pallas import tpu as pltpu

F32 = jnp.float32
BF16 = jnp.bfloat16

EPS = 1e-6
CHUNK = 64
CONV_WIDTH = 31
CONV_BUF = CONV_WIDTH - 1
HGRN_HEADS = 4
MEM_HEADS = 4
HALO = 32
CONV_ROWS = 32
SUBLANES = 8
BF16_ROWS = 16
HGRN_ROWS = 256
STACK_ROWS = 256
VMEM_LIMIT = 56 * 1024 * 1024


def _dot(a, b):
    return jnp.dot(a, b, preferred_element_type=F32)


def _dot_nt(a, b):
    return lax.dot_general(a, b, (((1,), (1,)), ((), ())), preferred_element_type=F32)


def _dot_tn(a, b):
    return lax.dot_general(a, b, (((0,), (0,)), ((), ())), preferred_element_type=F32)


def _rms(x, g):
    return x * lax.rsqrt(jnp.mean(x * x, axis=-1, keepdims=True) + EPS) * g


def _gated(a, x):
    ha = 0.5 * a
    return ha + ha * jnp.tanh(0.5 * x)


def _silu(x):
    hx = 0.5 * x
    return hx + hx * jnp.tanh(hx)


def _ffn(x, ng, wg, wu, wd):
    xn = _rms(x, ng).astype(BF16)
    h = (_silu(_dot(xn, wg)) * _dot(xn, wu)).astype(BF16)
    return x + 0.5 * _dot(h, wd)


def _const_spec(shape):
    nd = len(shape)
    return pl.BlockSpec(shape, lambda *_: (0,) * nd, pipeline_mode=pl.Buffered(1))


def _row_groups(m, n_split):
    mg = m // n_split
    return [(g * mg, (g + 1) * mg) for g in range(n_split)]


def _cast_specs(arrays, n_steps):
    specs = []
    for a in arrays:
        n_blk = max(n for n in range(1, n_steps + 1) if a.shape[0] % (n * BF16_ROWS) == 0)
        specs.append(pl.BlockSpec((a.shape[0] // n_blk, a.shape[1]),
                                  lambda i, n_blk=n_blk: (jnp.minimum(i, n_blk - 1), 0)))
    return specs


def _cast_rows(src_refs, dst_refs):
    for src, dst in zip(src_refs, dst_refs):
        dst[...] = src[...].astype(BF16)


def _ffn_kernel(n_split, n_cast, x_ref, ng_ref, wg_ref, wu_ref, wd_ref, *rest):
    cast_in, o_ref, cast_out = rest[:n_cast], rest[n_cast], rest[n_cast + 1:]
    for g0, g1 in _row_groups(x_ref.shape[0], n_split):
        o_ref[g0:g1, :] = _ffn(x_ref[g0:g1, :], ng_ref[...], wg_ref[...], wu_ref[...], wd_ref[...])
    _cast_rows(cast_in, cast_out)


def _ffn_call(x, ng, wg, wu, wd, tm, n_split, cast=()):
    m, d = x.shape
    n_steps = m // tm
    cast_specs = _cast_specs(cast, n_steps)
    out = pl.pallas_call(
        functools.partial(_ffn_kernel, n_split, len(cast)),
        grid=(n_steps,),
        in_specs=[pl.BlockSpec((tm, d), lambda i: (i, 0)),
                  _const_spec(ng.shape), _const_spec(wg.shape), _const_spec(wu.shape),
                  _const_spec(wd.shape)] + cast_specs,
        out_specs=[pl.BlockSpec((tm, d), lambda i: (i, 0))] + cast_specs,
        out_shape=[jax.ShapeDtypeStruct((m, d), F32)]
                  + [jax.ShapeDtypeStruct(a.shape, BF16) for a in cast],
        compiler_params=pltpu.CompilerParams(dimension_semantics=("arbitrary",),
                                             vmem_limit_bytes=VMEM_LIMIT),
        name="ffn",
    )(x, ng, wg, wu, wd, *cast)
    return out[0], out[1:]


def _memkv_kernel(n_cast, m_ref, g_ref, wk_ref, wv_ref, *rest):
    cast_in, (k_ref, v_ref, kb_ref, vb_ref), cast_out = rest[:n_cast], rest[n_cast:n_cast + 4], rest[n_cast + 4:]
    _cast_rows(cast_in, cast_out)
    nb, n_mem, n_head, dh = k_ref.shape
    mn = _rms(m_ref[...].reshape(nb * n_mem, m_ref.shape[-1]), g_ref[...]).astype(BF16)
    for w_ref, o_ref, ob_ref in ((wk_ref, k_ref, kb_ref), (wv_ref, v_ref, vb_ref)):
        y = _dot(mn, w_ref[...])
        ob_ref[...] = y.astype(BF16).reshape(ob_ref.shape)
        for h in range(n_head):
            o_ref[:, :, h, :] = y[:, h * dh:(h + 1) * dh].reshape(nb, n_mem, dh)


def _memkv_call(mem, g, wk, wv, nb, cast=()):
    bsz, n_mem, d = mem.shape
    dh = d // MEM_HEADS
    n_steps = bsz // nb
    row = pl.BlockSpec((nb, n_mem, d), lambda i: (i, 0, 0))
    kv = pl.BlockSpec((nb, n_mem, MEM_HEADS, dh), lambda i: (i, 0, 0, 0))
    cast_specs = _cast_specs(cast, n_steps)
    out = pl.pallas_call(
        functools.partial(_memkv_kernel, len(cast)),
        grid=(n_steps,),
        in_specs=[row, _const_spec(g.shape), _const_spec(wk.shape), _const_spec(wv.shape)] + cast_specs,
        out_specs=[kv, kv, row, row] + cast_specs,
        out_shape=[jax.ShapeDtypeStruct((bsz, n_mem, MEM_HEADS, dh), F32)] * 2
                  + [jax.ShapeDtypeStruct((bsz, n_mem, d), BF16)] * 2
                  + [jax.ShapeDtypeStruct(a.shape, BF16) for a in cast],
        compiler_params=pltpu.CompilerParams(dimension_semantics=("arbitrary",),
                                             vmem_limit_bytes=VMEM_LIMIT),
        name="memkv",
    )(mem, g, wk, wv, *cast)
    return out[:4], out[4:]


def _hgrn_tile(z, lbv, hn, masks, states, chunk, chunks_per_seq, sub, d_head):
    m = z.shape[0]
    n_c, per_sub = m // chunk, sub // chunk
    subs = [slice(s0, s0 + sub) for s0 in range(0, m, sub)]
    dh = HGRN_HEADS * d_head
    tri, causal = masks
    zq, zf, v, zg = (z[:, i * dh:(i + 1) * dh] for i in range(4))
    c0, c1 = 0.5 * (1.0 + lbv), 0.5 * (1.0 - lbv)
    ct = c1 * jnp.tanh(0.5 * zf)
    f = c0 + ct
    k = (1.0 - c0) - ct
    q = _silu(zq)
    logf = jnp.log(f)
    l_hi = logf.astype(BF16)
    l_lo = (logf - l_hi.astype(F32)).astype(BF16)
    gc = jnp.concatenate([_dot(tri, l_hi[rs]) + _dot(tri, l_lo[rs]) for rs in subs], axis=0)
    g_last = [gc[(c + 1) * chunk - 1:(c + 1) * chunk, :] for c in range(n_c)]
    g_end = jnp.concatenate([jnp.broadcast_to(g, (chunk, dh)) for g in g_last], axis=0)
    q_dec = (q * jnp.exp(gc)).astype(BF16)
    k_dec = (k * jnp.exp(-gc)).astype(BF16)
    k_end = (k * jnp.exp(g_end - gc)).astype(BF16)
    decay = [jnp.exp(g) for g in g_last]
    vb = v.astype(BF16)
    zero = jnp.zeros((chunk, d_head), BF16)
    out, cur = [], [list(s) for s in states]
    for h in range(HGRN_HEADS):
        hs = slice(h * d_head, (h + 1) * d_head)
        o_intra, kv = [], []
        for rs in subs:
            a = jnp.where(causal, _dot_nt(q_dec[rs, hs], k_dec[rs, hs]), 0.0).astype(BF16)
            o_intra.append(_dot(a, vb[rs, hs]))
            ke = k_end[rs, hs]
            k_blocks = jnp.concatenate(
                [jnp.concatenate([ke[j * chunk:(j + 1) * chunk] if i == j else zero
                                  for i in range(per_sub)], axis=1) for j in range(per_sub)], axis=0)
            kv.append(_dot_tn(vb[rs, hs], k_blocks))
        st_in = []
        for c in range(n_c):
            seq, j = c // chunks_per_seq, c % per_sub
            st_in.append(cur[seq][h])
            cur[seq][h] = st_in[c] * decay[c][:, hs] + kv[c // per_sub][:, j * d_head:(j + 1) * d_head]
        o = []
        for si, rs in enumerate(subs):
            stack = jnp.concatenate([s.astype(BF16) for s in st_in[si * per_sub:(si + 1) * per_sub]], axis=0)
            o_inter = _dot_nt(q_dec[rs, hs], stack)
            o.append(o_intra[si] + jnp.concatenate(
                [o_inter[j * chunk:(j + 1) * chunk, j * d_head:(j + 1) * d_head] for j in range(per_sub)],
                axis=0))
        o = jnp.concatenate(o, axis=0)
        on = o * lax.rsqrt(jnp.mean(o * o, axis=-1, keepdims=True) + EPS) * hn[:, hs]
        out.append((on * _silu(zg[:, hs])).astype(BF16))
    return out, cur


def _mix_kernel(nb, tt, chunk,
                x_ref, ng_ref, win_ref, cw_ref, cb_ref, lng_ref, lnb_ref, hlb_ref, hn_ref, wout_ref,
                sconv_ref, shgrn_ref,
                o_ref, oconv_ref, ohgrn_ref,
                zbuf, ubuf, ushift, wtap, sbuf, cat):
    t = pl.program_id(1)
    d_conv = cw_ref.shape[1]
    d_head = sbuf.shape[-1]
    m = nb * tt
    first = HALO - CONV_BUF

    @pl.when((pl.program_id(0) == 0) & (t == 0))
    def _spread_taps():
        wtap[...] = jnp.broadcast_to(cw_ref[...][:, None, :], wtap.shape)

    @pl.when(t == 0)
    def _load_state():
        ubuf[:, 0:first, :] = jnp.zeros((nb, first, d_conv), F32)
        ubuf[:, first:HALO, :] = sconv_ref[...]
        for b in range(nb):
            for h in range(HGRN_HEADS):
                sbuf[b, h] = shgrn_ref[b, h].T

    x = x_ref[...].reshape(m, x_ref.shape[-1])
    zbuf[...] = _dot(_rms(x, ng_ref[...]).astype(BF16), win_ref[...])

    for b in range(nb):
        r = slice(b * tt, (b + 1) * tt)
        ubuf[b, HALO:HALO + tt, :] = _gated(zbuf[r, 0:d_conv], zbuf[r, d_conv:2 * d_conv])
    cb = cb_ref[...]
    lng = lng_ref[...]
    lnb = lnb_ref[...]
    n_shift = ushift.shape[2]
    for r in range(1, SUBLANES):
        ushift[r - 1] = ubuf[:, r:r + n_shift, :]
    for b in range(nb):
        for r0 in range(0, tt, CONV_ROWS):
            acc = jnp.broadcast_to(cb, (CONV_ROWS // SUBLANES, SUBLANES, d_conv))
            for j in range(CONV_WIDTH):
                a8, r = divmod(first + j, SUBLANES)
                lo = r0 + a8 * SUBLANES
                src = ubuf[b, lo:lo + CONV_ROWS, :] if r == 0 else ushift[r - 1, b, lo:lo + CONV_ROWS, :]
                acc = acc + wtap[j] * src.reshape(acc.shape)
            acc = acc.reshape(CONV_ROWS, d_conv)
            mu = jnp.mean(acc, axis=-1, keepdims=True)
            dev = acc - mu
            var = jnp.mean(dev * dev, axis=-1, keepdims=True)
            yc = _silu(dev * lax.rsqrt(var + EPS) * lng + lnb)
            cat[b * tt + r0:b * tt + r0 + CONV_ROWS, 0:d_conv] = yc.astype(BF16)
    new_rows = ubuf[:, tt + first:tt + HALO, :]
    ubuf[:, first:HALO, :] = new_rows

    hlb = hlb_ref[...]
    e = jnp.exp(hlb - jnp.max(hlb, axis=0, keepdims=True))
    lbv = e[0:1, :] / jnp.sum(e, axis=0, keepdims=True)
    shift = chunk.bit_length() - 1
    assert chunk == 1 << shift
    sub = min(m, HGRN_ROWS)
    row_i = lax.broadcasted_iota(jnp.int32, (sub, sub), 0)
    col_i = lax.broadcasted_iota(jnp.int32, (sub, sub), 1)
    causal = (row_i >= col_i) & (lax.shift_right_logical(row_i, shift) == lax.shift_right_logical(col_i, shift))
    masks = (jnp.where(causal, 1.0, 0.0).astype(BF16), causal)
    heads, state = _hgrn_tile(zbuf[:, 2 * d_conv:], lbv, hn_ref[...], masks,
                              [[sbuf[b, h] for h in range(HGRN_HEADS)] for b in range(nb)],
                              chunk, tt // chunk, sub, d_head)
    for h in range(HGRN_HEADS):
        cat[:, d_conv + h * d_head:d_conv + (h + 1) * d_head] = heads[h]
        for b in range(nb):
            sbuf[b, h] = state[b][h]

    o_ref[...] = (x + _dot(cat[...], wout_ref[...])).reshape(o_ref.shape)

    @pl.when(t == pl.num_programs(1) - 1)
    def _store_state():
        oconv_ref[...] = ubuf[:, first:HALO, :]
        for b in range(nb):
            for h in range(HGRN_HEADS):
                ohgrn_ref[b, h] = sbuf[b, h].T


def _mix_call(x, ng, win, cw, cb, lng, lnb, hlb, hn, wout, sconv, shgrn, nb, tt):
    bsz, seq, d = x.shape
    d_conv = cw.shape[1]
    d_head = shgrn.shape[-1]
    chunk = min(CHUNK, seq)
    kern = functools.partial(_mix_kernel, nb, tt, chunk)
    xspec = pl.BlockSpec((nb, tt, d), lambda i, t: (i, t, 0))
    cspec = pl.BlockSpec((nb, CONV_BUF, d_conv), lambda i, t: (i, 0, 0))
    sspec = pl.BlockSpec((nb, HGRN_HEADS, d_head, d_head), lambda i, t: (i, 0, 0, 0))
    return pl.pallas_call(
        kern,
        grid=(bsz // nb, seq // tt),
        in_specs=[xspec] + [_const_spec(a.shape) for a in (ng, win, cw, cb, lng, lnb, hlb, hn, wout)]
                 + [cspec, sspec],
        out_specs=[xspec, cspec, sspec],
        out_shape=[jax.ShapeDtypeStruct(x.shape, F32),
                   jax.ShapeDtypeStruct(sconv.shape, F32),
                   jax.ShapeDtypeStruct(shgrn.shape, F32)],
        scratch_shapes=[pltpu.VMEM((nb * tt, win.shape[1]), F32),
                        pltpu.VMEM((nb, HALO + tt, d_conv), F32),
                        pltpu.VMEM((SUBLANES - 1, nb, HALO + tt - SUBLANES, d_conv), F32),
                        pltpu.VMEM((CONV_WIDTH, SUBLANES, d_conv), F32),
                        pltpu.VMEM((nb, HGRN_HEADS, d_head, d_head), F32),
                        pltpu.VMEM((nb * tt, d), BF16)],
        compiler_params=pltpu.CompilerParams(dimension_semantics=("arbitrary", "arbitrary"),
                                             vmem_limit_bytes=VMEM_LIMIT),
        name="mix",
    )(x, ng, win, cw, cb, lng, lnb, hlb, hn, wout, sconv, shgrn)


def _tail_kernel(nb, tt, n_split, kv_native,
                 x_ref, k_ref, v_ref, xg_ref, wq_ref, wo_ref, ng_ref, wg_ref, wu_ref, wd_ref, fg_ref,
                 o_ref, *scratch):
    d = x_ref.shape[-1]
    dh = d // MEM_HEADS
    scale = dh ** -0.5
    ffn_w, w_sem, scratch = scratch[:3], scratch[3], scratch[4:]
    first_step = (pl.program_id(0) == 0) & (pl.program_id(1) == 0)
    w_copies = [pltpu.make_async_copy(src, dst, w_sem.at[j])
                for j, (src, dst) in enumerate(zip((wg_ref, wu_ref, wd_ref), ffn_w))]

    @pl.when(first_step)
    def _start_weights():
        for cp in w_copies:
            cp.start()

    if kv_native:
        kvbuf, sem = scratch
        i, n = pl.program_id(0), pl.num_programs(0)

        def kv_copies(blk, slot):
            return [pltpu.make_async_copy(src.at[pl.ds(blk * nb, nb), :, h, :],
                                          kvbuf.at[slot, j, :, :, h * dh:(h + 1) * dh],
                                          sem.at[slot, j, h])
                    for j, src in enumerate((k_ref, v_ref)) for h in range(MEM_HEADS)]

        def start_all(copies):
            for c, cp in enumerate(copies):
                cp.start(priority=c % 2)

        @pl.when(i == 0)
        def _first_block():
            start_all(kv_copies(0, 0))

        @pl.when(i + 1 < n)
        def _next_block():
            start_all(kv_copies(i + 1, lax.rem(i + 1, 2)))

        slot = lax.rem(i, 2)
        for cp in kv_copies(i, slot):
            cp.wait()
        k_ref, v_ref = kvbuf.at[slot, 0], kvbuf.at[slot, 1]
    kb = [k_ref[b].astype(BF16) for b in range(nb)]
    vb = [v_ref[b].astype(BF16) for b in range(nb)]
    m = nb * tt
    x_all = x_ref[...].reshape(m, d)
    q = _dot(_rms(x_all, xg_ref[...]).astype(BF16), wq_ref[...])
    hsl = [slice(h * dh, (h + 1) * dh) for h in range(MEM_HEADS)]
    qb = (q * scale).astype(BF16)
    stacked = tt * MEM_HEADS <= STACK_ROWS
    scores = []
    for b in range(nb):
        qr = qb[b * tt:(b + 1) * tt]
        if stacked:
            zq = jnp.zeros((tt, dh), BF16)
            q_stack = jnp.concatenate(
                [jnp.concatenate([qr[:, hs] if j == h else zq for j in range(MEM_HEADS)], axis=1)
                 for h, hs in enumerate(hsl)], axis=0)
            scores.append(_dot_nt(q_stack, kb[b]))
        else:
            scores += [_dot_nt(qr[:, hs], kb[b][:, hs]) for hs in hsl]
    sc = jnp.concatenate(scores, axis=0)
    ex = jnp.exp(sc - jnp.max(sc, axis=-1, keepdims=True))
    p = (ex * (1.0 / jnp.sum(ex, axis=-1, keepdims=True))).astype(BF16)
    att = []
    for b in range(nb):
        pb = p[b * MEM_HEADS * tt:(b + 1) * MEM_HEADS * tt]
        if stacked:
            pv = _dot(pb, vb[b])
            heads = [pv[h * tt:(h + 1) * tt, hs] for h, hs in enumerate(hsl)]
        else:
            heads = [_dot(pb[h * tt:(h + 1) * tt], vb[b][:, hs]) for h, hs in enumerate(hsl)]
        att.append(jnp.concatenate(heads, axis=1).astype(BF16))
    xa = x_all + _dot(jnp.concatenate(att, axis=0), wo_ref[...])
    @pl.when(first_step)
    def _weights_ready():
        for cp in w_copies:
            cp.wait()

    wg, wu, wd = ffn_w
    for g0, g1 in _row_groups(m, n_split):
        if nb == 1:
            rows, blk = (0, slice(g0, g1)), (g1 - g0, d)
        else:
            assert g0 % tt == 0 and g1 % tt == 0
            rows, blk = (slice(g0 // tt, g1 // tt),), ((g1 - g0) // tt, tt, d)
        x = _ffn(xa[g0:g1], ng_ref[...], wg[...], wu[...], wd[...])
        o_ref[rows] = _rms(x, fg_ref[...]).reshape(blk)


def _tail_call(x, k, v, xg, wq, wo, ng, wg, wu, wd, fg, nb, tt, n_split):
    bsz, seq, d = x.shape
    kv_native = k.ndim == 4
    kern = functools.partial(_tail_kernel, nb, tt, n_split, kv_native)
    xspec = pl.BlockSpec((nb, tt, d), lambda i, t: (i, t, 0))
    if kv_native:
        assert seq == tt, "the key/value copies are pipelined over the batch axis of the grid"
        kvspec = pl.BlockSpec(memory_space=pl.ANY)
        scratch = [pltpu.VMEM((2, 2, nb, k.shape[1], d), F32), pltpu.SemaphoreType.DMA((2, 2, MEM_HEADS))]
    else:
        kvspec = pl.BlockSpec((nb,) + k.shape[1:], lambda i, t: (i, 0, 0))
        scratch = []
    late = (wg, wu, wd)
    scratch = [pltpu.VMEM(a.shape, a.dtype) for a in late] + [pltpu.SemaphoreType.DMA((len(late),))] + scratch
    hbm = pl.BlockSpec(memory_space=pl.ANY)
    return pl.pallas_call(
        kern,
        grid=(bsz // nb, seq // tt),
        in_specs=[xspec, kvspec, kvspec]
                 + [_const_spec(a.shape) for a in (xg, wq, wo, ng)] + [hbm] * len(late) + [_const_spec(fg.shape)],
        out_specs=xspec,
        out_shape=jax.ShapeDtypeStruct(x.shape, F32),
        scratch_shapes=scratch,
        compiler_params=pltpu.CompilerParams(dimension_semantics=("arbitrary", "arbitrary"),
                                             vmem_limit_bytes=VMEM_LIMIT),
        name="tail",
    )(x, k, v, xg, wq, wo, ng, wg, wu, wd, fg)


def _mix_tail(x, sconv, shgrn, mk, mv, w, fg, mix_tile, tail_tile):
    x, new_conv, new_hgrn = _mix_call(x, w["mix_norm"], w["w_in"], w["conv_dw"], w["conv_dw_b"],
                                      w["conv_ln_g"], w["conv_ln_b"], w["hgrn_lb"], w["hgrn_norm"],
                                      w["w_out"], sconv, shgrn, *mix_tile)
    y = _tail_call(x, mk, mv, w["xattn_norm"], w["xattn_wq"], w["xattn_wo"], w["ffn2_norm"],
                   w["ffn2_w_gate"], w["ffn2_w_up"], w["ffn2_w_down"], fg, *tail_tile)
    return y, new_conv, new_hgrn


def kernel(x_prompt, x_sample, mem_prompt, state_conv, state_hgrn, cache_mem_k, cache_mem_v, ffn1_norm, ffn1_w_gate, ffn1_w_up, ffn1_w_down, mix_norm, w_in, conv_dw, conv_dw_b, conv_ln_g, conv_ln_b, hgrn_lb, hgrn_norm, w_out, mem_norm, mem_wk, mem_wv, xattn_norm, xattn_wq, xattn_wo, ffn2_norm, ffn2_w_gate, ffn2_w_up, ffn2_w_down, final_norm):
    depth = ffn1_norm.shape[0]
    assert depth == 1, "the per-layer forget-gate bound is taken from row 0 of the cumulative softmax"
    bp, seq, d = x_prompt.shape
    bs, dec_seq, _ = x_sample.shape
    n_mem = mem_prompt.shape[1]
    d_hgrn = hgrn_lb.shape[1]
    d_conv = conv_dw.shape[2]
    d_head = d_hgrn // HGRN_HEADS

    def row(a):
        return a.reshape(1, -1)

    fg = row(final_norm)
    (mk, mv, mk_flat, mv_flat), ffn1_w = _memkv_call(
        mem_prompt, row(mem_norm[0]), mem_wk[0].astype(BF16), mem_wv[0].astype(BF16), 1,
        cast=(ffn1_w_gate[0], ffn1_w_up[0], ffn1_w_down[0]))

    ffn1 = (row(ffn1_norm[0]), *ffn1_w)
    later = {"w_in": w_in[0], "w_out": w_out[0], "xattn_wq": xattn_wq[0], "xattn_wo": xattn_wo[0],
             "ffn2_w_gate": ffn2_w_gate[0], "ffn2_w_up": ffn2_w_up[0], "ffn2_w_down": ffn2_w_down[0]}
    xp, later_bf16 = _ffn_call(x_prompt.reshape(bp * seq, d), *ffn1, 1024, 4, cast=tuple(later.values()))
    xs, _ = _ffn_call(x_sample.reshape(bs * dec_seq, d), *ffn1, 256, 1)
    w = dict(zip(later, later_bf16))
    w.update({
        "mix_norm": row(mix_norm[0]), "conv_dw": conv_dw[0], "conv_dw_b": row(conv_dw_b[0]),
        "conv_ln_g": row(conv_ln_g[0]), "conv_ln_b": row(conv_ln_b[0]),
        "hgrn_lb": hgrn_lb, "hgrn_norm": row(hgrn_norm[0]),
        "xattn_norm": row(xattn_norm[0]), "ffn2_norm": row(ffn2_norm[0]),
    })

    y_p, conv_p, hgrn_p = _mix_tail(
        xp.reshape(bp, seq, d), jnp.zeros((bp, CONV_BUF, d_conv), F32),
        jnp.zeros((bp, HGRN_HEADS, d_head, d_head), F32), mk_flat, mv_flat, w, fg,
        mix_tile=(1, 512), tail_tile=(1, 1024, 4))
    y_s, conv_s, hgrn_s = _mix_tail(
        xs.reshape(bs, dec_seq, d), state_conv[0], state_hgrn[0],
        cache_mem_k[0], cache_mem_v[0], w, fg,
        mix_tile=(8, dec_seq), tail_tile=(4, dec_seq, 1))

    return (y_p, y_s, conv_p[None], hgrn_p[None], mk[None], mv[None],
            conv_s[None], hgrn_s[None])
```

```python
import functools

import jax
import jax.numpy as jnp
from jax import lax
from jax.experimental import pallas as pl
from jax.experimental.pallas import tpu as pltpu

F32 = jnp.float32
BF16 = jnp.bfloat16

EPS = 1e-6
CHUNK = 64
CONV_WIDTH = 31
CONV_BUF = CONV_WIDTH - 1
HGRN_HEADS = 4
MEM_HEADS = 4
HALO = 32
CONV_ROWS = 32
SUBLANES = 8
BF16_ROWS = 16
HGRN_ROWS = 256
STACK_ROWS = 256
VMEM_LIMIT = 56 * 1024 * 1024


def _dot(a, b):
    return jnp.dot(a, b, preferred_element_type=F32)


def _dot_nt(a, b):
    return lax.dot_general(a, b, (((1,), (1,)), ((), ())), preferred_element_type=F32)


def _dot_tn(a, b):
    return lax.dot_general(a, b, (((0,), (0,)), ((), ())), preferred_element_type=F32)


def _rms(x, g):
    return x * lax.rsqrt(jnp.mean(x * x, axis=-1, keepdims=True) + EPS) * g


def _gated(a, x):
    ha = 0.5 * a
    return ha + ha * jnp.tanh(0.5 * x)


def _silu(x):
    hx = 0.5 * x
    return hx + hx * jnp.tanh(hx)


def _ffn(x, ng, wg, wu, wd):
    xn = _rms(x, ng).astype(BF16)
    h = (_silu(_dot(xn, wg)) * _dot(xn, wu)).astype(BF16)
    return x + 0.5 * _dot(h, wd)


def _const_spec(shape):
    nd = len(shape)
    return pl.BlockSpec(shape, lambda *_: (0,) * nd, pipeline_mode=pl.Buffered(1))


def _row_groups(m, n_split):
    mg = m // n_split
    return [(g * mg, (g + 1) * mg) for g in range(n_split)]


def _cast_specs(arrays, n_steps):
    specs = []
    for a in arrays:
        n_blk = max(n for n in range(1, n_steps + 1) if a.shape[0] % (n * BF16_ROWS) == 0)
        specs.append(pl.BlockSpec((a.shape[0] // n_blk, a.shape[1]),
                                  lambda i, n_blk=n_blk: (jnp.minimum(i, n_blk - 1), 0)))
    return specs


def _cast_rows(src_refs, dst_refs):
    for src, dst in zip(src_refs, dst_refs):
        dst[...] = src[...].astype(BF16)


def _ffn_kernel(plan, n_cast, ng_ref, wg_ref, wu_ref, wd_ref, *rest):
    n_x = len(plan)
    x_refs, cast_in = rest[:n_x], rest[n_x:n_x + n_cast]
    o_refs, cast_out = rest[n_x + n_cast:2 * n_x + n_cast], rest[2 * n_x + n_cast:]
    i = pl.program_id(0)
    for (start, n_blk, n_split), x_ref, o_ref in zip(plan, x_refs, o_refs):
        @pl.when((i >= start) & (i < start + n_blk))
        def _(x_ref=x_ref, o_ref=o_ref, n_split=n_split):
            for g0, g1 in _row_groups(x_ref.shape[0], n_split):
                o_ref[g0:g1, :] = _ffn(x_ref[g0:g1, :], ng_ref[...], wg_ref[...], wu_ref[...], wd_ref[...])
    _cast_rows(cast_in, cast_out)


def _ffn_call(xs, ng, wg, wu, wd, tiles, cast=()):
    d = xs[0].shape[1]
    plan, row_specs, start = [], [], 0
    for x, (tm, n_split) in zip(xs, tiles):
        n_blk = x.shape[0] // tm
        plan.append((start, n_blk, n_split))
        row_specs.append(pl.BlockSpec(
            (tm, d), lambda i, start=start, n_blk=n_blk: (jnp.clip(i - start, 0, n_blk - 1), 0)))
        start += n_blk
    cast_specs = _cast_specs(cast, start)
    out = pl.pallas_call(
        functools.partial(_ffn_kernel, tuple(plan), len(cast)),
        grid=(start,),
        in_specs=[_const_spec(ng.shape), _const_spec(wg.shape), _const_spec(wu.shape), _const_spec(wd.shape)]
                 + row_specs + cast_specs,
        out_specs=row_specs + cast_specs,
        out_shape=[jax.ShapeDtypeStruct(x.shape, F32) for x in xs]
                  + [jax.ShapeDtypeStruct(a.shape, BF16) for a in cast],
        compiler_params=pltpu.CompilerParams(dimension_semantics=("arbitrary",),
                                             vmem_limit_bytes=VMEM_LIMIT),
        name="ffn",
    )(ng, wg, wu, wd, *xs, *cast)
    return out[:len(xs)], out[len(xs):]


def _memkv_kernel(n_cast, m_ref, g_ref, wk_ref, wv_ref, *rest):
    cast_in, (k_ref, v_ref, kb_ref, vb_ref), cast_out = rest[:n_cast], rest[n_cast:n_cast + 4], rest[n_cast + 4:]
    _cast_rows(cast_in, cast_out)
    nb, n_mem, n_head, dh = k_ref.shape
    mn = _rms(m_ref[...].reshape(nb * n_mem, m_ref.shape[-1]), g_ref[...]).astype(BF16)
    for w_ref, o_ref, ob_ref in ((wk_ref, k_ref, kb_ref), (wv_ref, v_ref, vb_ref)):
        y = _dot(mn, w_ref[...])
        ob_ref[...] = y.astype(BF16).reshape(ob_ref.shape)
        for h in range(n_head):
            o_ref[:, :, h, :] = y[:, h * dh:(h + 1) * dh].reshape(nb, n_mem, dh)


def _memkv_call(mem, g, wk, wv, nb, cast=()):
    bsz, n_mem, d = mem.shape
    dh = d // MEM_HEADS
    n_steps = bsz // nb
    row = pl.BlockSpec((nb, n_mem, d), lambda i: (i, 0, 0))
    kv = pl.BlockSpec((nb, n_mem, MEM_HEADS, dh), lambda i: (i, 0, 0, 0))
    cast_specs = _cast_specs(cast, n_steps)
    out = pl.pallas_call(
        functools.partial(_memkv_kernel, len(cast)),
        grid=(n_steps,),
        in_specs=[row, _const_spec(g.shape), _const_spec(wk.shape), _const_spec(wv.shape)] + cast_specs,
        out_specs=[kv, kv, row, row] + cast_specs,
        out_shape=[jax.ShapeDtypeStruct((bsz, n_mem, MEM_HEADS, dh), F32)] * 2
                  + [jax.ShapeDtypeStruct((bsz, n_mem, d), BF16)] * 2
                  + [jax.ShapeDtypeStruct(a.shape, BF16) for a in cast],
        compiler_params=pltpu.CompilerParams(dimension_semantics=("arbitrary",),
                                             vmem_limit_bytes=VMEM_LIMIT),
        name="memkv",
    )(mem, g, wk, wv, *cast)
    return out[:4], out[4:]


def _hgrn_tile(z, lbv, hn, masks, states, chunk, chunks_per_seq, sub, d_head):
    m = z.shape[0]
    n_c, per_sub = m // chunk, sub // chunk
    subs = [slice(s0, s0 + sub) for s0 in range(0, m, sub)]
    dh = HGRN_HEADS * d_head
    tri, causal = masks
    zq, zf, v, zg = (z[:, i * dh:(i + 1) * dh] for i in range(4))
    c0, c1 = 0.5 * (1.0 + lbv), 0.5 * (1.0 - lbv)
    ct = c1 * jnp.tanh(0.5 * zf)
    f = c0 + ct
    k = (1.0 - c0) - ct
    q = _silu(zq)
    logf = jnp.log(f)
    l_hi = logf.astype(BF16)
    l_lo = (logf - l_hi.astype(F32)).astype(BF16)
    gc = jnp.concatenate([_dot(tri, l_hi[rs]) + _dot(tri, l_lo[rs]) for rs in subs], axis=0)
    g_last = [gc[(c + 1) * chunk - 1:(c + 1) * chunk, :] for c in range(n_c)]
    g_end = jnp.concatenate([jnp.broadcast_to(g, (chunk, dh)) for g in g_last], axis=0)
    q_dec = (q * jnp.exp(gc)).astype(BF16)
    k_dec = (k * jnp.exp(-gc)).astype(BF16)
    k_end = (k * jnp.exp(g_end - gc)).astype(BF16)
    decay = [jnp.exp(g) for g in g_last]
    vb = v.astype(BF16)
    zero = jnp.zeros((chunk, d_head), BF16)
    out, cur = [], [list(s) for s in states]
    for h in range(HGRN_HEADS):
        hs = slice(h * d_head, (h + 1) * d_head)
        o_intra, kv = [], []
        for rs in subs:
            a = jnp.where(causal, _dot_nt(q_dec[rs, hs], k_dec[rs, hs]), 0.0).astype(BF16)
            o_intra.append(_dot(a, vb[rs, hs]))
            ke = k_end[rs, hs]
            k_blocks = jnp.concatenate(
                [jnp.concatenate([ke[j * chunk:(j + 1) * chunk] if i == j else zero
                                  for i in range(per_sub)], axis=1) for j in range(per_sub)], axis=0)
            kv.append(_dot_tn(vb[rs, hs], k_blocks))
        st_in = []
        for c in range(n_c):
            seq, j = c // chunks_per_seq, c % per_sub
            st_in.append(cur[seq][h])
            cur[seq][h] = st_in[c] * decay[c][:, hs] + kv[c // per_sub][:, j * d_head:(j + 1) * d_head]
        o = []
        for si, rs in enumerate(subs):
            stack = jnp.concatenate([s.astype(BF16) for s in st_in[si * per_sub:(si + 1) * per_sub]], axis=0)
            o_inter = _dot_nt(q_dec[rs, hs], stack)
            o.append(o_intra[si] + jnp.concatenate(
                [o_inter[j * chunk:(j + 1) * chunk, j * d_head:(j + 1) * d_head] for j in range(per_sub)],
                axis=0))
        o = jnp.concatenate(o, axis=0)
        on = o * lax.rsqrt(jnp.mean(o * o, axis=-1, keepdims=True) + EPS) * hn[:, hs]
        out.append((on * _silu(zg[:, hs])).astype(BF16))
    return out, cur


def _mix_kernel(nb, tt, chunk,
                x_ref, ng_ref, win_ref, cw_ref, cb_ref, lng_ref, lnb_ref, hlb_ref, hn_ref, wout_ref,
                sconv_ref, shgrn_ref,
                o_ref, oconv_ref, ohgrn_ref,
                zbuf, ubuf, ushift, wtap, sbuf, cat):
    t = pl.program_id(1)
    d_conv = cw_ref.shape[1]
    d_head = sbuf.shape[-1]
    m = nb * tt
    first = HALO - CONV_BUF

    @pl.when((pl.program_id(0) == 0) & (t == 0))
    def _spread_taps():
        wtap[...] = jnp.broadcast_to(cw_ref[...][:, None, :], wtap.shape)

    @pl.when(t == 0)
    def _load_state():
        ubuf[:, 0:first, :] = jnp.zeros((nb, first, d_conv), F32)
        ubuf[:, first:HALO, :] = sconv_ref[...]
        for b in range(nb):
            for h in range(HGRN_HEADS):
                sbuf[b, h] = shgrn_ref[b, h].T

    x = x_ref[...].reshape(m, x_ref.shape[-1])
    zbuf[...] = _dot(_rms(x, ng_ref[...]).astype(BF16), win_ref[...])

    for b in range(nb):
        r = slice(b * tt, (b + 1) * tt)
        ubuf[b, HALO:HALO + tt, :] = _gated(zbuf[r, 0:d_conv], zbuf[r, d_conv:2 * d_conv])
    cb = cb_ref[...]
    lng = lng_ref[...]
    lnb = lnb_ref[...]
    n_shift = ushift.shape[2]
    for r in range(1, SUBLANES):
        ushift[r - 1] = ubuf[:, r:r + n_shift, :]
    for b in range(nb):
        for r0 in range(0, tt, CONV_ROWS):
            acc = jnp.broadcast_to(cb, (CONV_ROWS // SUBLANES, SUBLANES, d_conv))
            for j in range(CONV_WIDTH):
                a8, r = divmod(first + j, SUBLANES)
                lo = r0 + a8 * SUBLANES
                src = ubuf[b, lo:lo + CONV_ROWS, :] if r == 0 else ushift[r - 1, b, lo:lo + CONV_ROWS, :]
                acc = acc + wtap[j] * src.reshape(acc.shape)
            acc = acc.reshape(CONV_ROWS, d_conv)
            mu = jnp.mean(acc, axis=-1, keepdims=True)
            dev = acc - mu
            var = jnp.mean(dev * dev, axis=-1, keepdims=True)
            yc = _silu(dev * lax.rsqrt(var + EPS) * lng + lnb)
            cat[b * tt + r0:b * tt + r0 + CONV_ROWS, 0:d_conv] = yc.astype(BF16)
    new_rows = ubuf[:, tt + first:tt + HALO, :]
    ubuf[:, first:HALO, :] = new_rows

    hlb = hlb_ref[...]
    e = jnp.exp(hlb - jnp.max(hlb, axis=0, keepdims=True))
    lbv = e[0:1, :] / jnp.sum(e, axis=0, keepdims=True)
    shift = chunk.bit_length() - 1
    assert chunk == 1 << shift
    sub = min(m, HGRN_ROWS)
    row_i = lax.broadcasted_iota(jnp.int32, (sub, sub), 0)
    col_i = lax.broadcasted_iota(jnp.int32, (sub, sub), 1)
    causal = (row_i >= col_i) & (lax.shift_right_logical(row_i, shift) == lax.shift_right_logical(col_i, shift))
    masks = (jnp.where(causal, 1.0, 0.0).astype(BF16), causal)
    heads, state = _hgrn_tile(zbuf[:, 2 * d_conv:], lbv, hn_ref[...], masks,
                              [[sbuf[b, h] for h in range(HGRN_HEADS)] for b in range(nb)],
                              chunk, tt // chunk, sub, d_head)
    for h in range(HGRN_HEADS):
        cat[:, d_conv + h * d_head:d_conv + (h + 1) * d_head] = heads[h]
        for b in range(nb):
            sbuf[b, h] = state[b][h]

    o_ref[...] = (x + _dot(cat[...], wout_ref[...])).reshape(o_ref.shape)

    @pl.when(t == pl.num_programs(1) - 1)
    def _store_state():
        oconv_ref[...] = ubuf[:, first:HALO, :]
        for b in range(nb):
            for h in range(HGRN_HEADS):
                ohgrn_ref[b, h] = sbuf[b, h].T


def _mix_call(x, ng, win, cw, cb, lng, lnb, hlb, hn, wout, sconv, shgrn, nb, tt):
    bsz, seq, d = x.shape
    d_conv = cw.shape[1]
    d_head = shgrn.shape[-1]
    chunk = min(CHUNK, seq)
    kern = functools.partial(_mix_kernel, nb, tt, chunk)
    xspec = pl.BlockSpec((nb, tt, d), lambda i, t: (i, t, 0))
    cspec = pl.BlockSpec((nb, CONV_BUF, d_conv), lambda i, t: (i, 0, 0))
    sspec = pl.BlockSpec((nb, HGRN_HEADS, d_head, d_head), lambda i, t: (i, 0, 0, 0))
    return pl.pallas_call(
        kern,
        grid=(bsz // nb, seq // tt),
        in_specs=[xspec] + [_const_spec(a.shape) for a in (ng, win, cw, cb, lng, lnb, hlb, hn, wout)]
                 + [cspec, sspec],
        out_specs=[xspec, cspec, sspec],
        out_shape=[jax.ShapeDtypeStruct(x.shape, F32),
                   jax.ShapeDtypeStruct(sconv.shape, F32),
                   jax.ShapeDtypeStruct(shgrn.shape, F32)],
        scratch_shapes=[pltpu.VMEM((nb * tt, win.shape[1]), F32),
                        pltpu.VMEM((nb, HALO + tt, d_conv), F32),
                        pltpu.VMEM((SUBLANES - 1, nb, HALO + tt - SUBLANES, d_conv), F32),
                        pltpu.VMEM((CONV_WIDTH, SUBLANES, d_conv), F32),
                        pltpu.VMEM((nb, HGRN_HEADS, d_head, d_head), F32),
                        pltpu.VMEM((nb * tt, d), BF16)],
        compiler_params=pltpu.CompilerParams(dimension_semantics=("arbitrary", "arbitrary"),
                                             vmem_limit_bytes=VMEM_LIMIT),
        name="mix",
    )(x, ng, win, cw, cb, lng, lnb, hlb, hn, wout, sconv, shgrn)


def _tail_kernel(nb, tt, n_split, kv_native,
                 x_ref, k_ref, v_ref, xg_ref, wq_ref, wo_ref, ng_ref, wg_ref, wu_ref, wd_ref, fg_ref,
                 o_ref, *scratch):
    d = x_ref.shape[-1]
    dh = d // MEM_HEADS
    scale = dh ** -0.5
    ffn_w, w_sem, scratch = scratch[:3], scratch[3], scratch[4:]
    first_step = (pl.program_id(0) == 0) & (pl.program_id(1) == 0)
    w_copies = [pltpu.make_async_copy(src, dst, w_sem.at[j])
                for j, (src, dst) in enumerate(zip((wg_ref, wu_ref, wd_ref), ffn_w))]

    @pl.when(first_step)
    def _start_weights():
        for cp in w_copies:
            cp.start()

    if kv_native:
        kvbuf, sem = scratch
        i, n = pl.program_id(0), pl.num_programs(0)

        def kv_copies(blk, slot):
            return [pltpu.make_async_copy(src.at[pl.ds(blk * nb, nb), :, h, :],
                                          kvbuf.at[slot, j, :, :, h * dh:(h + 1) * dh],
                                          sem.at[slot, j, h])
                    for j, src in enumerate((k_ref, v_ref)) for h in range(MEM_HEADS)]

        def start_all(copies):
            for c, cp in enumerate(copies):
                cp.start(priority=c % 2)

        @pl.when(i == 0)
        def _first_block():
            start_all(kv_copies(0, 0))

        @pl.when(i + 1 < n)
        def _next_block():
            start_all(kv_copies(i + 1, lax.rem(i + 1, 2)))

        slot = lax.rem(i, 2)
        for cp in kv_copies(i, slot):
            cp.wait()
        k_ref, v_ref = kvbuf.at[slot, 0], kvbuf.at[slot, 1]
    kb = [k_ref[b].astype(BF16) for b in range(nb)]
    vb = [v_ref[b].astype(BF16) for b in range(nb)]
    m = nb * tt
    x_all = x_ref[...].reshape(m, d)
    q = _dot(_rms(x_all, xg_ref[...]).astype(BF16), wq_ref[...])
    hsl = [slice(h * dh, (h + 1) * dh) for h in range(MEM_HEADS)]
    qb = (q * scale).astype(BF16)
    stacked = tt * MEM_HEADS <= STACK_ROWS
    scores = []
    for b in range(nb):
        qr = qb[b * tt:(b + 1) * tt]
        if stacked:
            zq = jnp.zeros((tt, dh), BF16)
            q_stack = jnp.concatenate(
                [jnp.concatenate([qr[:, hs] if j == h else zq for j in range(MEM_HEADS)], axis=1)
                 for h, hs in enumerate(hsl)], axis=0)
            scores.append(_dot_nt(q_stack, kb[b]))
        else:
            scores += [_dot_nt(qr[:, hs], kb[b][:, hs]) for hs in hsl]
    sc = jnp.concatenate(scores, axis=0)
    ex = jnp.exp(sc - jnp.max(sc, axis=-1, keepdims=True))
    p = (ex * (1.0 / jnp.sum(ex, axis=-1, keepdims=True))).astype(BF16)
    att = []
    for b in range(nb):
        pb = p[b * MEM_HEADS * tt:(b + 1) * MEM_HEADS * tt]
        if stacked:
            pv = _dot(pb, vb[b])
            heads = [pv[h * tt:(h + 1) * tt, hs] for h, hs in enumerate(hsl)]
        else:
            heads = [_dot(pb[h * tt:(h + 1) * tt], vb[b][:, hs]) for h, hs in enumerate(hsl)]
        att.append(jnp.concatenate(heads, axis=1).astype(BF16))
    xa = x_all + _dot(jnp.concatenate(att, axis=0), wo_ref[...])
    @pl.when(first_step)
    def _weights_ready():
        for cp in w_copies:
            cp.wait()

    wg, wu, wd = ffn_w
    for g0, g1 in _row_groups(m, n_split):
        if nb == 1:
            rows, blk = (0, slice(g0, g1)), (g1 - g0, d)
        else:
            assert g0 % tt == 0 and g1 % tt == 0
            rows, blk = (slice(g0 // tt, g1 // tt),), ((g1 - g0) // tt, tt, d)
        x = _ffn(xa[g0:g1], ng_ref[...], wg[...], wu[...], wd[...])
        o_ref[rows] = _rms(x, fg_ref[...]).reshape(blk)


def _tail_call(x, k, v, xg, wq, wo, ng, wg, wu, wd, fg, nb, tt, n_split):
    bsz, seq, d = x.shape
    kv_native = k.ndim == 4
    kern = functools.partial(_tail_kernel, nb, tt, n_split, kv_native)
    xspec = pl.BlockSpec((nb, tt, d), lambda i, t: (i, t, 0))
    if kv_native:
        assert seq == tt, "the key/value copies are pipelined over the batch axis of the grid"
        kvspec = pl.BlockSpec(memory_space=pl.ANY)
        scratch = [pltpu.VMEM((2, 2, nb, k.shape[1], d), F32), pltpu.SemaphoreType.DMA((2, 2, MEM_HEADS))]
    else:
        kvspec = pl.BlockSpec((nb,) + k.shape[1:], lambda i, t: (i, 0, 0))
        scratch = []
    late = (wg, wu, wd)
    scratch = [pltpu.VMEM(a.shape, a.dtype) for a in late] + [pltpu.SemaphoreType.DMA((len(late),))] + scratch
    hbm = pl.BlockSpec(memory_space=pl.ANY)
    return pl.pallas_call(
        kern,
        grid=(bsz // nb, seq // tt),
        in_specs=[xspec, kvspec, kvspec]
                 + [_const_spec(a.shape) for a in (xg, wq, wo, ng)] + [hbm] * len(late) + [_const_spec(fg.shape)],
        out_specs=xspec,
        out_shape=jax.ShapeDtypeStruct(x.shape, F32),
        scratch_shapes=scratch,
        compiler_params=pltpu.CompilerParams(dimension_semantics=("arbitrary", "arbitrary"),
                                             vmem_limit_bytes=VMEM_LIMIT),
        name="tail",
    )(x, k, v, xg, wq, wo, ng, wg, wu, wd, fg)


def _mix_tail(x, sconv, shgrn, mk, mv, w, fg, mix_tile, tail_tile):
    x, new_conv, new_hgrn = _mix_call(x, w["mix_norm"], w["w_in"], w["conv_dw"], w["conv_dw_b"],
                                      w["conv_ln_g"], w["conv_ln_b"], w["hgrn_lb"], w["hgrn_norm"],
                                      w["w_out"], sconv, shgrn, *mix_tile)
    y = _tail_call(x, mk, mv, w["xattn_norm"], w["xattn_wq"], w["xattn_wo"], w["ffn2_norm"],
                   w["ffn2_w_gate"], w["ffn2_w_up"], w["ffn2_w_down"], fg, *tail_tile)
    return y, new_conv, new_hgrn


def kernel(x_prompt, x_sample, mem_prompt, state_conv, state_hgrn, cache_mem_k, cache_mem_v, ffn1_norm, ffn1_w_gate, ffn1_w_up, ffn1_w_down, mix_norm, w_in, conv_dw, conv_dw_b, conv_ln_g, conv_ln_b, hgrn_lb, hgrn_norm, w_out, mem_norm, mem_wk, mem_wv, xattn_norm, xattn_wq, xattn_wo, ffn2_norm, ffn2_w_gate, ffn2_w_up, ffn2_w_down, final_norm):
    depth = ffn1_norm.shape[0]
    assert depth == 1, "the per-layer forget-gate bound is taken from row 0 of the cumulative softmax"
    bp, seq, d = x_prompt.shape
    bs, dec_seq, _ = x_sample.shape
    n_mem = mem_prompt.shape[1]
    d_hgrn = hgrn_lb.shape[1]
    d_conv = conv_dw.shape[2]
    d_head = d_hgrn // HGRN_HEADS

    def row(a):
        return a.reshape(1, -1)

    fg = row(final_norm)
    (mk, mv, mk_flat, mv_flat), ffn1_w = _memkv_call(
        mem_prompt, row(mem_norm[0]), mem_wk[0].astype(BF16), mem_wv[0].astype(BF16), 1,
        cast=(ffn1_w_gate[0], ffn1_w_up[0], ffn1_w_down[0]))

    later = {"w_in": w_in[0], "w_out": w_out[0], "xattn_wq": xattn_wq[0], "xattn_wo": xattn_wo[0],
             "ffn2_w_gate": ffn2_w_gate[0], "ffn2_w_up": ffn2_w_up[0], "ffn2_w_down": ffn2_w_down[0]}
    (xp, xs), later_bf16 = _ffn_call(
        (x_prompt.reshape(bp * seq, d), x_sample.reshape(bs * dec_seq, d)), row(ffn1_norm[0]), *ffn1_w,
        tiles=((1024, 4), (512, 2)), cast=tuple(later.values()))
    w = dict(zip(later, later_bf16))
    w.update({
        "mix_norm": row(mix_norm[0]), "conv_dw": conv_dw[0], "conv_dw_b": row(conv_dw_b[0]),
        "conv_ln_g": row(conv_ln_g[0]), "conv_ln_b": row(conv_ln_b[0]),
        "hgrn_lb": hgrn_lb, "hgrn_norm": row(hgrn_norm[0]),
        "xattn_norm": row(xattn_norm[0]), "ffn2_norm": row(ffn2_norm[0]),
    })

    y_p, conv_p, hgrn_p = _mix_tail(
        xp.reshape(bp, seq, d), jnp.zeros((bp, CONV_BUF, d_conv), F32),
        jnp.zeros((bp, HGRN_HEADS, d_head, d_head), F32), mk_flat, mv_flat, w, fg,
        mix_tile=(1, 512), tail_tile=(1, 1024, 4))
    y_s, conv_s, hgrn_s = _mix_tail(
        xs.reshape(bs, dec_seq, d), state_conv[0], state_hgrn[0],
        cache_mem_k[0], cache_mem_v[0], w, fg,
        mix_tile=(8, dec_seq), tail_tile=(4, dec_seq, 1))

    return (y_p, y_s, conv_p[None], hgrn_p[None], mk[None], mv[None],
            conv_s[None], hgrn_s[None])
```

```python
import functools

import jax
import jax.numpy as jnp
from jax import lax
from jax.experimental import pallas as pl
from jax.experimental.pallas import tpu as pltpu

F32 = jnp.float32
BF16 = jnp.bfloat16

EPS = 1e-6
CHUNK = 64
CONV_WIDTH = 31
CONV_BUF = CONV_WIDTH - 1
HGRN_HEADS = 4
MEM_HEADS = 4
HALO = 32
CONV_ROWS = 32
SUBLANES = 8
BF16_ROWS = 16
W_CHUNKS = 16
HGRN_ROWS = 256
STACK_ROWS = 256
VMEM_LIMIT = 56 * 1024 * 1024


def _dot(a, b):
    return jnp.dot(a, b, preferred_element_type=F32)


def _dot_nt(a, b):
    return lax.dot_general(a, b, (((1,), (1,)), ((), ())), preferred_element_type=F32)


def _dot_tn(a, b):
    return lax.dot_general(a, b, (((0,), (0,)), ((), ())), preferred_element_type=F32)


def _rms(x, g):
    return x * lax.rsqrt(jnp.mean(x * x, axis=-1, keepdims=True) + EPS) * g


def _gated(a, x):
    ha = 0.5 * a
    return ha + ha * jnp.tanh(0.5 * x)


def _silu(x):
    hx = 0.5 * x
    return hx + hx * jnp.tanh(hx)


def _ffn(x, ng, wg, wu, wd):
    xn = _rms(x, ng).astype(BF16)
    h = (_silu(_dot(xn, wg)) * _dot(xn, wu)).astype(BF16)
    return x + 0.5 * _dot(h, wd)


def _const_spec(shape):
    nd = len(shape)
    return pl.BlockSpec(shape, lambda *_: (0,) * nd, pipeline_mode=pl.Buffered(1))


def _row_groups(m, n_split):
    mg = m // n_split
    return [(g * mg, (g + 1) * mg) for g in range(n_split)]


def _cast_specs(arrays, n_steps):
    specs = []
    for a in arrays:
        n_blk = max(n for n in range(1, n_steps + 1) if a.shape[0] % (n * BF16_ROWS) == 0)
        specs.append(pl.BlockSpec((a.shape[0] // n_blk, a.shape[1]),
                                  lambda i, n_blk=n_blk: (jnp.minimum(i, n_blk - 1), 0)))
    return specs


def _cast_rows(src_refs, dst_refs):
    for src, dst in zip(src_refs, dst_refs):
        dst[...] = src[...].astype(BF16)


def _ffn_kernel(plan, n_cast, ng_ref, wg_hbm, wu_hbm, wd_hbm, *rest):
    n_x = len(plan)
    x_refs, cast_in = rest[:n_x], rest[n_x:n_x + n_cast]
    o_refs, cast_out = rest[n_x + n_cast:2 * n_x + n_cast], rest[2 * n_x + n_cast:2 * n_x + 2 * n_cast]
    wg_ref, wu_ref, wd_ref, stage_in, stage_dn, sem = rest[2 * n_x + 2 * n_cast:]
    i = pl.program_id(0)

    @pl.when(i == 0)
    def _load_weights():
        jobs = [(src, dst, stage, r0)
                for src, dst, stage in ((wg_hbm, wg_ref, stage_in), (wu_hbm, wu_ref, stage_in),
                                        (wd_hbm, wd_ref, stage_dn))
                for r0 in range(0, src.shape[0], stage.shape[1])]

        def chunk_copy(k):
            src, _, stage, r0 = jobs[k]
            return pltpu.make_async_copy(src.at[pl.ds(r0, stage.shape[1]), :], stage.at[k % 2], sem.at[k % 2])

        chunk_copy(0).start()
        for k, (_, dst, stage, r0) in enumerate(jobs):
            if k + 1 < len(jobs):
                chunk_copy(k + 1).start()
            chunk_copy(k).wait()
            dst[r0:r0 + stage.shape[1], :] = stage[k % 2].astype(BF16)

    for (start, n_blk, n_split), x_ref, o_ref in zip(plan, x_refs, o_refs):
        @pl.when((i >= start) & (i < start + n_blk))
        def _(x_ref=x_ref, o_ref=o_ref, n_split=n_split):
            for g0, g1 in _row_groups(x_ref.shape[0], n_split):
                o_ref[g0:g1, :] = _ffn(x_ref[g0:g1, :], ng_ref[...], wg_ref[...], wu_ref[...], wd_ref[...])
    _cast_rows(cast_in, cast_out)


def _ffn_call(xs, ng, wg, wu, wd, tiles, cast=()):
    d = xs[0].shape[1]
    plan, row_specs, start = [], [], 0
    for x, (tm, n_split) in zip(xs, tiles):
        n_blk = x.shape[0] // tm
        plan.append((start, n_blk, n_split))
        row_specs.append(pl.BlockSpec(
            (tm, d), lambda i, start=start, n_blk=n_blk: (jnp.clip(i - start, 0, n_blk - 1), 0)))
        start += n_blk
    cast_specs = _cast_specs(cast, start)
    hbm = pl.BlockSpec(memory_space=pl.ANY)
    out = pl.pallas_call(
        functools.partial(_ffn_kernel, tuple(plan), len(cast)),
        grid=(start,),
        in_specs=[_const_spec(ng.shape), hbm, hbm, hbm] + row_specs + cast_specs,
        out_specs=row_specs + cast_specs,
        out_shape=[jax.ShapeDtypeStruct(x.shape, F32) for x in xs]
                  + [jax.ShapeDtypeStruct(a.shape, BF16) for a in cast],
        scratch_shapes=[pltpu.VMEM(a.shape, BF16) for a in (wg, wu, wd)]
                       + [pltpu.VMEM((2, wg.shape[0] // W_CHUNKS, wg.shape[1]), F32),
                          pltpu.VMEM((2, wd.shape[0] // W_CHUNKS, wd.shape[1]), F32),
                          pltpu.SemaphoreType.DMA((2,))],
        compiler_params=pltpu.CompilerParams(dimension_semantics=("arbitrary",),
                                             vmem_limit_bytes=VMEM_LIMIT),
        name="ffn",
    )(ng, wg, wu, wd, *xs, *cast)
    return out[:len(xs)], out[len(xs):]


def _memkv_kernel(n_cast, m_ref, g_ref, wk_ref, wv_ref, *rest):
    cast_in, (k_ref, v_ref, kb_ref, vb_ref) = rest[:n_cast], rest[n_cast:n_cast + 4]
    cast_out, (wkb, wvb) = rest[n_cast + 4:2 * n_cast + 4], rest[2 * n_cast + 4:]
    _cast_rows(cast_in, cast_out)

    @pl.when(pl.program_id(0) == 0)
    def _round_weights():
        wkb[...] = wk_ref[...].astype(BF16)
        wvb[...] = wv_ref[...].astype(BF16)

    nb, n_mem, n_head, dh = k_ref.shape
    mn = _rms(m_ref[...].reshape(nb * n_mem, m_ref.shape[-1]), g_ref[...]).astype(BF16)
    for w_ref, o_ref, ob_ref in ((wkb, k_ref, kb_ref), (wvb, v_ref, vb_ref)):
        y = _dot(mn, w_ref[...])
        ob_ref[...] = y.astype(BF16).reshape(ob_ref.shape)
        for h in range(n_head):
            o_ref[:, :, h, :] = y[:, h * dh:(h + 1) * dh].reshape(nb, n_mem, dh)


def _memkv_call(mem, g, wk, wv, nb, cast=()):
    bsz, n_mem, d = mem.shape
    dh = d // MEM_HEADS
    n_steps = bsz // nb
    row = pl.BlockSpec((nb, n_mem, d), lambda i: (i, 0, 0))
    kv = pl.BlockSpec((nb, n_mem, MEM_HEADS, dh), lambda i: (i, 0, 0, 0))
    cast_specs = _cast_specs(cast, n_steps)
    out = pl.pallas_call(
        functools.partial(_memkv_kernel, len(cast)),
        grid=(n_steps,),
        in_specs=[row, _const_spec(g.shape), _const_spec(wk.shape), _const_spec(wv.shape)] + cast_specs,
        out_specs=[kv, kv, row, row] + cast_specs,
        out_shape=[jax.ShapeDtypeStruct((bsz, n_mem, MEM_HEADS, dh), F32)] * 2
                  + [jax.ShapeDtypeStruct((bsz, n_mem, d), BF16)] * 2
                  + [jax.ShapeDtypeStruct(a.shape, BF16) for a in cast],
        scratch_shapes=[pltpu.VMEM(wk.shape, BF16), pltpu.VMEM(wv.shape, BF16)],
        compiler_params=pltpu.CompilerParams(dimension_semantics=("arbitrary",),
                                             vmem_limit_bytes=VMEM_LIMIT),
        name="memkv",
    )(mem, g, wk, wv, *cast)
    return out[:4], out[4:]


def _hgrn_tile(z, lbv, hn, masks, states, chunk, chunks_per_seq, sub, d_head):
    m = z.shape[0]
    n_c, per_sub = m // chunk, sub // chunk
    subs = [slice(s0, s0 + sub) for s0 in range(0, m, sub)]
    dh = HGRN_HEADS * d_head
    tri, causal = masks
    zq, zf, v, zg = (z[:, i * dh:(i + 1) * dh] for i in range(4))
    c0, c1 = 0.5 * (1.0 + lbv), 0.5 * (1.0 - lbv)
    ct = c1 * jnp.tanh(0.5 * zf)
    f = c0 + ct
    k = (1.0 - c0) - ct
    q = _silu(zq)
    logf = jnp.log(f)
    l_hi = logf.astype(BF16)
    l_lo = (logf - l_hi.astype(F32)).astype(BF16)
    gc = jnp.concatenate([_dot(tri, l_hi[rs]) + _dot(tri, l_lo[rs]) for rs in subs], axis=0)
    g_last = [gc[(c + 1) * chunk - 1:(c + 1) * chunk, :] for c in range(n_c)]
    decay = [jnp.exp(g) for g in g_last]
    q_dec = (q * jnp.exp(gc)).astype(BF16)
    k_dec_f = k * jnp.exp(-gc)
    k_dec = k_dec_f.astype(BF16)
    k_end = (k_dec_f * jnp.concatenate([jnp.broadcast_to(dc, (chunk, dh)) for dc in decay], axis=0)).astype(BF16)
    vb = v.astype(BF16)
    zero = jnp.zeros((chunk, d_head), BF16)
    out, cur = [], [list(s) for s in states]
    for h in range(HGRN_HEADS):
        hs = slice(h * d_head, (h + 1) * d_head)
        o_intra, kv = [], []
        for rs in subs:
            a = jnp.where(causal, _dot_nt(q_dec[rs, hs], k_dec[rs, hs]), 0.0).astype(BF16)
            o_intra.append(_dot(a, vb[rs, hs]))
            ke = k_end[rs, hs]
            k_blocks = jnp.concatenate(
                [jnp.concatenate([ke[j * chunk:(j + 1) * chunk] if i == j else zero
                                  for i in range(per_sub)], axis=1) for j in range(per_sub)], axis=0)
            kv.append(_dot_tn(vb[rs, hs], k_blocks))
        st_in = []
        for c in range(n_c):
            seq, j = c // chunks_per_seq, c % per_sub
            st_in.append(cur[seq][h])
            cur[seq][h] = st_in[c] * decay[c][:, hs] + kv[c // per_sub][:, j * d_head:(j + 1) * d_head]
        o = []
        for si, rs in enumerate(subs):
            stack = jnp.concatenate([s.astype(BF16) for s in st_in[si * per_sub:(si + 1) * per_sub]], axis=0)
            o_inter = _dot_nt(q_dec[rs, hs], stack)
            o.append(o_intra[si] + jnp.concatenate(
                [o_inter[j * chunk:(j + 1) * chunk, j * d_head:(j + 1) * d_head] for j in range(per_sub)],
                axis=0))
        o = jnp.concatenate(o, axis=0)
        on = o * lax.rsqrt(jnp.mean(o * o, axis=-1, keepdims=True) + EPS) * hn[:, hs]
        out.append((on * _silu(zg[:, hs])).astype(BF16))
    return out, cur


def _mix_kernel(nb, tt, chunk,
                x_ref, ng_ref, win_ref, cw_ref, cb_ref, lng_ref, lnb_ref, hlb_ref, hn_ref, wout_ref,
                sconv_ref, shgrn_ref,
                o_ref, oconv_ref, ohgrn_ref,
                zbuf, ubuf, ushift, wtap, sbuf, cat):
    t = pl.program_id(1)
    d_conv = cw_ref.shape[1]
    d_head = sbuf.shape[-1]
    m = nb * tt
    first = HALO - CONV_BUF

    @pl.when((pl.program_id(0) == 0) & (t == 0))
    def _spread_taps():
        wtap[...] = jnp.broadcast_to(cw_ref[...][:, None, :], wtap.shape)

    @pl.when(t == 0)
    def _load_state():
        ubuf[:, 0:first, :] = jnp.zeros((nb, first, d_conv), F32)
        ubuf[:, first:HALO, :] = sconv_ref[...]
        for b in range(nb):
            for h in range(HGRN_HEADS):
                sbuf[b, h] = shgrn_ref[b, h].T

    x = x_ref[...].reshape(m, x_ref.shape[-1])
    zbuf[...] = _dot(_rms(x, ng_ref[...]).astype(BF16), win_ref[...])

    for b in range(nb):
        r = slice(b * tt, (b + 1) * tt)
        ubuf[b, HALO:HALO + tt, :] = _gated(zbuf[r, 0:d_conv], zbuf[r, d_conv:2 * d_conv])
    cb = cb_ref[...]
    lng = lng_ref[...]
    lnb = lnb_ref[...]
    n_shift = ushift.shape[2]
    for r in range(1, SUBLANES):
        ushift[r - 1] = ubuf[:, r:r + n_shift, :]
    for b in range(nb):
        for r0 in range(0, tt, CONV_ROWS):
            acc = jnp.broadcast_to(cb, (CONV_ROWS // SUBLANES, SUBLANES, d_conv))
            for j in range(CONV_WIDTH):
                a8, r = divmod(first + j, SUBLANES)
                lo = r0 + a8 * SUBLANES
                src = ubuf[b, lo:lo + CONV_ROWS, :] if r == 0 else ushift[r - 1, b, lo:lo + CONV_ROWS, :]
                acc = acc + wtap[j] * src.reshape(acc.shape)
            acc = acc.reshape(CONV_ROWS, d_conv)
            mu = jnp.mean(acc, axis=-1, keepdims=True)
            dev = acc - mu
            var = jnp.mean(dev * dev, axis=-1, keepdims=True)
            yc = _silu(dev * lax.rsqrt(var + EPS) * lng + lnb)
            cat[b * tt + r0:b * tt + r0 + CONV_ROWS, 0:d_conv] = yc.astype(BF16)
    new_rows = ubuf[:, tt + first:tt + HALO, :]
    ubuf[:, first:HALO, :] = new_rows

    hlb = hlb_ref[...]
    e = jnp.exp(hlb - jnp.max(hlb, axis=0, keepdims=True))
    lbv = e[0:1, :] / jnp.sum(e, axis=0, keepdims=True)
    shift = chunk.bit_length() - 1
    assert chunk == 1 << shift
    sub = min(m, HGRN_ROWS)
    row_i = lax.broadcasted_iota(jnp.int32, (sub, sub), 0)
    col_i = lax.broadcasted_iota(jnp.int32, (sub, sub), 1)
    causal = (row_i >= col_i) & (lax.shift_right_logical(row_i, shift) == lax.shift_right_logical(col_i, shift))
    masks = (jnp.where(causal, 1.0, 0.0).astype(BF16), causal)
    heads, state = _hgrn_tile(zbuf[:, 2 * d_conv:], lbv, hn_ref[...], masks,
                              [[sbuf[b, h] for h in range(HGRN_HEADS)] for b in range(nb)],
                              chunk, tt // chunk, sub, d_head)
    for h in range(HGRN_HEADS):
        cat[:, d_conv + h * d_head:d_conv + (h + 1) * d_head] = heads[h]
        for b in range(nb):
            sbuf[b, h] = state[b][h]

    o_ref[...] = (x + _dot(cat[...], wout_ref[...])).reshape(o_ref.shape)

    @pl.when(t == pl.num_programs(1) - 1)
    def _store_state():
        oconv_ref[...] = ubuf[:, first:HALO, :]
        for b in range(nb):
            for h in range(HGRN_HEADS):
                ohgrn_ref[b, h] = sbuf[b, h].T


def _mix_call(x, ng, win, cw, cb, lng, lnb, hlb, hn, wout, sconv, shgrn, nb, tt):
    bsz, seq, d = x.shape
    d_conv = cw.shape[1]
    d_head = shgrn.shape[-1]
    chunk = min(CHUNK, seq)
    kern = functools.partial(_mix_kernel, nb, tt, chunk)
    xspec = pl.BlockSpec((nb, tt, d), lambda i, t: (i, t, 0))
    cspec = pl.BlockSpec((nb, CONV_BUF, d_conv), lambda i, t: (i, 0, 0))
    sspec = pl.BlockSpec((nb, HGRN_HEADS, d_head, d_head), lambda i, t: (i, 0, 0, 0))
    return pl.pallas_call(
        kern,
        grid=(bsz // nb, seq // tt),
        in_specs=[xspec] + [_const_spec(a.shape) for a in (ng, win, cw, cb, lng, lnb, hlb, hn, wout)]
                 + [cspec, sspec],
        out_specs=[xspec, cspec, sspec],
        out_shape=[jax.ShapeDtypeStruct(x.shape, F32),
                   jax.ShapeDtypeStruct(sconv.shape, F32),
                   jax.ShapeDtypeStruct(shgrn.shape, F32)],
        scratch_shapes=[pltpu.VMEM((nb * tt, win.shape[1]), F32),
                        pltpu.VMEM((nb, HALO + tt, d_conv), F32),
                        pltpu.VMEM((SUBLANES - 1, nb, HALO + tt - SUBLANES, d_conv), F32),
                        pltpu.VMEM((CONV_WIDTH, SUBLANES, d_conv), F32),
                        pltpu.VMEM((nb, HGRN_HEADS, d_head, d_head), F32),
                        pltpu.VMEM((nb * tt, d), BF16)],
        compiler_params=pltpu.CompilerParams(dimension_semantics=("arbitrary", "arbitrary"),
                                             vmem_limit_bytes=VMEM_LIMIT),
        name="mix",
    )(x, ng, win, cw, cb, lng, lnb, hlb, hn, wout, sconv, shgrn)


def _tail_kernel(nb, tt, n_split, kv_native,
                 x_ref, k_ref, v_ref, xg_ref, wq_ref, wo_ref, ng_ref, wg_ref, wu_ref, wd_ref, fg_ref,
                 o_ref, *scratch):
    d = x_ref.shape[-1]
    dh = d // MEM_HEADS
    scale = dh ** -0.5
    ffn_w, w_sem, scratch = scratch[:3], scratch[3], scratch[4:]
    first_step = (pl.program_id(0) == 0) & (pl.program_id(1) == 0)
    w_copies = [pltpu.make_async_copy(src, dst, w_sem.at[j])
                for j, (src, dst) in enumerate(zip((wg_ref, wu_ref, wd_ref), ffn_w))]

    @pl.when(first_step)
    def _start_weights():
        for cp in w_copies:
            cp.start()

    if kv_native:
        kvbuf, sem = scratch
        i, n = pl.program_id(0), pl.num_programs(0)

        def kv_copies(blk, slot):
            return [pltpu.make_async_copy(src.at[pl.ds(blk * nb, nb), :, h, :],
                                          kvbuf.at[slot, j, :, :, h * dh:(h + 1) * dh],
                                          sem.at[slot, j, h])
                    for j, src in enumerate((k_ref, v_ref)) for h in range(MEM_HEADS)]

        def start_all(copies):
            for c, cp in enumerate(copies):
                cp.start(priority=c % 2)

        @pl.when(i == 0)
        def _first_block():
            start_all(kv_copies(0, 0))

        @pl.when(i + 1 < n)
        def _next_block():
            start_all(kv_copies(i + 1, lax.rem(i + 1, 2)))

        slot = lax.rem(i, 2)
        for cp in kv_copies(i, slot):
            cp.wait()
        k_ref, v_ref = kvbuf.at[slot, 0], kvbuf.at[slot, 1]
    kb = [k_ref[b].astype(BF16) for b in range(nb)]
    vb = [v_ref[b].astype(BF16) for b in range(nb)]
    m = nb * tt
    x_all = x_ref[...].reshape(m, d)
    q = _dot(_rms(x_all, xg_ref[...]).astype(BF16), wq_ref[...])
    hsl = [slice(h * dh, (h + 1) * dh) for h in range(MEM_HEADS)]
    qb = (q * scale).astype(BF16)
    stacked = tt * MEM_HEADS <= STACK_ROWS
    scores = []
    for b in range(nb):
        qr = qb[b * tt:(b + 1) * tt]
        if stacked:
            zq = jnp.zeros((tt, dh), BF16)
            q_stack = jnp.concatenate(
                [jnp.concatenate([qr[:, hs] if j == h else zq for j in range(MEM_HEADS)], axis=1)
                 for h, hs in enumerate(hsl)], axis=0)
            scores.append(_dot_nt(q_stack, kb[b]))
        else:
            scores += [_dot_nt(qr[:, hs], kb[b][:, hs]) for hs in hsl]
    sc = jnp.concatenate(scores, axis=0)
    ex = jnp.exp(sc - jnp.max(sc, axis=-1, keepdims=True))
    p = (ex * (1.0 / jnp.sum(ex, axis=-1, keepdims=True))).astype(BF16)
    att = []
    for b in range(nb):
        pb = p[b * MEM_HEADS * tt:(b + 1) * MEM_HEADS * tt]
        if stacked:
            pv = _dot(pb, vb[b])
            heads = [pv[h * tt:(h + 1) * tt, hs] for h, hs in enumerate(hsl)]
        else:
            heads = [_dot(pb[h * tt:(h + 1) * tt], vb[b][:, hs]) for h, hs in enumerate(hsl)]
        att.append(jnp.concatenate(heads, axis=1).astype(BF16))
    xa = x_all + _dot(jnp.concatenate(att, axis=0), wo_ref[...])
    @pl.when(first_step)
    def _weights_ready():
        for cp in w_copies:
            cp.wait()

    wg, wu, wd = ffn_w
    for g0, g1 in _row_groups(m, n_split):
        if nb == 1:
            rows, blk = (0, slice(g0, g1)), (g1 - g0, d)
        else:
            assert g0 % tt == 0 and g1 % tt == 0
            rows, blk = (slice(g0 // tt, g1 // tt),), ((g1 - g0) // tt, tt, d)
        x = _ffn(xa[g0:g1], ng_ref[...], wg[...], wu[...], wd[...])
        o_ref[rows] = _rms(x, fg_ref[...]).reshape(blk)


def _tail_call(x, k, v, xg, wq, wo, ng, wg, wu, wd, fg, nb, tt, n_split):
    bsz, seq, d = x.shape
    kv_native = k.ndim == 4
    kern = functools.partial(_tail_kernel, nb, tt, n_split, kv_native)
    xspec = pl.BlockSpec((nb, tt, d), lambda i, t: (i, t, 0))
    if kv_native:
        assert seq == tt, "the key/value copies are pipelined over the batch axis of the grid"
        kvspec = pl.BlockSpec(memory_space=pl.ANY)
        scratch = [pltpu.VMEM((2, 2, nb, k.shape[1], d), F32), pltpu.SemaphoreType.DMA((2, 2, MEM_HEADS))]
    else:
        kvspec = pl.BlockSpec((nb,) + k.shape[1:], lambda i, t: (i, 0, 0))
        scratch = []
    late = (wg, wu, wd)
    scratch = [pltpu.VMEM(a.shape, a.dtype) for a in late] + [pltpu.SemaphoreType.DMA((len(late),))] + scratch
    hbm = pl.BlockSpec(memory_space=pl.ANY)
    return pl.pallas_call(
        kern,
        grid=(bsz // nb, seq // tt),
        in_specs=[xspec, kvspec, kvspec]
                 + [_const_spec(a.shape) for a in (xg, wq, wo, ng)] + [hbm] * len(late) + [_const_spec(fg.shape)],
        out_specs=xspec,
        out_shape=jax.ShapeDtypeStruct(x.shape, F32),
        scratch_shapes=scratch,
        compiler_params=pltpu.CompilerParams(dimension_semantics=("arbitrary", "arbitrary"),
                                             vmem_limit_bytes=VMEM_LIMIT),
        name="tail",
    )(x, k, v, xg, wq, wo, ng, wg, wu, wd, fg)


def _mix_tail(x, sconv, shgrn, mk, mv, w, fg, mix_tile, tail_tile):
    x, new_conv, new_hgrn = _mix_call(x, w["mix_norm"], w["w_in"], w["conv_dw"], w["conv_dw_b"],
                                      w["conv_ln_g"], w["conv_ln_b"], w["hgrn_lb"], w["hgrn_norm"],
                                      w["w_out"], sconv, shgrn, *mix_tile)
    y = _tail_call(x, mk, mv, w["xattn_norm"], w["xattn_wq"], w["xattn_wo"], w["ffn2_norm"],
                   w["ffn2_w_gate"], w["ffn2_w_up"], w["ffn2_w_down"], fg, *tail_tile)
    return y, new_conv, new_hgrn


def kernel(x_prompt, x_sample, mem_prompt, state_conv, state_hgrn, cache_mem_k, cache_mem_v, ffn1_norm, ffn1_w_gate, ffn1_w_up, ffn1_w_down, mix_norm, w_in, conv_dw, conv_dw_b, conv_ln_g, conv_ln_b, hgrn_lb, hgrn_norm, w_out, mem_norm, mem_wk, mem_wv, xattn_norm, xattn_wq, xattn_wo, ffn2_norm, ffn2_w_gate, ffn2_w_up, ffn2_w_down, final_norm):
    depth = ffn1_norm.shape[0]
    assert depth == 1, "the per-layer forget-gate bound is taken from row 0 of the cumulative softmax"
    bp, seq, d = x_prompt.shape
    bs, dec_seq, _ = x_sample.shape
    n_mem = mem_prompt.shape[1]
    d_hgrn = hgrn_lb.shape[1]
    d_conv = conv_dw.shape[2]
    d_head = d_hgrn // HGRN_HEADS

    def row(a):
        return a.reshape(1, -1)

    fg = row(final_norm)
    (mk, mv, mk_flat, mv_flat), _ = _memkv_call(mem_prompt, row(mem_norm[0]), mem_wk[0], mem_wv[0], 1)

    later = {"w_in": w_in[0], "w_out": w_out[0], "xattn_wq": xattn_wq[0], "xattn_wo": xattn_wo[0],
             "ffn2_w_gate": ffn2_w_gate[0], "ffn2_w_up": ffn2_w_up[0], "ffn2_w_down": ffn2_w_down[0]}
    (xp, xs), later_bf16 = _ffn_call(
        (x_prompt.reshape(bp * seq, d), x_sample.reshape(bs * dec_seq, d)), row(ffn1_norm[0]),
        ffn1_w_gate[0], ffn1_w_up[0], ffn1_w_down[0],
        tiles=((1024, 4), (256, 1)), cast=tuple(later.values()))
    w = dict(zip(later, later_bf16))
    w.update({
        "mix_norm": row(mix_norm[0]), "conv_dw": conv_dw[0], "conv_dw_b": row(conv_dw_b[0]),
        "conv_ln_g": row(conv_ln_g[0]), "conv_ln_b": row(conv_ln_b[0]),
        "hgrn_lb": hgrn_lb, "hgrn_norm": row(hgrn_norm[0]),
        "xattn_norm": row(xattn_norm[0]), "ffn2_norm": row(ffn2_norm[0]),
    })

    y_p, conv_p, hgrn_p = _mix_tail(
        xp.reshape(bp, seq, d), jnp.zeros((bp, CONV_BUF, d_conv), F32),
        jnp.zeros((bp, HGRN_HEADS, d_head, d_head), F32), mk_flat, mv_flat, w, fg,
        mix_tile=(1, 512), tail_tile=(1, 1024, 4))
    y_s, conv_s, hgrn_s = _mix_tail(
        xs.reshape(bs, dec_seq, d), state_conv[0], state_hgrn[0],
        cache_mem_k[0], cache_mem_v[0], w, fg,
        mix_tile=(8, dec_seq), tail_tile=(4, dec_seq, 1))

    return (y_p, y_s, conv_p[None], hgrn_p[None], mk[None], mv[None],
            conv_s[None], hgrn_s[None])
```

```python
import functools

import jax
import jax.numpy as jnp
from jax import lax
from jax.experimental import pallas as pl
from jax.experimental.pallas import tpu as pltpu

F32 = jnp.float32
BF16 = jnp.bfloat16

EPS = 1e-6
CHUNK = 64
CONV_WIDTH = 31
CONV_BUF = CONV_WIDTH - 1
HGRN_HEADS = 4
MEM_HEADS = 4
HALO = 32
CONV_ROWS = 32
SUBLANES = 8
BF16_ROWS = 16
STACK_ROWS = 256
VMEM_LIMIT = 56 * 1024 * 1024


def _dot(a, b):
    return jnp.dot(a, b, preferred_element_type=F32)


def _dot_nt(a, b):
    return lax.dot_general(a, b, (((1,), (1,)), ((), ())), preferred_element_type=F32)


def _dot_tn(a, b):
    return lax.dot_general(a, b, (((0,), (0,)), ((), ())), preferred_element_type=F32)


def _rms(x, g):
    return x * lax.rsqrt(jnp.mean(x * x, axis=-1, keepdims=True) + EPS) * g


def _gated(a, x):
    ha = 0.5 * a
    return ha + ha * jnp.tanh(0.5 * x)


def _silu(x):
    hx = 0.5 * x
    return hx + hx * jnp.tanh(hx)


def _ffn(x, ng, wg, wu, wd):
    xn = _rms(x, ng).astype(BF16)
    h = (_silu(_dot(xn, wg)) * _dot(xn, wu)).astype(BF16)
    return x + 0.5 * _dot(h, wd)


def _const_spec(shape):
    nd = len(shape)
    return pl.BlockSpec(shape, lambda *_: (0,) * nd, pipeline_mode=pl.Buffered(1))


def _row_groups(m, n_split):
    mg = m // n_split
    return [(g * mg, (g + 1) * mg) for g in range(n_split)]


def _cast_specs(arrays, n_steps):
    specs = []
    for a in arrays:
        n_blk = max(n for n in range(1, n_steps + 1) if a.shape[0] % (n * BF16_ROWS) == 0)
        specs.append(pl.BlockSpec((a.shape[0] // n_blk, a.shape[1]),
                                  lambda i, n_blk=n_blk: (jnp.minimum(i, n_blk - 1), 0)))
    return specs


def _cast_rows(src_refs, dst_refs):
    for src, dst in zip(src_refs, dst_refs):
        dst[...] = src[...].astype(BF16)


def _ffn_kernel(plan, n_cast, ng_ref, wg_ref, wu_ref, wd_ref, *rest):
    n_x = len(plan)
    x_refs, cast_in = rest[:n_x], rest[n_x:n_x + n_cast]
    o_refs, cast_out = rest[n_x + n_cast:2 * n_x + n_cast], rest[2 * n_x + n_cast:]
    i = pl.program_id(0)
    for (start, n_blk, n_split), x_ref, o_ref in zip(plan, x_refs, o_refs):
        @pl.when((i >= start) & (i < start + n_blk))
        def _(x_ref=x_ref, o_ref=o_ref, n_split=n_split):
            for g0, g1 in _row_groups(x_ref.shape[0], n_split):
                o_ref[g0:g1, :] = _ffn(x_ref[g0:g1, :], ng_ref[...], wg_ref[...], wu_ref[...], wd_ref[...])
    _cast_rows(cast_in, cast_out)


def _ffn_call(xs, ng, wg, wu, wd, tiles, cast=()):
    d = xs[0].shape[1]
    plan, row_specs, start = [], [], 0
    for x, (tm, n_split) in zip(xs, tiles):
        n_blk = x.shape[0] // tm
        plan.append((start, n_blk, n_split))
        row_specs.append(pl.BlockSpec(
            (tm, d), lambda i, start=start, n_blk=n_blk: (jnp.clip(i - start, 0, n_blk - 1), 0)))
        start += n_blk
    cast_specs = _cast_specs(cast, start)
    out = pl.pallas_call(
        functools.partial(_ffn_kernel, tuple(plan), len(cast)),
        grid=(start,),
        in_specs=[_const_spec(ng.shape), _const_spec(wg.shape), _const_spec(wu.shape), _const_spec(wd.shape)]
                 + row_specs + cast_specs,
        out_specs=row_specs + cast_specs,
        out_shape=[jax.ShapeDtypeStruct(x.shape, F32) for x in xs]
                  + [jax.ShapeDtypeStruct(a.shape, BF16) for a in cast],
        compiler_params=pltpu.CompilerParams(dimension_semantics=("arbitrary",),
                                             vmem_limit_bytes=VMEM_LIMIT),
        name="ffn",
    )(ng, wg, wu, wd, *xs, *cast)
    return out[:len(xs)], out[len(xs):]


def _memkv_kernel(n_cast, m_ref, g_ref, wk_ref, wv_ref, *rest):
    cast_in, (k_ref, v_ref, kb_ref, vb_ref) = rest[:n_cast], rest[n_cast:n_cast + 4]
    cast_out, (wkb, wvb) = rest[n_cast + 4:2 * n_cast + 4], rest[2 * n_cast + 4:]
    _cast_rows(cast_in, cast_out)

    @pl.when(pl.program_id(0) == 0)
    def _round_weights():
        wkb[...] = wk_ref[...].astype(BF16)
        wvb[...] = wv_ref[...].astype(BF16)

    nb, n_mem, n_head, dh = k_ref.shape
    mn = _rms(m_ref[...].reshape(nb * n_mem, m_ref.shape[-1]), g_ref[...]).astype(BF16)
    for w_ref, o_ref, ob_ref in ((wkb, k_ref, kb_ref), (wvb, v_ref, vb_ref)):
        y = _dot(mn, w_ref[...])
        ob_ref[...] = y.astype(BF16).reshape(ob_ref.shape)
        for h in range(n_head):
            o_ref[:, :, h, :] = y[:, h * dh:(h + 1) * dh].reshape(nb, n_mem, dh)


def _memkv_call(mem, g, wk, wv, nb, cast=()):
    bsz, n_mem, d = mem.shape
    dh = d // MEM_HEADS
    n_steps = bsz // nb
    row = pl.BlockSpec((nb, n_mem, d), lambda i: (i, 0, 0))
    kv = pl.BlockSpec((nb, n_mem, MEM_HEADS, dh), lambda i: (i, 0, 0, 0))
    cast_specs = _cast_specs(cast, n_steps)
    out = pl.pallas_call(
        functools.partial(_memkv_kernel, len(cast)),
        grid=(n_steps,),
        in_specs=[row, _const_spec(g.shape), _const_spec(wk.shape), _const_spec(wv.shape)] + cast_specs,
        out_specs=[kv, kv, row, row] + cast_specs,
        out_shape=[jax.ShapeDtypeStruct((bsz, n_mem, MEM_HEADS, dh), F32)] * 2
                  + [jax.ShapeDtypeStruct((bsz, n_mem, d), BF16)] * 2
                  + [jax.ShapeDtypeStruct(a.shape, BF16) for a in cast],
        scratch_shapes=[pltpu.VMEM(wk.shape, BF16), pltpu.VMEM(wv.shape, BF16)],
        compiler_params=pltpu.CompilerParams(dimension_semantics=("arbitrary",),
                                             vmem_limit_bytes=VMEM_LIMIT),
        name="memkv",
    )(mem, g, wk, wv, *cast)
    return out[:4], out[4:]


def _hgrn_tile(z, lbv, hn, masks, states, chunk, chunks_per_seq, sub, d_head):
    m = z.shape[0]
    n_c, per_sub = m // chunk, sub // chunk
    subs = [slice(s0, s0 + sub) for s0 in range(0, m, sub)]
    dh = HGRN_HEADS * d_head
    tri, causal = masks
    zq, zf, v, zg = (z[:, i * dh:(i + 1) * dh] for i in range(4))
    c0, c1 = 0.5 * (1.0 + lbv), 0.5 * (1.0 - lbv)
    ct = c1 * jnp.tanh(0.5 * zf)
    f = c0 + ct
    k = (1.0 - c0) - ct
    q = _silu(zq)
    logf = jnp.log(f)
    l_hi = logf.astype(BF16)
    l_lo = (logf - l_hi.astype(F32)).astype(BF16)
    gc = jnp.concatenate([_dot(tri, l_hi[rs]) + _dot(tri, l_lo[rs]) for rs in subs], axis=0)
    g_last = [gc[(c + 1) * chunk - 1:(c + 1) * chunk, :] for c in range(n_c)]
    decay = [jnp.exp(g) for g in g_last]
    q_dec = (q * jnp.exp(gc)).astype(BF16)
    k_dec_f = k * jnp.exp(-gc)
    k_dec = k_dec_f.astype(BF16)
    k_end = (k_dec_f * jnp.concatenate([jnp.broadcast_to(dc, (chunk, dh)) for dc in decay], axis=0)).astype(BF16)
    vb = v.astype(BF16)
    zero = jnp.zeros((chunk, d_head), BF16)
    out, cur = [], [list(s) for s in states]
    for h in range(HGRN_HEADS):
        hs = slice(h * d_head, (h + 1) * d_head)
        o_intra, kv = [], []
        for rs in subs:
            a = jnp.where(causal, _dot_nt(q_dec[rs, hs], k_dec[rs, hs]), 0.0).astype(BF16)
            o_intra.append(_dot(a, vb[rs, hs]))
            ke = k_end[rs, hs]
            k_blocks = jnp.concatenate(
                [jnp.concatenate([ke[j * chunk:(j + 1) * chunk] if i == j else zero
                                  for i in range(per_sub)], axis=1) for j in range(per_sub)], axis=0)
            kv.append(_dot_tn(vb[rs, hs], k_blocks))
        st_in = []
        for c in range(n_c):
            seq, j = c // chunks_per_seq, c % per_sub
            st_in.append(cur[seq][h])
            cur[seq][h] = st_in[c] * decay[c][:, hs] + kv[c // per_sub][:, j * d_head:(j + 1) * d_head]
        o = []
        for si, rs in enumerate(subs):
            stack = jnp.concatenate([s.astype(BF16) for s in st_in[si * per_sub:(si + 1) * per_sub]], axis=0)
            o_inter = _dot_nt(q_dec[rs, hs], stack)
            o.append(o_intra[si] + jnp.concatenate(
                [o_inter[j * chunk:(j + 1) * chunk, j * d_head:(j + 1) * d_head] for j in range(per_sub)],
                axis=0))
        o = jnp.concatenate(o, axis=0)
        on = o * lax.rsqrt(jnp.mean(o * o, axis=-1, keepdims=True) + EPS) * hn[:, hs]
        out.append((on * _silu(zg[:, hs])).astype(BF16))
    return out, cur


def _mix_kernel(nb, tt, chunk, hgrn_rows,
                x_ref, ng_ref, win_ref, cw_ref, cb_ref, lng_ref, lnb_ref, hlb_ref, hn_ref, wout_ref,
                sconv_ref, shgrn_ref,
                o_ref, oconv_ref, ohgrn_ref,
                zbuf, ubuf, ushift, wtap, sbuf, cat):
    t = pl.program_id(1)
    d_conv = cw_ref.shape[1]
    d_head = sbuf.shape[-1]
    m = nb * tt
    first = HALO - CONV_BUF

    @pl.when((pl.program_id(0) == 0) & (t == 0))
    def _spread_taps():
        wtap[...] = jnp.broadcast_to(cw_ref[...][:, None, :], wtap.shape)

    @pl.when(t == 0)
    def _load_state():
        ubuf[:, 0:first, :] = jnp.zeros((nb, first, d_conv), F32)
        ubuf[:, first:HALO, :] = sconv_ref[...]
        for b in range(nb):
            for h in range(HGRN_HEADS):
                sbuf[b, h] = shgrn_ref[b, h].T

    x = x_ref[...].reshape(m, x_ref.shape[-1])
    zbuf[...] = _dot(_rms(x, ng_ref[...]).astype(BF16), win_ref[...])

    for b in range(nb):
        r = slice(b * tt, (b + 1) * tt)
        ubuf[b, HALO:HALO + tt, :] = _gated(zbuf[r, 0:d_conv], zbuf[r, d_conv:2 * d_conv])
    cb = cb_ref[...]
    lng = lng_ref[...]
    lnb = lnb_ref[...]
    n_shift = ushift.shape[2]
    for r in range(1, SUBLANES):
        ushift[r - 1] = ubuf[:, r:r + n_shift, :]
    for b in range(nb):
        for r0 in range(0, tt, CONV_ROWS):
            acc = jnp.broadcast_to(cb, (CONV_ROWS // SUBLANES, SUBLANES, d_conv))
            for j in range(CONV_WIDTH):
                a8, r = divmod(first + j, SUBLANES)
                lo = r0 + a8 * SUBLANES
                src = ubuf[b, lo:lo + CONV_ROWS, :] if r == 0 else ushift[r - 1, b, lo:lo + CONV_ROWS, :]
                acc = acc + wtap[j] * src.reshape(acc.shape)
            acc = acc.reshape(CONV_ROWS, d_conv)
            mu = jnp.mean(acc, axis=-1, keepdims=True)
            dev = acc - mu
            var = jnp.mean(dev * dev, axis=-1, keepdims=True)
            yc = _silu(dev * lax.rsqrt(var + EPS) * lng + lnb)
            cat[b * tt + r0:b * tt + r0 + CONV_ROWS, 0:d_conv] = yc.astype(BF16)
    new_rows = ubuf[:, tt + first:tt + HALO, :]
    ubuf[:, first:HALO, :] = new_rows

    hlb = hlb_ref[...]
    e = jnp.exp(hlb - jnp.max(hlb, axis=0, keepdims=True))
    lbv = e[0:1, :] / jnp.sum(e, axis=0, keepdims=True)
    shift = chunk.bit_length() - 1
    assert chunk == 1 << shift
    sub = min(m, hgrn_rows)
    row_i = lax.broadcasted_iota(jnp.int32, (sub, sub), 0)
    col_i = lax.broadcasted_iota(jnp.int32, (sub, sub), 1)
    causal = (row_i >= col_i) & (lax.shift_right_logical(row_i, shift) == lax.shift_right_logical(col_i, shift))
    masks = (jnp.where(causal, 1.0, 0.0).astype(BF16), causal)
    heads, state = _hgrn_tile(zbuf[:, 2 * d_conv:], lbv, hn_ref[...], masks,
                              [[sbuf[b, h] for h in range(HGRN_HEADS)] for b in range(nb)],
                              chunk, tt // chunk, sub, d_head)
    for h in range(HGRN_HEADS):
        cat[:, d_conv + h * d_head:d_conv + (h + 1) * d_head] = heads[h]
        for b in range(nb):
            sbuf[b, h] = state[b][h]

    o_ref[...] = (x + _dot(cat[...], wout_ref[...])).reshape(o_ref.shape)

    @pl.when(t == pl.num_programs(1) - 1)
    def _store_state():
        oconv_ref[...] = ubuf[:, first:HALO, :]
        for b in range(nb):
            for h in range(HGRN_HEADS):
                ohgrn_ref[b, h] = sbuf[b, h].T


def _mix_call(x, ng, win, cw, cb, lng, lnb, hlb, hn, wout, sconv, shgrn, nb, tt, hgrn_rows):
    bsz, seq, d = x.shape
    d_conv = cw.shape[1]
    d_head = shgrn.shape[-1]
    chunk = min(CHUNK, seq)
    kern = functools.partial(_mix_kernel, nb, tt, chunk, hgrn_rows)
    xspec = pl.BlockSpec((nb, tt, d), lambda i, t: (i, t, 0))
    cspec = pl.BlockSpec((nb, CONV_BUF, d_conv), lambda i, t: (i, 0, 0))
    sspec = pl.BlockSpec((nb, HGRN_HEADS, d_head, d_head), lambda i, t: (i, 0, 0, 0))
    return pl.pallas_call(
        kern,
        grid=(bsz // nb, seq // tt),
        in_specs=[xspec] + [_const_spec(a.shape) for a in (ng, win, cw, cb, lng, lnb, hlb, hn, wout)]
                 + [cspec, sspec],
        out_specs=[xspec, cspec, sspec],
        out_shape=[jax.ShapeDtypeStruct(x.shape, F32),
                   jax.ShapeDtypeStruct(sconv.shape, F32),
                   jax.ShapeDtypeStruct(shgrn.shape, F32)],
        scratch_shapes=[pltpu.VMEM((nb * tt, win.shape[1]), F32),
                        pltpu.VMEM((nb, HALO + tt, d_conv), F32),
                        pltpu.VMEM((SUBLANES - 1, nb, HALO + tt - SUBLANES, d_conv), F32),
                        pltpu.VMEM((CONV_WIDTH, SUBLANES, d_conv), F32),
                        pltpu.VMEM((nb, HGRN_HEADS, d_head, d_head), F32),
                        pltpu.VMEM((nb * tt, d), BF16)],
        compiler_params=pltpu.CompilerParams(dimension_semantics=("arbitrary", "arbitrary"),
                                             vmem_limit_bytes=VMEM_LIMIT),
        name="mix",
    )(x, ng, win, cw, cb, lng, lnb, hlb, hn, wout, sconv, shgrn)


def _tail_kernel(nb, tt, n_split, kv_native,
                 x_ref, k_ref, v_ref, xg_ref, wq_ref, wo_ref, ng_ref, wg_ref, wu_ref, wd_ref, fg_ref,
                 o_ref, *scratch):
    d = x_ref.shape[-1]
    dh = d // MEM_HEADS
    scale = dh ** -0.5
    ffn_w, w_sem, scratch = scratch[:3], scratch[3], scratch[4:]
    first_step = (pl.program_id(0) == 0) & (pl.program_id(1) == 0)
    w_copies = [pltpu.make_async_copy(src, dst, w_sem.at[j])
                for j, (src, dst) in enumerate(zip((wg_ref, wu_ref, wd_ref), ffn_w))]

    @pl.when(first_step)
    def _start_weights():
        for cp in w_copies:
            cp.start()

    if kv_native:
        kvbuf, sem = scratch
        i, n = pl.program_id(0), pl.num_programs(0)

        def kv_copies(blk, slot):
            return [pltpu.make_async_copy(src.at[pl.ds(blk * nb, nb), :, h, :],
                                          kvbuf.at[slot, j, :, :, h * dh:(h + 1) * dh],
                                          sem.at[slot, j, h])
                    for j, src in enumerate((k_ref, v_ref)) for h in range(MEM_HEADS)]

        def start_all(copies):
            for c, cp in enumerate(copies):
                cp.start(priority=c % 2)

        @pl.when(i == 0)
        def _first_block():
            start_all(kv_copies(0, 0))

        @pl.when(i + 1 < n)
        def _next_block():
            start_all(kv_copies(i + 1, lax.rem(i + 1, 2)))

        slot = lax.rem(i, 2)
        for cp in kv_copies(i, slot):
            cp.wait()
        k_ref, v_ref = kvbuf.at[slot, 0], kvbuf.at[slot, 1]
    kb = [k_ref[b].astype(BF16) for b in range(nb)]
    vb = [v_ref[b].astype(BF16) for b in range(nb)]
    m = nb * tt
    x_all = x_ref[...].reshape(m, d)
    q = _dot(_rms(x_all, xg_ref[...]).astype(BF16), wq_ref[...])
    hsl = [slice(h * dh, (h + 1) * dh) for h in range(MEM_HEADS)]
    qb = (q * scale).astype(BF16)
    stacked = tt * MEM_HEADS <= STACK_ROWS
    scores = []
    for b in range(nb):
        qr = qb[b * tt:(b + 1) * tt]
        if stacked:
            zq = jnp.zeros((tt, dh), BF16)
            q_stack = jnp.concatenate(
                [jnp.concatenate([qr[:, hs] if j == h else zq for j in range(MEM_HEADS)], axis=1)
                 for h, hs in enumerate(hsl)], axis=0)
            scores.append(_dot_nt(q_stack, kb[b]))
        else:
            scores += [_dot_nt(qr[:, hs], kb[b][:, hs]) for hs in hsl]
    sc = jnp.concatenate(scores, axis=0)
    ex = jnp.exp(sc - jnp.max(sc, axis=-1, keepdims=True))
    p = (ex * (1.0 / jnp.sum(ex, axis=-1, keepdims=True))).astype(BF16)
    att = []
    for b in range(nb):
        pb = p[b * MEM_HEADS * tt:(b + 1) * MEM_HEADS * tt]
        if stacked:
            pv = _dot(pb, vb[b])
            heads = [pv[h * tt:(h + 1) * tt, hs] for h, hs in enumerate(hsl)]
        else:
            heads = [_dot(pb[h * tt:(h + 1) * tt], vb[b][:, hs]) for h, hs in enumerate(hsl)]
        att.append(jnp.concatenate(heads, axis=1).astype(BF16))
    xa = x_all + _dot(jnp.concatenate(att, axis=0), wo_ref[...])
    @pl.when(first_step)
    def _weights_ready():
        for cp in w_copies:
            cp.wait()

    wg, wu, wd = ffn_w
    for g0, g1 in _row_groups(m, n_split):
        if nb == 1:
            rows, blk = (0, slice(g0, g1)), (g1 - g0, d)
        else:
            assert g0 % tt == 0 and g1 % tt == 0
            rows, blk = (slice(g0 // tt, g1 // tt),), ((g1 - g0) // tt, tt, d)
        x = _ffn(xa[g0:g1], ng_ref[...], wg[...], wu[...], wd[...])
        o_ref[rows] = _rms(x, fg_ref[...]).reshape(blk)


def _tail_call(x, k, v, xg, wq, wo, ng, wg, wu, wd, fg, nb, tt, n_split):
    bsz, seq, d = x.shape
    kv_native = k.ndim == 4
    kern = functools.partial(_tail_kernel, nb, tt, n_split, kv_native)
    xspec = pl.BlockSpec((nb, tt, d), lambda i, t: (i, t, 0))
    if kv_native:
        assert seq == tt, "the key/value copies are pipelined over the batch axis of the grid"
        kvspec = pl.BlockSpec(memory_space=pl.ANY)
        scratch = [pltpu.VMEM((2, 2, nb, k.shape[1], d), F32), pltpu.SemaphoreType.DMA((2, 2, MEM_HEADS))]
    else:
        kvspec = pl.BlockSpec((nb,) + k.shape[1:], lambda i, t: (i, 0, 0))
        scratch = []
    late = (wg, wu, wd)
    scratch = [pltpu.VMEM(a.shape, a.dtype) for a in late] + [pltpu.SemaphoreType.DMA((len(late),))] + scratch
    hbm = pl.BlockSpec(memory_space=pl.ANY)
    return pl.pallas_call(
        kern,
        grid=(bsz // nb, seq // tt),
        in_specs=[xspec, kvspec, kvspec]
                 + [_const_spec(a.shape) for a in (xg, wq, wo, ng)] + [hbm] * len(late) + [_const_spec(fg.shape)],
        out_specs=xspec,
        out_shape=jax.ShapeDtypeStruct(x.shape, F32),
        scratch_shapes=scratch,
        compiler_params=pltpu.CompilerParams(dimension_semantics=("arbitrary", "arbitrary"),
                                             vmem_limit_bytes=VMEM_LIMIT),
        name="tail",
    )(x, k, v, xg, wq, wo, ng, wg, wu, wd, fg)


def _mix_tail(x, sconv, shgrn, mk, mv, w, fg, mix_tile, tail_tile):
    x, new_conv, new_hgrn = _mix_call(x, w["mix_norm"], w["w_in"], w["conv_dw"], w["conv_dw_b"],
                                      w["conv_ln_g"], w["conv_ln_b"], w["hgrn_lb"], w["hgrn_norm"],
                                      w["w_out"], sconv, shgrn, *mix_tile)
    y = _tail_call(x, mk, mv, w["xattn_norm"], w["xattn_wq"], w["xattn_wo"], w["ffn2_norm"],
                   w["ffn2_w_gate"], w["ffn2_w_up"], w["ffn2_w_down"], fg, *tail_tile)
    return y, new_conv, new_hgrn


def kernel(x_prompt, x_sample, mem_prompt, state_conv, state_hgrn, cache_mem_k, cache_mem_v, ffn1_norm, ffn1_w_gate, ffn1_w_up, ffn1_w_down, mix_norm, w_in, conv_dw, conv_dw_b, conv_ln_g, conv_ln_b, hgrn_lb, hgrn_norm, w_out, mem_norm, mem_wk, mem_wv, xattn_norm, xattn_wq, xattn_wo, ffn2_norm, ffn2_w_gate, ffn2_w_up, ffn2_w_down, final_norm):
    depth = ffn1_norm.shape[0]
    assert depth == 1, "the per-layer forget-gate bound is taken from row 0 of the cumulative softmax"
    bp, seq, d = x_prompt.shape
    bs, dec_seq, _ = x_sample.shape
    n_mem = mem_prompt.shape[1]
    d_hgrn = hgrn_lb.shape[1]
    d_conv = conv_dw.shape[2]
    d_head = d_hgrn // HGRN_HEADS

    def row(a):
        return a.reshape(1, -1)

    fg = row(final_norm)
    (mk, mv, mk_flat, mv_flat), ffn1_w = _memkv_call(
        mem_prompt, row(mem_norm[0]), mem_wk[0], mem_wv[0], 1,
        cast=(ffn1_w_gate[0], ffn1_w_up[0], ffn1_w_down[0]))

    later = {"w_in": w_in[0], "w_out": w_out[0], "xattn_wq": xattn_wq[0], "xattn_wo": xattn_wo[0],
             "ffn2_w_gate": ffn2_w_gate[0], "ffn2_w_up": ffn2_w_up[0], "ffn2_w_down": ffn2_w_down[0]}
    (xp, xs), later_bf16 = _ffn_call(
        (x_prompt.reshape(bp * seq, d), x_sample.reshape(bs * dec_seq, d)), row(ffn1_norm[0]), *ffn1_w,
        tiles=((1024, 4), (512, 2)), cast=tuple(later.values()))
    w = dict(zip(later, later_bf16))
    w.update({
        "mix_norm": row(mix_norm[0]), "conv_dw": conv_dw[0], "conv_dw_b": row(conv_dw_b[0]),
        "conv_ln_g": row(conv_ln_g[0]), "conv_ln_b": row(conv_ln_b[0]),
        "hgrn_lb": hgrn_lb, "hgrn_norm": row(hgrn_norm[0]),
        "xattn_norm": row(xattn_norm[0]), "ffn2_norm": row(ffn2_norm[0]),
    })

    y_p, conv_p, hgrn_p = _mix_tail(
        xp.reshape(bp, seq, d), jnp.zeros((bp, CONV_BUF, d_conv), F32),
        jnp.zeros((bp, HGRN_HEADS, d_head, d_head), F32), mk_flat, mv_flat, w, fg,
        mix_tile=(1, 512, 128), tail_tile=(1, 1024, 4))
    y_s, conv_s, hgrn_s = _mix_tail(
        xs.reshape(bs, dec_seq, d), state_conv[0], state_hgrn[0],
        cache_mem_k[0], cache_mem_v[0], w, fg,
        mix_tile=(8, dec_seq, 256), tail_tile=(4, dec_seq, 1))

    return (y_p, y_s, conv_p[None], hgrn_p[None], mk[None], mv[None],
            conv_s[None], hgrn_s[None])
```

```python
import functools

import jax
import jax.numpy as jnp
from jax import lax
from jax.experimental import pallas as pl
from jax.experimental.pallas import tpu as pltpu

F32 = jnp.float32
BF16 = jnp.bfloat16

EPS = 1e-6
CHUNK = 64
CONV_WIDTH = 31
CONV_BUF = CONV_WIDTH - 1
HGRN_HEADS = 4
MEM_HEADS = 4
HALO = 32
CONV_ROWS = 32
SUBLANES = 8
BF16_ROWS = 16
STACK_ROWS = 256
VMEM_LIMIT = 56 * 1024 * 1024


def _dot(a, b):
    return jnp.dot(a, b, preferred_element_type=F32)


def _dot_nt(a, b):
    return lax.dot_general(a, b, (((1,), (1,)), ((), ())), preferred_element_type=F32)


def _dot_tn(a, b):
    return lax.dot_general(a, b, (((0,), (0,)), ((), ())), preferred_element_type=F32)


def _rms(x, g):
    return x * lax.rsqrt(jnp.mean(x * x, axis=-1, keepdims=True) + EPS) * g


def _gated(a, x):
    ha = 0.5 * a
    return ha + ha * jnp.tanh(0.5 * x)


def _silu(x):
    hx = 0.5 * x
    return hx + hx * jnp.tanh(hx)


def _ffn(x, ng, wg, wu, wd):
    xn = _rms(x, ng).astype(BF16)
    h = (_silu(_dot(xn, wg)) * _dot(xn, wu)).astype(BF16)
    return x + 0.5 * _dot(h, wd)


def _const_spec(shape):
    nd = len(shape)
    return pl.BlockSpec(shape, lambda *_: (0,) * nd, pipeline_mode=pl.Buffered(1))


def _row_groups(m, n_split):
    mg = m // n_split
    return [(g * mg, (g + 1) * mg) for g in range(n_split)]


def _cast_specs(arrays, n_steps):
    specs = []
    for a in arrays:
        n_blk = max(n for n in range(1, n_steps + 1) if a.shape[0] % (n * BF16_ROWS) == 0)
        specs.append(pl.BlockSpec((a.shape[0] // n_blk, a.shape[1]),
                                  lambda i, n_blk=n_blk: (jnp.minimum(i, n_blk - 1), 0)))
    return specs


def _cast_rows(src_refs, dst_refs):
    for src, dst in zip(src_refs, dst_refs):
        dst[...] = src[...].astype(BF16)


def _ffn_kernel(plan, n_cast, ng_ref, wg_ref, wu_ref, wd_ref, *rest):
    n_x = len(plan)
    x_refs, cast_in = rest[:n_x], rest[n_x:n_x + n_cast]
    o_refs, cast_out = rest[n_x + n_cast:2 * n_x + n_cast], rest[2 * n_x + n_cast:]
    i = pl.program_id(0)
    for (start, n_blk, n_split), x_ref, o_ref in zip(plan, x_refs, o_refs):
        @pl.when((i >= start) & (i < start + n_blk))
        def _(x_ref=x_ref, o_ref=o_ref, n_split=n_split):
            for g0, g1 in _row_groups(x_ref.shape[0], n_split):
                o_ref[g0:g1, :] = _ffn(x_ref[g0:g1, :], ng_ref[...], wg_ref[...], wu_ref[...], wd_ref[...])
    _cast_rows(cast_in, cast_out)


def _ffn_call(xs, ng, wg, wu, wd, tiles, cast=()):
    d = xs[0].shape[1]
    plan, row_specs, start = [], [], 0
    for x, (tm, n_split) in zip(xs, tiles):
        n_blk = x.shape[0] // tm
        plan.append((start, n_blk, n_split))
        row_specs.append(pl.BlockSpec(
            (tm, d), lambda i, start=start, n_blk=n_blk: (jnp.clip(i - start, 0, n_blk - 1), 0)))
        start += n_blk
    cast_specs = _cast_specs(cast, start)
    out = pl.pallas_call(
        functools.partial(_ffn_kernel, tuple(plan), len(cast)),
        grid=(start,),
        in_specs=[_const_spec(ng.shape), _const_spec(wg.shape), _const_spec(wu.shape), _const_spec(wd.shape)]
                 + row_specs + cast_specs,
        out_specs=row_specs + cast_specs,
        out_shape=[jax.ShapeDtypeStruct(x.shape, F32) for x in xs]
                  + [jax.ShapeDtypeStruct(a.shape, BF16) for a in cast],
        compiler_params=pltpu.CompilerParams(dimension_semantics=("arbitrary",),
                                             vmem_limit_bytes=VMEM_LIMIT),
        name="ffn",
    )(ng, wg, wu, wd, *xs, *cast)
    return out[:len(xs)], out[len(xs):]


def _memkv_kernel(n_cast, m_ref, g_ref, wk_ref, wv_ref, *rest):
    cast_in, (k_ref, v_ref, kb_ref, vb_ref) = rest[:n_cast], rest[n_cast:n_cast + 4]
    cast_out, (wkb, wvb) = rest[n_cast + 4:2 * n_cast + 4], rest[2 * n_cast + 4:]
    _cast_rows(cast_in, cast_out)

    @pl.when(pl.program_id(0) == 0)
    def _round_weights():
        wkb[...] = wk_ref[...].astype(BF16)
        wvb[...] = wv_ref[...].astype(BF16)

    nb, n_mem, n_head, dh = k_ref.shape
    mn = _rms(m_ref[...].reshape(nb * n_mem, m_ref.shape[-1]), g_ref[...]).astype(BF16)
    for w_ref, o_ref, ob_ref in ((wkb, k_ref, kb_ref), (wvb, v_ref, vb_ref)):
        y = _dot(mn, w_ref[...])
        ob_ref[...] = y.astype(BF16).reshape(ob_ref.shape)
        for h in range(n_head):
            o_ref[:, :, h, :] = y[:, h * dh:(h + 1) * dh].reshape(nb, n_mem, dh)


def _memkv_call(mem, g, wk, wv, nb, cast=()):
    bsz, n_mem, d = mem.shape
    dh = d // MEM_HEADS
    n_steps = bsz // nb
    row = pl.BlockSpec((nb, n_mem, d), lambda i: (i, 0, 0))
    kv = pl.BlockSpec((nb, n_mem, MEM_HEADS, dh), lambda i: (i, 0, 0, 0))
    cast_specs = _cast_specs(cast, n_steps)
    out = pl.pallas_call(
        functools.partial(_memkv_kernel, len(cast)),
        grid=(n_steps,),
        in_specs=[row, _const_spec(g.shape), _const_spec(wk.shape), _const_spec(wv.shape)] + cast_specs,
        out_specs=[kv, kv, row, row] + cast_specs,
        out_shape=[jax.ShapeDtypeStruct((bsz, n_mem, MEM_HEADS, dh), F32)] * 2
                  + [jax.ShapeDtypeStruct((bsz, n_mem, d), BF16)] * 2
                  + [jax.ShapeDtypeStruct(a.shape, BF16) for a in cast],
        scratch_shapes=[pltpu.VMEM(wk.shape, BF16), pltpu.VMEM(wv.shape, BF16)],
        compiler_params=pltpu.CompilerParams(dimension_semantics=("arbitrary",),
                                             vmem_limit_bytes=VMEM_LIMIT),
        name="memkv",
    )(mem, g, wk, wv, *cast)
    return out[:4], out[4:]


def _hgrn_tile(z, lbv, hn, masks, states, chunk, chunks_per_seq, sub, d_head):
    m = z.shape[0]
    n_c, per_sub = m // chunk, sub // chunk
    subs = [slice(s0, s0 + sub) for s0 in range(0, m, sub)]
    dh = HGRN_HEADS * d_head
    o0 = z.shape[1] - 4 * dh
    tri, causal = masks
    c0, c1 = 0.5 * (1.0 + lbv), 0.5 * (1.0 - lbv)
    k_sub, gc_sub, decay = [], [], []
    for rs in subs:
        ct = c1 * jnp.tanh(0.5 * z[rs, o0 + dh:o0 + 2 * dh])
        k_sub.append((1.0 - c0) - ct)
        logf = jnp.log(c0 + ct)
        l_hi = logf.astype(BF16)
        l_lo = (logf - l_hi.astype(F32)).astype(BF16)
        gc = _dot(tri, l_hi) + _dot(tri, l_lo)
        gc_sub.append(gc)
        decay += [jnp.exp(gc[(j + 1) * chunk - 1:(j + 1) * chunk, :]) for j in range(per_sub)]
    zero = jnp.zeros((chunk, d_head), BF16)
    out, cur = [], [list(s) for s in states]
    for h in range(HGRN_HEADS):
        hs = slice(h * d_head, (h + 1) * d_head)
        q_dec, o_intra, kv = [], [], []
        for si, rs in enumerate(subs):
            g = gc_sub[si][:, hs]
            qd = (_silu(z[rs, o0 + h * d_head:o0 + (h + 1) * d_head]) * jnp.exp(g)).astype(BF16)
            kdf = k_sub[si][:, hs] * jnp.exp(-g)
            vb = z[rs, o0 + 2 * dh + h * d_head:o0 + 2 * dh + (h + 1) * d_head].astype(BF16)
            a = jnp.where(causal, _dot_nt(qd, kdf.astype(BF16)), 0.0).astype(BF16)
            o_intra.append(_dot(a, vb))
            k_blocks = jnp.concatenate(
                [jnp.concatenate(
                    [(kdf[j * chunk:(j + 1) * chunk] * decay[si * per_sub + j][:, hs]).astype(BF16)
                     if i == j else zero for i in range(per_sub)], axis=1) for j in range(per_sub)], axis=0)
            kv.append(_dot_tn(vb, k_blocks))
            q_dec.append(qd)
        st_in = []
        for c in range(n_c):
            seq, j = c // chunks_per_seq, c % per_sub
            st_in.append(cur[seq][h])
            cur[seq][h] = st_in[c] * decay[c][:, hs] + kv[c // per_sub][:, j * d_head:(j + 1) * d_head]
        o = []
        for si, rs in enumerate(subs):
            stack = jnp.concatenate([s.astype(BF16) for s in st_in[si * per_sub:(si + 1) * per_sub]], axis=0)
            o_inter = _dot_nt(q_dec[si], stack)
            o.append(o_intra[si] + jnp.concatenate(
                [o_inter[j * chunk:(j + 1) * chunk, j * d_head:(j + 1) * d_head] for j in range(per_sub)],
                axis=0))
        o = jnp.concatenate(o, axis=0)
        on = o * lax.rsqrt(jnp.mean(o * o, axis=-1, keepdims=True) + EPS) * hn[:, hs]
        zg = z[:, o0 + 3 * dh + h * d_head:o0 + 3 * dh + (h + 1) * d_head]
        out.append((on * _silu(zg)).astype(BF16))
    return out, cur


def _mix_kernel(nb, tt, chunk, hgrn_rows,
                x_ref, ng_ref, win_ref, cw_ref, cb_ref, lng_ref, lnb_ref, hlb_ref, hn_ref, wout_ref,
                sconv_ref, shgrn_ref,
                o_ref, oconv_ref, ohgrn_ref,
                zbuf, ubuf, ushift, wtap, sbuf, cat):
    t = pl.program_id(1)
    d_conv = cw_ref.shape[1]
    d_head = sbuf.shape[-1]
    m = nb * tt
    first = HALO - CONV_BUF

    @pl.when((pl.program_id(0) == 0) & (t == 0))
    def _spread_taps():
        wtap[...] = jnp.broadcast_to(cw_ref[...][:, None, :], wtap.shape)

    @pl.when(t == 0)
    def _load_state():
        ubuf[:, 0:first, :] = jnp.zeros((nb, first, d_conv), F32)
        ubuf[:, first:HALO, :] = sconv_ref[...]
        for b in range(nb):
            for h in range(HGRN_HEADS):
                sbuf[b, h] = shgrn_ref[b, h].T

    x = x_ref[...].reshape(m, x_ref.shape[-1])
    zbuf[...] = _dot(_rms(x, ng_ref[...]).astype(BF16), win_ref[...])

    for b in range(nb):
        r = slice(b * tt, (b + 1) * tt)
        ubuf[b, HALO:HALO + tt, :] = _gated(zbuf[r, 0:d_conv], zbuf[r, d_conv:2 * d_conv])
    cb = cb_ref[...]
    lng = lng_ref[...]
    lnb = lnb_ref[...]
    n_shift = ushift.shape[2]
    for r in range(1, SUBLANES):
        ushift[r - 1] = ubuf[:, r:r + n_shift, :]
    for b in range(nb):
        for r0 in range(0, tt, CONV_ROWS):
            acc = jnp.broadcast_to(cb, (CONV_ROWS // SUBLANES, SUBLANES, d_conv))
            for j in range(CONV_WIDTH):
                a8, r = divmod(first + j, SUBLANES)
                lo = r0 + a8 * SUBLANES
                src = ubuf[b, lo:lo + CONV_ROWS, :] if r == 0 else ushift[r - 1, b, lo:lo + CONV_ROWS, :]
                acc = acc + wtap[j] * src.reshape(acc.shape)
            acc = acc.reshape(CONV_ROWS, d_conv)
            mu = jnp.mean(acc, axis=-1, keepdims=True)
            dev = acc - mu
            var = jnp.mean(dev * dev, axis=-1, keepdims=True)
            yc = _silu(dev * lax.rsqrt(var + EPS) * lng + lnb)
            cat[b * tt + r0:b * tt + r0 + CONV_ROWS, 0:d_conv] = yc.astype(BF16)
    new_rows = ubuf[:, tt + first:tt + HALO, :]
    ubuf[:, first:HALO, :] = new_rows

    hlb = hlb_ref[...]
    e = jnp.exp(hlb - jnp.max(hlb, axis=0, keepdims=True))
    lbv = e[0:1, :] / jnp.sum(e, axis=0, keepdims=True)
    shift = chunk.bit_length() - 1
    assert chunk == 1 << shift
    sub = min(m, hgrn_rows)
    row_i = lax.broadcasted_iota(jnp.int32, (sub, sub), 0)
    col_i = lax.broadcasted_iota(jnp.int32, (sub, sub), 1)
    causal = (row_i >= col_i) & (lax.shift_right_logical(row_i, shift) == lax.shift_right_logical(col_i, shift))
    masks = (jnp.where(causal, 1.0, 0.0).astype(BF16), causal)
    heads, state = _hgrn_tile(zbuf, lbv, hn_ref[...], masks,
                              [[sbuf[b, h] for h in range(HGRN_HEADS)] for b in range(nb)],
                              chunk, tt // chunk, sub, d_head)
    for h in range(HGRN_HEADS):
        cat[:, d_conv + h * d_head:d_conv + (h + 1) * d_head] = heads[h]
        for b in range(nb):
            sbuf[b, h] = state[b][h]

    o_ref[...] = (x + _dot(cat[...], wout_ref[...])).reshape(o_ref.shape)

    @pl.when(t == pl.num_programs(1) - 1)
    def _store_state():
        oconv_ref[...] = ubuf[:, first:HALO, :]
        for b in range(nb):
            for h in range(HGRN_HEADS):
                ohgrn_ref[b, h] = sbuf[b, h].T


def _mix_call(x, ng, win, cw, cb, lng, lnb, hlb, hn, wout, sconv, shgrn, nb, tt, hgrn_rows):
    bsz, seq, d = x.shape
    d_conv = cw.shape[1]
    d_head = shgrn.shape[-1]
    chunk = min(CHUNK, seq)
    kern = functools.partial(_mix_kernel, nb, tt, chunk, hgrn_rows)
    xspec = pl.BlockSpec((nb, tt, d), lambda i, t: (i, t, 0))
    cspec = pl.BlockSpec((nb, CONV_BUF, d_conv), lambda i, t: (i, 0, 0))
    sspec = pl.BlockSpec((nb, HGRN_HEADS, d_head, d_head), lambda i, t: (i, 0, 0, 0))
    return pl.pallas_call(
        kern,
        grid=(bsz // nb, seq // tt),
        in_specs=[xspec] + [_const_spec(a.shape) for a in (ng, win, cw, cb, lng, lnb, hlb, hn, wout)]
                 + [cspec, sspec],
        out_specs=[xspec, cspec, sspec],
        out_shape=[jax.ShapeDtypeStruct(x.shape, F32),
                   jax.ShapeDtypeStruct(sconv.shape, F32),
                   jax.ShapeDtypeStruct(shgrn.shape, F32)],
        scratch_shapes=[pltpu.VMEM((nb * tt, win.shape[1]), F32),
                        pltpu.VMEM((nb, HALO + tt, d_conv), F32),
                        pltpu.VMEM((SUBLANES - 1, nb, HALO + tt - SUBLANES, d_conv), F32),
                        pltpu.VMEM((CONV_WIDTH, SUBLANES, d_conv), F32),
                        pltpu.VMEM((nb, HGRN_HEADS, d_head, d_head), F32),
                        pltpu.VMEM((nb * tt, d), BF16)],
        compiler_params=pltpu.CompilerParams(dimension_semantics=("arbitrary", "arbitrary"),
                                             vmem_limit_bytes=VMEM_LIMIT),
        name="mix",
    )(x, ng, win, cw, cb, lng, lnb, hlb, hn, wout, sconv, shgrn)


def _tail_kernel(nb, tt, n_split, kv_native,
                 x_ref, k_ref, v_ref, xg_ref, wq_ref, wo_ref, ng_ref, wg_ref, wu_ref, wd_ref, fg_ref,
                 o_ref, *scratch):
    d = x_ref.shape[-1]
    dh = d // MEM_HEADS
    scale = dh ** -0.5
    ffn_w, w_sem, scratch = scratch[:3], scratch[3], scratch[4:]
    first_step = (pl.program_id(0) == 0) & (pl.program_id(1) == 0)
    w_copies = [pltpu.make_async_copy(src, dst, w_sem.at[j])
                for j, (src, dst) in enumerate(zip((wg_ref, wu_ref, wd_ref), ffn_w))]

    @pl.when(first_step)
    def _start_weights():
        for cp in w_copies:
            cp.start()

    if kv_native:
        kvbuf, sem = scratch
        i, n = pl.program_id(0), pl.num_programs(0)

        def kv_copies(blk, slot):
            return [pltpu.make_async_copy(src.at[pl.ds(blk * nb, nb), :, h, :],
                                          kvbuf.at[slot, j, :, :, h * dh:(h + 1) * dh],
                                          sem.at[slot, j, h])
                    for j, src in enumerate((k_ref, v_ref)) for h in range(MEM_HEADS)]

        def start_all(copies):
            for c, cp in enumerate(copies):
                cp.start(priority=c % 2)

        @pl.when(i == 0)
        def _first_block():
            start_all(kv_copies(0, 0))

        @pl.when(i + 1 < n)
        def _next_block():
            start_all(kv_copies(i + 1, lax.rem(i + 1, 2)))

        slot = lax.rem(i, 2)
        for cp in kv_copies(i, slot):
            cp.wait()
        k_ref, v_ref = kvbuf.at[slot, 0], kvbuf.at[slot, 1]
    kb = [k_ref[b].astype(BF16) for b in range(nb)]
    vb = [v_ref[b].astype(BF16) for b in range(nb)]
    m = nb * tt
    x_all = x_ref[...].reshape(m, d)
    q = _dot(_rms(x_all, xg_ref[...]).astype(BF16), wq_ref[...])
    hsl = [slice(h * dh, (h + 1) * dh) for h in range(MEM_HEADS)]
    qb = (q * scale).astype(BF16)
    stacked = tt * MEM_HEADS <= STACK_ROWS
    scores = []
    for b in range(nb):
        qr = qb[b * tt:(b + 1) * tt]
        if stacked:
            zq = jnp.zeros((tt, dh), BF16)
            q_stack = jnp.concatenate(
                [jnp.concatenate([qr[:, hs] if j == h else zq for j in range(MEM_HEADS)], axis=1)
                 for h, hs in enumerate(hsl)], axis=0)
            scores.append(_dot_nt(q_stack, kb[b]))
        else:
            scores += [_dot_nt(qr[:, hs], kb[b][:, hs]) for hs in hsl]
    sc = jnp.concatenate(scores, axis=0)
    ex = jnp.exp(sc - jnp.max(sc, axis=-1, keepdims=True))
    p = (ex * (1.0 / jnp.sum(ex, axis=-1, keepdims=True))).astype(BF16)
    att = []
    for b in range(nb):
        pb = p[b * MEM_HEADS * tt:(b + 1) * MEM_HEADS * tt]
        if stacked:
            pv = _dot(pb, vb[b])
            heads = [pv[h * tt:(h + 1) * tt, hs] for h, hs in enumerate(hsl)]
        else:
            heads = [_dot(pb[h * tt:(h + 1) * tt], vb[b][:, hs]) for h, hs in enumerate(hsl)]
        att.append(jnp.concatenate(heads, axis=1).astype(BF16))
    xa = x_all + _dot(jnp.concatenate(att, axis=0), wo_ref[...])
    @pl.when(first_step)
    def _weights_ready():
        for cp in w_copies:
            cp.wait()

    wg, wu, wd = ffn_w
    for g0, g1 in _row_groups(m, n_split):
        if nb == 1:
            rows, blk = (0, slice(g0, g1)), (g1 - g0, d)
        else:
            assert g0 % tt == 0 and g1 % tt == 0
            rows, blk = (slice(g0 // tt, g1 // tt),), ((g1 - g0) // tt, tt, d)
        x = _ffn(xa[g0:g1], ng_ref[...], wg[...], wu[...], wd[...])
        o_ref[rows] = _rms(x, fg_ref[...]).reshape(blk)


def _tail_call(x, k, v, xg, wq, wo, ng, wg, wu, wd, fg, nb, tt, n_split):
    bsz, seq, d = x.shape
    kv_native = k.ndim == 4
    kern = functools.partial(_tail_kernel, nb, tt, n_split, kv_native)
    xspec = pl.BlockSpec((nb, tt, d), lambda i, t: (i, t, 0))
    if kv_native:
        assert seq == tt, "the key/value copies are pipelined over the batch axis of the grid"
        kvspec = pl.BlockSpec(memory_space=pl.ANY)
        scratch = [pltpu.VMEM((2, 2, nb, k.shape[1], d), F32), pltpu.SemaphoreType.DMA((2, 2, MEM_HEADS))]
    else:
        kvspec = pl.BlockSpec((nb,) + k.shape[1:], lambda i, t: (i, 0, 0))
        scratch = []
    late = (wg, wu, wd)
    scratch = [pltpu.VMEM(a.shape, a.dtype) for a in late] + [pltpu.SemaphoreType.DMA((len(late),))] + scratch
    hbm = pl.BlockSpec(memory_space=pl.ANY)
    return pl.pallas_call(
        kern,
        grid=(bsz // nb, seq // tt),
        in_specs=[xspec, kvspec, kvspec]
                 + [_const_spec(a.shape) for a in (xg, wq, wo, ng)] + [hbm] * len(late) + [_const_spec(fg.shape)],
        out_specs=xspec,
        out_shape=jax.ShapeDtypeStruct(x.shape, F32),
        scratch_shapes=scratch,
        compiler_params=pltpu.CompilerParams(dimension_semantics=("arbitrary", "arbitrary"),
                                             vmem_limit_bytes=VMEM_LIMIT),
        name="tail",
    )(x, k, v, xg, wq, wo, ng, wg, wu, wd, fg)


def _mix_tail(x, sconv, shgrn, mk, mv, w, fg, mix_tile, tail_tile):
    x, new_conv, new_hgrn = _mix_call(x, w["mix_norm"], w["w_in"], w["conv_dw"], w["conv_dw_b"],
                                      w["conv_ln_g"], w["conv_ln_b"], w["hgrn_lb"], w["hgrn_norm"],
                                      w["w_out"], sconv, shgrn, *mix_tile)
    y = _tail_call(x, mk, mv, w["xattn_norm"], w["xattn_wq"], w["xattn_wo"], w["ffn2_norm"],
                   w["ffn2_w_gate"], w["ffn2_w_up"], w["ffn2_w_down"], fg, *tail_tile)
    return y, new_conv, new_hgrn


def kernel(x_prompt, x_sample, mem_prompt, state_conv, state_hgrn, cache_mem_k, cache_mem_v, ffn1_norm, ffn1_w_gate, ffn1_w_up, ffn1_w_down, mix_norm, w_in, conv_dw, conv_dw_b, conv_ln_g, conv_ln_b, hgrn_lb, hgrn_norm, w_out, mem_norm, mem_wk, mem_wv, xattn_norm, xattn_wq, xattn_wo, ffn2_norm, ffn2_w_gate, ffn2_w_up, ffn2_w_down, final_norm):
    depth = ffn1_norm.shape[0]
    assert depth == 1, "the per-layer forget-gate bound is taken from row 0 of the cumulative softmax"
    bp, seq, d = x_prompt.shape
    bs, dec_seq, _ = x_sample.shape
    n_mem = mem_prompt.shape[1]
    d_hgrn = hgrn_lb.shape[1]
    d_conv = conv_dw.shape[2]
    d_head = d_hgrn // HGRN_HEADS

    def row(a):
        return a.reshape(1, -1)

    fg = row(final_norm)
    (mk, mv, mk_flat, mv_flat), ffn1_w = _memkv_call(
        mem_prompt, row(mem_norm[0]), mem_wk[0], mem_wv[0], 1,
        cast=(ffn1_w_gate[0], ffn1_w_up[0], ffn1_w_down[0]))

    later = {"w_in": w_in[0], "w_out": w_out[0], "xattn_wq": xattn_wq[0], "xattn_wo": xattn_wo[0],
             "ffn2_w_gate": ffn2_w_gate[0], "ffn2_w_up": ffn2_w_up[0], "ffn2_w_down": ffn2_w_down[0]}
    (xp, xs), later_bf16 = _ffn_call(
        (x_prompt.reshape(bp * seq, d), x_sample.reshape(bs * dec_seq, d)), row(ffn1_norm[0]), *ffn1_w,
        tiles=((1024, 4), (512, 2)), cast=tuple(later.values()))
    w = dict(zip(later, later_bf16))
    w.update({
        "mix_norm": row(mix_norm[0]), "conv_dw": conv_dw[0], "conv_dw_b": row(conv_dw_b[0]),
        "conv_ln_g": row(conv_ln_g[0]), "conv_ln_b": row(conv_ln_b[0]),
        "hgrn_lb": hgrn_lb, "hgrn_norm": row(hgrn_norm[0]),
        "xattn_norm": row(xattn_norm[0]), "ffn2_norm": row(ffn2_norm[0]),
    })

    y_p, conv_p, hgrn_p = _mix_tail(
        xp.reshape(bp, seq, d), jnp.zeros((bp, CONV_BUF, d_conv), F32),
        jnp.zeros((bp, HGRN_HEADS, d_head, d_head), F32), mk_flat, mv_flat, w, fg,
        mix_tile=(1, 512, 128), tail_tile=(1, 1024, 4))
    y_s, conv_s, hgrn_s = _mix_tail(
        xs.reshape(bs, dec_seq, d), state_conv[0], state_hgrn[0],
        cache_mem_k[0], cache_mem_v[0], w, fg,
        mix_tile=(8, dec_seq, 256), tail_tile=(4, dec_seq, 1))

    return (y_p, y_s, conv_p[None], hgrn_p[None], mk[None], mv[None],
            conv_s[None], hgrn_s[None])
```

```python
import functools

import jax
import jax.numpy as jnp
from jax import lax
from jax.experimental import pallas as pl
from jax.experimental.pallas import tpu as pltpu

F32 = jnp.float32
BF16 = jnp.bfloat16

EPS = 1e-6
CHUNK = 64
CONV_WIDTH = 31
CONV_BUF = CONV_WIDTH - 1
HGRN_HEADS = 4
MEM_HEADS = 4
HALO = 32
CONV_ROWS = 32
SUBLANES = 8
BF16_ROWS = 16
STACK_ROWS = 256
VMEM_LIMIT = 56 * 1024 * 1024


def _dot(a, b):
    return jnp.dot(a, b, preferred_element_type=F32)


def _dot_nt(a, b):
    return lax.dot_general(a, b, (((1,), (1,)), ((), ())), preferred_element_type=F32)


def _dot_tn(a, b):
    return lax.dot_general(a, b, (((0,), (0,)), ((), ())), preferred_element_type=F32)


def _rms(x, g):
    return x * lax.rsqrt(jnp.mean(x * x, axis=-1, keepdims=True) + EPS) * g


def _gated(a, x):
    ha = 0.5 * a
    return ha + ha * jnp.tanh(0.5 * x)


def _silu(x):
    hx = 0.5 * x
    return hx + hx * jnp.tanh(hx)


def _ffn(x, ng, wg, wu, wd):
    xn = _rms(x, ng).astype(BF16)
    h = (_silu(_dot(xn, wg)) * _dot(xn, wu)).astype(BF16)
    return x + 0.5 * _dot(h, wd)


def _const_spec(shape):
    nd = len(shape)
    return pl.BlockSpec(shape, lambda *_: (0,) * nd, pipeline_mode=pl.Buffered(1))


def _row_groups(m, n_split):
    mg = m // n_split
    return [(g * mg, (g + 1) * mg) for g in range(n_split)]


def _cast_specs(arrays, n_steps):
    specs = []
    for a in arrays:
        n_blk = max(n for n in range(1, n_steps + 1) if a.shape[0] % (n * BF16_ROWS) == 0)
        specs.append(pl.BlockSpec((a.shape[0] // n_blk, a.shape[1]),
                                  lambda i, n_blk=n_blk: (jnp.minimum(i, n_blk - 1), 0)))
    return specs


def _cast_rows(src_refs, dst_refs):
    for src, dst in zip(src_refs, dst_refs):
        dst[...] = src[...].astype(BF16)


def _ffn_kernel(plan, n_cast, ng_ref, wg_ref, wu_ref, wd_ref, *rest):
    n_x = len(plan)
    x_refs, cast_in = rest[:n_x], rest[n_x:n_x + n_cast]
    o_refs, cast_out = rest[n_x + n_cast:2 * n_x + n_cast], rest[2 * n_x + n_cast:]
    i = pl.program_id(0)
    for (start, n_blk, n_split), x_ref, o_ref in zip(plan, x_refs, o_refs):
        @pl.when((i >= start) & (i < start + n_blk))
        def _(x_ref=x_ref, o_ref=o_ref, n_split=n_split):
            for g0, g1 in _row_groups(x_ref.shape[0], n_split):
                o_ref[g0:g1, :] = _ffn(x_ref[g0:g1, :], ng_ref[...], wg_ref[...], wu_ref[...], wd_ref[...])
    _cast_rows(cast_in, cast_out)


def _ffn_call(xs, ng, wg, wu, wd, tiles, cast=()):
    d = xs[0].shape[1]
    plan, row_specs, start = [], [], 0
    for x, (tm, n_split) in zip(xs, tiles):
        n_blk = x.shape[0] // tm
        plan.append((start, n_blk, n_split))
        row_specs.append(pl.BlockSpec(
            (tm, d), lambda i, start=start, n_blk=n_blk: (jnp.clip(i - start, 0, n_blk - 1), 0)))
        start += n_blk
    cast_specs = _cast_specs(cast, start)
    out = pl.pallas_call(
        functools.partial(_ffn_kernel, tuple(plan), len(cast)),
        grid=(start,),
        in_specs=[_const_spec(ng.shape), _const_spec(wg.shape), _const_spec(wu.shape), _const_spec(wd.shape)]
                 + row_specs + cast_specs,
        out_specs=row_specs + cast_specs,
        out_shape=[jax.ShapeDtypeStruct(x.shape, F32) for x in xs]
                  + [jax.ShapeDtypeStruct(a.shape, BF16) for a in cast],
        compiler_params=pltpu.CompilerParams(dimension_semantics=("arbitrary",),
                                             vmem_limit_bytes=VMEM_LIMIT),
        name="ffn",
    )(ng, wg, wu, wd, *xs, *cast)
    return out[:len(xs)], out[len(xs):]


def _memkv_kernel(n_cast, m_ref, g_ref, wk_ref, wv_ref, *rest):
    cast_in, (k_ref, v_ref, kb_ref, vb_ref) = rest[:n_cast], rest[n_cast:n_cast + 4]
    cast_out, (wkb, wvb) = rest[n_cast + 4:2 * n_cast + 4], rest[2 * n_cast + 4:]
    _cast_rows(cast_in, cast_out)

    @pl.when(pl.program_id(0) == 0)
    def _round_weights():
        wkb[...] = wk_ref[...].astype(BF16)
        wvb[...] = wv_ref[...].astype(BF16)

    nb, n_mem, n_head, dh = k_ref.shape
    mn = _rms(m_ref[...].reshape(nb * n_mem, m_ref.shape[-1]), g_ref[...]).astype(BF16)
    for w_ref, o_ref, ob_ref in ((wkb, k_ref, kb_ref), (wvb, v_ref, vb_ref)):
        y = _dot(mn, w_ref[...])
        ob_ref[...] = y.astype(BF16).reshape(ob_ref.shape)
        for h in range(n_head):
            o_ref[:, :, h, :] = y[:, h * dh:(h + 1) * dh].reshape(nb, n_mem, dh)


def _memkv_call(mem, g, wk, wv, nb, cast=()):
    bsz, n_mem, d = mem.shape
    dh = d // MEM_HEADS
    n_steps = bsz // nb
    row = pl.BlockSpec((nb, n_mem, d), lambda i: (i, 0, 0))
    kv = pl.BlockSpec((nb, n_mem, MEM_HEADS, dh), lambda i: (i, 0, 0, 0))
    cast_specs = _cast_specs(cast, n_steps)
    out = pl.pallas_call(
        functools.partial(_memkv_kernel, len(cast)),
        grid=(n_steps,),
        in_specs=[row, _const_spec(g.shape), _const_spec(wk.shape), _const_spec(wv.shape)] + cast_specs,
        out_specs=[kv, kv, row, row] + cast_specs,
        out_shape=[jax.ShapeDtypeStruct((bsz, n_mem, MEM_HEADS, dh), F32)] * 2
                  + [jax.ShapeDtypeStruct((bsz, n_mem, d), BF16)] * 2
                  + [jax.ShapeDtypeStruct(a.shape, BF16) for a in cast],
        scratch_shapes=[pltpu.VMEM(wk.shape, BF16), pltpu.VMEM(wv.shape, BF16)],
        compiler_params=pltpu.CompilerParams(dimension_semantics=("arbitrary",),
                                             vmem_limit_bytes=VMEM_LIMIT),
        name="memkv",
    )(mem, g, wk, wv, *cast)
    return out[:4], out[4:]


def _hgrn_tile(z, lbv, hn, masks, states, chunk, chunks_per_seq, sub, d_head):
    m = z.shape[0]
    n_c, per_sub = m // chunk, sub // chunk
    subs = [slice(s0, s0 + sub) for s0 in range(0, m, sub)]
    dh = HGRN_HEADS * d_head
    o0 = z.shape[1] - 4 * dh
    tri, causal = masks
    c0, c1 = 0.5 * (1.0 + lbv), 0.5 * (1.0 - lbv)
    k_sub, gc_sub, decay = [], [], []
    for rs in subs:
        ct = c1 * jnp.tanh(0.5 * z[rs, o0 + dh:o0 + 2 * dh])
        k_sub.append((1.0 - c0) - ct)
        logf = jnp.log(c0 + ct)
        l_hi = logf.astype(BF16)
        l_lo = (logf - l_hi.astype(F32)).astype(BF16)
        gc = _dot(tri, l_hi) + _dot(tri, l_lo)
        gc_sub.append(gc)
        decay += [jnp.exp(gc[(j + 1) * chunk - 1:(j + 1) * chunk, :]) for j in range(per_sub)]
    zero = jnp.zeros((chunk, d_head), BF16)
    out, cur = [], [list(s) for s in states]
    for h in range(HGRN_HEADS):
        hs = slice(h * d_head, (h + 1) * d_head)
        q_dec, o_intra, kv = [], [], []
        for si, rs in enumerate(subs):
            g = gc_sub[si][:, hs]
            qd = (_silu(z[rs, o0 + h * d_head:o0 + (h + 1) * d_head]) * jnp.exp(g)).astype(BF16)
            kdf = k_sub[si][:, hs] * jnp.exp(-g)
            vb = z[rs, o0 + 2 * dh + h * d_head:o0 + 2 * dh + (h + 1) * d_head].astype(BF16)
            a = jnp.where(causal, _dot_nt(qd, kdf.astype(BF16)), 0.0).astype(BF16)
            o_intra.append(_dot(a, vb))
            k_blocks = jnp.concatenate(
                [jnp.concatenate(
                    [(kdf[j * chunk:(j + 1) * chunk] * decay[si * per_sub + j][:, hs]).astype(BF16)
                     if i == j else zero for i in range(per_sub)], axis=1) for j in range(per_sub)], axis=0)
            kv.append(_dot_tn(vb, k_blocks))
            q_dec.append(qd)
        st_in = []
        for c in range(n_c):
            seq, j = c // chunks_per_seq, c % per_sub
            st_in.append(cur[seq][h])
            cur[seq][h] = st_in[c] * decay[c][:, hs] + kv[c // per_sub][:, j * d_head:(j + 1) * d_head]
        o = []
        for si, rs in enumerate(subs):
            stack = jnp.concatenate([s.astype(BF16) for s in st_in[si * per_sub:(si + 1) * per_sub]], axis=0)
            o_inter = _dot_nt(q_dec[si], stack)
            o.append(o_intra[si] + jnp.concatenate(
                [o_inter[j * chunk:(j + 1) * chunk, j * d_head:(j + 1) * d_head] for j in range(per_sub)],
                axis=0))
        o = jnp.concatenate(o, axis=0)
        on = o * lax.rsqrt(jnp.mean(o * o, axis=-1, keepdims=True) + EPS) * hn[:, hs]
        zg = z[:, o0 + 3 * dh + h * d_head:o0 + 3 * dh + (h + 1) * d_head]
        out.append((on * _silu(zg)).astype(BF16))
    return out, cur


def _mix_kernel(nb, tt, chunk, hgrn_rows,
                x_ref, ng_ref, win_ref, cw_ref, cb_ref, lng_ref, lnb_ref, hlb_ref, hn_ref, wout_ref,
                sconv_ref, shgrn_ref,
                o_ref, oconv_ref, ohgrn_ref,
                zbuf, ubuf, ushift, wtap, sbuf, cat):
    t = pl.program_id(1)
    d_conv = cw_ref.shape[1]
    d_head = sbuf.shape[-1]
    m = nb * tt
    first = HALO - CONV_BUF

    @pl.when((pl.program_id(0) == 0) & (t == 0))
    def _spread_taps():
        wtap[...] = jnp.broadcast_to(cw_ref[...][:, None, :], wtap.shape)

    @pl.when(t == 0)
    def _load_state():
        ubuf[:, 0:first, :] = jnp.zeros((nb, first, d_conv), F32)
        ubuf[:, first:HALO, :] = sconv_ref[...]
        for b in range(nb):
            for h in range(HGRN_HEADS):
                sbuf[b, h] = shgrn_ref[b, h].T

    x = x_ref[...].reshape(m, x_ref.shape[-1])
    zbuf[...] = _dot(_rms(x, ng_ref[...]).astype(BF16), win_ref[...])

    for b in range(nb):
        r = slice(b * tt, (b + 1) * tt)
        ubuf[b, HALO:HALO + tt, :] = _gated(zbuf[r, 0:d_conv], zbuf[r, d_conv:2 * d_conv])
    cb = cb_ref[...]
    lng = lng_ref[...]
    lnb = lnb_ref[...]
    n_shift = ushift.shape[2]
    for r in range(1, SUBLANES):
        ushift[r - 1] = ubuf[:, r:r + n_shift, :]
    for b in range(nb):
        for r0 in range(0, tt, CONV_ROWS):
            acc = jnp.broadcast_to(cb, (CONV_ROWS // SUBLANES, SUBLANES, d_conv))
            for j in range(CONV_WIDTH):
                a8, r = divmod(first + j, SUBLANES)
                lo = r0 + a8 * SUBLANES
                src = ubuf[b, lo:lo + CONV_ROWS, :] if r == 0 else ushift[r - 1, b, lo:lo + CONV_ROWS, :]
                acc = acc + wtap[j] * src.reshape(acc.shape)
            acc = acc.reshape(CONV_ROWS, d_conv)
            mu = jnp.mean(acc, axis=-1, keepdims=True)
            dev = acc - mu
            var = jnp.mean(dev * dev, axis=-1, keepdims=True)
            yc = _silu(dev * lax.rsqrt(var + EPS) * lng + lnb)
            cat[b * tt + r0:b * tt + r0 + CONV_ROWS, 0:d_conv] = yc.astype(BF16)
    new_rows = ubuf[:, tt + first:tt + HALO, :]
    ubuf[:, first:HALO, :] = new_rows

    hlb = hlb_ref[...]
    e = jnp.exp(hlb - jnp.max(hlb, axis=0, keepdims=True))
    lbv = e[0:1, :] / jnp.sum(e, axis=0, keepdims=True)
    shift = chunk.bit_length() - 1
    assert chunk == 1 << shift
    sub = min(m, hgrn_rows)
    row_i = lax.broadcasted_iota(jnp.int32, (sub, sub), 0)
    col_i = lax.broadcasted_iota(jnp.int32, (sub, sub), 1)
    causal = (row_i >= col_i) & (lax.shift_right_logical(row_i, shift) == lax.shift_right_logical(col_i, shift))
    masks = (jnp.where(causal, 1.0, 0.0).astype(BF16), causal)
    heads, state = _hgrn_tile(zbuf, lbv, hn_ref[...], masks,
                              [[sbuf[b, h] for h in range(HGRN_HEADS)] for b in range(nb)],
                              chunk, tt // chunk, sub, d_head)
    for h in range(HGRN_HEADS):
        cat[:, d_conv + h * d_head:d_conv + (h + 1) * d_head] = heads[h]
        for b in range(nb):
            sbuf[b, h] = state[b][h]

    o_ref[...] = (x + _dot(cat[...], wout_ref[...])).reshape(o_ref.shape)

    @pl.when(t == pl.num_programs(1) - 1)
    def _store_state():
        oconv_ref[...] = ubuf[:, first:HALO, :]
        for b in range(nb):
            for h in range(HGRN_HEADS):
                ohgrn_ref[b, h] = sbuf[b, h].T


def _mix_call(x, ng, win, cw, cb, lng, lnb, hlb, hn, wout, sconv, shgrn, nb, tt, hgrn_rows):
    bsz, seq, d = x.shape
    d_conv = cw.shape[1]
    d_head = shgrn.shape[-1]
    chunk = min(CHUNK, seq)
    kern = functools.partial(_mix_kernel, nb, tt, chunk, hgrn_rows)
    xspec = pl.BlockSpec((nb, tt, d), lambda i, t: (i, t, 0))
    cspec = pl.BlockSpec((nb, CONV_BUF, d_conv), lambda i, t: (i, 0, 0))
    sspec = pl.BlockSpec((nb, HGRN_HEADS, d_head, d_head), lambda i, t: (i, 0, 0, 0))
    return pl.pallas_call(
        kern,
        grid=(bsz // nb, seq // tt),
        in_specs=[xspec] + [_const_spec(a.shape) for a in (ng, win, cw, cb, lng, lnb, hlb, hn, wout)]
                 + [cspec, sspec],
        out_specs=[xspec, cspec, sspec],
        out_shape=[jax.ShapeDtypeStruct(x.shape, F32),
                   jax.ShapeDtypeStruct(sconv.shape, F32),
                   jax.ShapeDtypeStruct(shgrn.shape, F32)],
        scratch_shapes=[pltpu.VMEM((nb * tt, win.shape[1]), F32),
                        pltpu.VMEM((nb, HALO + tt, d_conv), F32),
                        pltpu.VMEM((SUBLANES - 1, nb, HALO + tt - SUBLANES, d_conv), F32),
                        pltpu.VMEM((CONV_WIDTH, SUBLANES, d_conv), F32),
                        pltpu.VMEM((nb, HGRN_HEADS, d_head, d_head), F32),
                        pltpu.VMEM((nb * tt, d), BF16)],
        compiler_params=pltpu.CompilerParams(dimension_semantics=("arbitrary", "arbitrary"),
                                             vmem_limit_bytes=VMEM_LIMIT),
        name="mix",
    )(x, ng, win, cw, cb, lng, lnb, hlb, hn, wout, sconv, shgrn)


def _tail_kernel(nb, tt, n_split, kv_native,
                 x_ref, k_ref, v_ref, xg_ref, wq_ref, wo_ref, ng_ref, wg_ref, wu_ref, wd_ref, fg_ref,
                 o_ref, *scratch):
    d = x_ref.shape[-1]
    dh = d // MEM_HEADS
    scale = dh ** -0.5
    if kv_native:
        kvbuf, sem = scratch
        i, n = pl.program_id(0), pl.num_programs(0)

        def kv_copies(blk, slot):
            return [pltpu.make_async_copy(src.at[pl.ds(blk * nb, nb), :, h, :],
                                          kvbuf.at[slot, j, :, :, h * dh:(h + 1) * dh],
                                          sem.at[slot, j, h])
                    for j, src in enumerate((k_ref, v_ref)) for h in range(MEM_HEADS)]

        def start_all(copies):
            for c, cp in enumerate(copies):
                cp.start(priority=c % 2)

        @pl.when(i == 0)
        def _first_block():
            start_all(kv_copies(0, 0))

        @pl.when(i + 1 < n)
        def _next_block():
            start_all(kv_copies(i + 1, lax.rem(i + 1, 2)))

        slot = lax.rem(i, 2)
        for cp in kv_copies(i, slot):
            cp.wait()
        k_ref, v_ref = kvbuf.at[slot, 0], kvbuf.at[slot, 1]
    kb = [k_ref[b].astype(BF16) for b in range(nb)]
    vb = [v_ref[b].astype(BF16) for b in range(nb)]
    m = nb * tt
    x_all = x_ref[...].reshape(m, d)
    q = _dot(_rms(x_all, xg_ref[...]).astype(BF16), wq_ref[...])
    hsl = [slice(h * dh, (h + 1) * dh) for h in range(MEM_HEADS)]
    qb = (q * scale).astype(BF16)
    stacked = tt * MEM_HEADS <= STACK_ROWS
    scores = []
    for b in range(nb):
        qr = qb[b * tt:(b + 1) * tt]
        if stacked:
            zq = jnp.zeros((tt, dh), BF16)
            q_stack = jnp.concatenate(
                [jnp.concatenate([qr[:, hs] if j == h else zq for j in range(MEM_HEADS)], axis=1)
                 for h, hs in enumerate(hsl)], axis=0)
            scores.append(_dot_nt(q_stack, kb[b]))
        else:
            scores += [_dot_nt(qr[:, hs], kb[b][:, hs]) for hs in hsl]
    sc = jnp.concatenate(scores, axis=0)
    ex = jnp.exp(sc - jnp.max(sc, axis=-1, keepdims=True))
    p = (ex * (1.0 / jnp.sum(ex, axis=-1, keepdims=True))).astype(BF16)
    att = []
    for b in range(nb):
        pb = p[b * MEM_HEADS * tt:(b + 1) * MEM_HEADS * tt]
        if stacked:
            pv = _dot(pb, vb[b])
            heads = [pv[h * tt:(h + 1) * tt, hs] for h, hs in enumerate(hsl)]
        else:
            heads = [_dot(pb[h * tt:(h + 1) * tt], vb[b][:, hs]) for h, hs in enumerate(hsl)]
        att.append(jnp.concatenate(heads, axis=1).astype(BF16))
    xa = x_all + _dot(jnp.concatenate(att, axis=0), wo_ref[...])
    wg, wu, wd = wg_ref, wu_ref, wd_ref
    for g0, g1 in _row_groups(m, n_split):
        if nb == 1:
            rows, blk = (0, slice(g0, g1)), (g1 - g0, d)
        else:
            assert g0 % tt == 0 and g1 % tt == 0
            rows, blk = (slice(g0 // tt, g1 // tt),), ((g1 - g0) // tt, tt, d)
        x = _ffn(xa[g0:g1], ng_ref[...], wg[...], wu[...], wd[...])
        o_ref[rows] = _rms(x, fg_ref[...]).reshape(blk)


def _tail_call(x, k, v, xg, wq, wo, ng, wg, wu, wd, fg, nb, tt, n_split):
    bsz, seq, d = x.shape
    kv_native = k.ndim == 4
    kern = functools.partial(_tail_kernel, nb, tt, n_split, kv_native)
    xspec = pl.BlockSpec((nb, tt, d), lambda i, t: (i, t, 0))
    if kv_native:
        assert seq == tt, "the key/value copies are pipelined over the batch axis of the grid"
        kvspec = pl.BlockSpec(memory_space=pl.ANY)
        scratch = [pltpu.VMEM((2, 2, nb, k.shape[1], d), F32), pltpu.SemaphoreType.DMA((2, 2, MEM_HEADS))]
    else:
        kvspec = pl.BlockSpec((nb,) + k.shape[1:], lambda i, t: (i, 0, 0))
        scratch = []
    return pl.pallas_call(
        kern,
        grid=(bsz // nb, seq // tt),
        in_specs=[xspec, kvspec, kvspec]
                 + [_const_spec(a.shape) for a in (xg, wq, wo, ng, wg, wu, wd, fg)],
        out_specs=xspec,
        out_shape=jax.ShapeDtypeStruct(x.shape, F32),
        scratch_shapes=scratch,
        compiler_params=pltpu.CompilerParams(dimension_semantics=("arbitrary", "arbitrary"),
                                             vmem_limit_bytes=VMEM_LIMIT),
        name="tail",
    )(x, k, v, xg, wq, wo, ng, wg, wu, wd, fg)


def _mix_tail(x, sconv, shgrn, mk, mv, w, fg, mix_tile, tail_tile):
    x, new_conv, new_hgrn = _mix_call(x, w["mix_norm"], w["w_in"], w["conv_dw"], w["conv_dw_b"],
                                      w["conv_ln_g"], w["conv_ln_b"], w["hgrn_lb"], w["hgrn_norm"],
                                      w["w_out"], sconv, shgrn, *mix_tile)
    y = _tail_call(x, mk, mv, w["xattn_norm"], w["xattn_wq"], w["xattn_wo"], w["ffn2_norm"],
                   w["ffn2_w_gate"], w["ffn2_w_up"], w["ffn2_w_down"], fg, *tail_tile)
    return y, new_conv, new_hgrn


def kernel(x_prompt, x_sample, mem_prompt, state_conv, state_hgrn, cache_mem_k, cache_mem_v, ffn1_norm, ffn1_w_gate, ffn1_w_up, ffn1_w_down, mix_norm, w_in, conv_dw, conv_dw_b, conv_ln_g, conv_ln_b, hgrn_lb, hgrn_norm, w_out, mem_norm, mem_wk, mem_wv, xattn_norm, xattn_wq, xattn_wo, ffn2_norm, ffn2_w_gate, ffn2_w_up, ffn2_w_down, final_norm):
    depth = ffn1_norm.shape[0]
    assert depth == 1, "the per-layer forget-gate bound is taken from row 0 of the cumulative softmax"
    bp, seq, d = x_prompt.shape
    bs, dec_seq, _ = x_sample.shape
    n_mem = mem_prompt.shape[1]
    d_hgrn = hgrn_lb.shape[1]
    d_conv = conv_dw.shape[2]
    d_head = d_hgrn // HGRN_HEADS

    def row(a):
        return a.reshape(1, -1)

    fg = row(final_norm)
    (mk, mv, mk_flat, mv_flat), ffn1_w = _memkv_call(
        mem_prompt, row(mem_norm[0]), mem_wk[0], mem_wv[0], 1,
        cast=(ffn1_w_gate[0], ffn1_w_up[0], ffn1_w_down[0]))

    later = {"w_in": w_in[0], "w_out": w_out[0], "xattn_wq": xattn_wq[0], "xattn_wo": xattn_wo[0],
             "ffn2_w_gate": ffn2_w_gate[0], "ffn2_w_up": ffn2_w_up[0], "ffn2_w_down": ffn2_w_down[0]}
    (xp, xs), later_bf16 = _ffn_call(
        (x_prompt.reshape(bp * seq, d), x_sample.reshape(bs * dec_seq, d)), row(ffn1_norm[0]), *ffn1_w,
        tiles=((1024, 4), (512, 2)), cast=tuple(later.values()))
    w = dict(zip(later, later_bf16))
    w.update({
        "mix_norm": row(mix_norm[0]), "conv_dw": conv_dw[0], "conv_dw_b": row(conv_dw_b[0]),
        "conv_ln_g": row(conv_ln_g[0]), "conv_ln_b": row(conv_ln_b[0]),
        "hgrn_lb": hgrn_lb, "hgrn_norm": row(hgrn_norm[0]),
        "xattn_norm": row(xattn_norm[0]), "ffn2_norm": row(ffn2_norm[0]),
    })

    y_p, conv_p, hgrn_p = _mix_tail(
        xp.reshape(bp, seq, d), jnp.zeros((bp, CONV_BUF, d_conv), F32),
        jnp.zeros((bp, HGRN_HEADS, d_head, d_head), F32), mk_flat, mv_flat, w, fg,
        mix_tile=(1, 512, 128), tail_tile=(1, 1024, 4))
    y_s, conv_s, hgrn_s = _mix_tail(
        xs.reshape(bs, dec_seq, d), state_conv[0], state_hgrn[0],
        cache_mem_k[0], cache_mem_v[0], w, fg,
        mix_tile=(8, dec_seq, 256), tail_tile=(4, dec_seq, 1))

    return (y_p, y_s, conv_p[None], hgrn_p[None], mk[None], mv[None],
            conv_s[None], hgrn_s[None])
```

```python
import functools

import jax
import jax.numpy as jnp
from jax import lax
from jax.experimental import pallas as pl
from jax.experimental.pallas import tpu as pltpu

F32 = jnp.float32
BF16 = jnp.bfloat16

EPS = 1e-6
CHUNK = 64
CONV_WIDTH = 31
CONV_BUF = CONV_WIDTH - 1
HGRN_HEADS = 4
MEM_HEADS = 4
HALO = 32
CONV_ROWS = 32
SUBLANES = 8
BF16_ROWS = 16
STACK_ROWS = 256
VMEM_LIMIT = 56 * 1024 * 1024


def _dot(a, b):
    return jnp.dot(a, b, preferred_element_type=F32)


def _dot_nt(a, b):
    return lax.dot_general(a, b, (((1,), (1,)), ((), ())), preferred_element_type=F32)


def _dot_tn(a, b):
    return lax.dot_general(a, b, (((0,), (0,)), ((), ())), preferred_element_type=F32)


def _rms(x, g):
    return x * lax.rsqrt(jnp.mean(x * x, axis=-1, keepdims=True) + EPS) * g


def _gated(a, x):
    ha = 0.5 * a
    return ha + ha * jnp.tanh(0.5 * x)


def _silu(x):
    hx = 0.5 * x
    return hx + hx * jnp.tanh(hx)


def _ffn(x, ng, wg, wu, wd):
    xn = _rms(x, ng).astype(BF16)
    h = (_silu(_dot(xn, wg)) * _dot(xn, wu)).astype(BF16)
    return x + 0.5 * _dot(h, wd() if callable(wd) else wd)


def _const_spec(shape):
    nd = len(shape)
    return pl.BlockSpec(shape, lambda *_: (0,) * nd, pipeline_mode=pl.Buffered(1))


def _row_groups(m, n_split):
    mg = m // n_split
    return [(g * mg, (g + 1) * mg) for g in range(n_split)]


def _cast_specs(arrays, n_steps):
    specs = []
    for a in arrays:
        n_blk = max(n for n in range(1, n_steps + 1) if a.shape[0] % (n * BF16_ROWS) == 0)
        specs.append(pl.BlockSpec((a.shape[0] // n_blk, a.shape[1]),
                                  lambda i, n_blk=n_blk: (jnp.minimum(i, n_blk - 1), 0)))
    return specs


def _cast_rows(src_refs, dst_refs):
    for src, dst in zip(src_refs, dst_refs):
        dst[...] = src[...].astype(BF16)


def _ffn_kernel(plan, n_cast, ng_ref, wg_ref, wu_ref, wd_hbm, *rest):
    n_x = len(plan)
    x_refs, cast_in = rest[:n_x], rest[n_x:n_x + n_cast]
    o_refs, cast_out = rest[n_x + n_cast:2 * n_x + n_cast], rest[2 * n_x + n_cast:2 * n_x + 2 * n_cast]
    wd_ref, wd_sem = rest[2 * n_x + 2 * n_cast:]
    i = pl.program_id(0)
    wd_copy = pltpu.make_async_copy(wd_hbm, wd_ref, wd_sem.at[0])

    @pl.when(i == 0)
    def _():
        wd_copy.start()

    for k, ((start, n_blk, n_split), x_ref, o_ref) in enumerate(zip(plan, x_refs, o_refs)):
        @pl.when((i >= start) & (i < start + n_blk))
        def _(k=k, x_ref=x_ref, o_ref=o_ref, n_split=n_split):
            for n, (g0, g1) in enumerate(_row_groups(x_ref.shape[0], n_split)):
                def wd(first=(k == 0 and n == 0)):
                    if first:
                        @pl.when(i == 0)
                        def _():
                            wd_copy.wait()
                    return wd_ref[...]
                o_ref[g0:g1, :] = _ffn(x_ref[g0:g1, :], ng_ref[...], wg_ref[...], wu_ref[...], wd)
    _cast_rows(cast_in, cast_out)


def _ffn_call(xs, ng, wg, wu, wd, tiles, cast=()):
    d = xs[0].shape[1]
    plan, row_specs, start = [], [], 0
    for x, (tm, n_split) in zip(xs, tiles):
        n_blk = x.shape[0] // tm
        plan.append((start, n_blk, n_split))
        row_specs.append(pl.BlockSpec(
            (tm, d), lambda i, start=start, n_blk=n_blk: (jnp.clip(i - start, 0, n_blk - 1), 0)))
        start += n_blk
    cast_specs = _cast_specs(cast, start)
    out = pl.pallas_call(
        functools.partial(_ffn_kernel, tuple(plan), len(cast)),
        grid=(start,),
        in_specs=[_const_spec(ng.shape), _const_spec(wg.shape), _const_spec(wu.shape),
                  pl.BlockSpec(memory_space=pl.ANY)] + row_specs + cast_specs,
        out_specs=row_specs + cast_specs,
        out_shape=[jax.ShapeDtypeStruct(x.shape, F32) for x in xs]
                  + [jax.ShapeDtypeStruct(a.shape, BF16) for a in cast],
        scratch_shapes=[pltpu.VMEM(wd.shape, wd.dtype), pltpu.SemaphoreType.DMA((1,))],
        compiler_params=pltpu.CompilerParams(dimension_semantics=("arbitrary",),
                                             vmem_limit_bytes=VMEM_LIMIT),
        name="ffn",
    )(ng, wg, wu, wd, *xs, *cast)
    return out[:len(xs)], out[len(xs):]


def _memkv_kernel(n_cast, m_ref, g_ref, wk_ref, wv_ref, *rest):
    cast_in, (k_ref, v_ref, kb_ref, vb_ref) = rest[:n_cast], rest[n_cast:n_cast + 4]
    cast_out, (wkb, wvb) = rest[n_cast + 4:2 * n_cast + 4], rest[2 * n_cast + 4:]
    _cast_rows(cast_in, cast_out)

    @pl.when(pl.program_id(0) == 0)
    def _round_weights():
        wkb[...] = wk_ref[...].astype(BF16)
        wvb[...] = wv_ref[...].astype(BF16)

    nb, n_mem, n_head, dh = k_ref.shape
    mn = _rms(m_ref[...].reshape(nb * n_mem, m_ref.shape[-1]), g_ref[...]).astype(BF16)
    for w_ref, o_ref, ob_ref in ((wkb, k_ref, kb_ref), (wvb, v_ref, vb_ref)):
        y = _dot(mn, w_ref[...])
        ob_ref[...] = y.astype(BF16).reshape(ob_ref.shape)
        for h in range(n_head):
            o_ref[:, :, h, :] = y[:, h * dh:(h + 1) * dh].reshape(nb, n_mem, dh)


def _memkv_call(mem, g, wk, wv, nb, cast=()):
    bsz, n_mem, d = mem.shape
    dh = d // MEM_HEADS
    n_steps = bsz // nb
    row = pl.BlockSpec((nb, n_mem, d), lambda i: (i, 0, 0))
    kv = pl.BlockSpec((nb, n_mem, MEM_HEADS, dh), lambda i: (i, 0, 0, 0))
    cast_specs = _cast_specs(cast, n_steps)
    out = pl.pallas_call(
        functools.partial(_memkv_kernel, len(cast)),
        grid=(n_steps,),
        in_specs=[row, _const_spec(g.shape), _const_spec(wk.shape), _const_spec(wv.shape)] + cast_specs,
        out_specs=[kv, kv, row, row] + cast_specs,
        out_shape=[jax.ShapeDtypeStruct((bsz, n_mem, MEM_HEADS, dh), F32)] * 2
                  + [jax.ShapeDtypeStruct((bsz, n_mem, d), BF16)] * 2
                  + [jax.ShapeDtypeStruct(a.shape, BF16) for a in cast],
        scratch_shapes=[pltpu.VMEM(wk.shape, BF16), pltpu.VMEM(wv.shape, BF16)],
        compiler_params=pltpu.CompilerParams(dimension_semantics=("arbitrary",),
                                             vmem_limit_bytes=VMEM_LIMIT),
        name="memkv",
    )(mem, g, wk, wv, *cast)
    return out[:4], out[4:]


def _hgrn_tile(z, lbv, hn, masks, states, chunk, chunks_per_seq, sub, d_head):
    m = z.shape[0]
    n_c, per_sub = m // chunk, sub // chunk
    subs = [slice(s0, s0 + sub) for s0 in range(0, m, sub)]
    dh = HGRN_HEADS * d_head
    o0 = z.shape[1] - 4 * dh
    tri, causal = masks
    c0, c1 = 0.5 * (1.0 + lbv), 0.5 * (1.0 - lbv)
    k_sub, gc_sub, decay = [], [], []
    for rs in subs:
        ct = c1 * jnp.tanh(0.5 * z[rs, o0 + dh:o0 + 2 * dh])
        k_sub.append((1.0 - c0) - ct)
        logf = jnp.log(c0 + ct)
        l_hi = logf.astype(BF16)
        l_lo = (logf - l_hi.astype(F32)).astype(BF16)
        gc = _dot(tri, l_hi) + _dot(tri, l_lo)
        gc_sub.append(gc)
        decay += [jnp.exp(gc[(j + 1) * chunk - 1:(j + 1) * chunk, :]) for j in range(per_sub)]
    zero = jnp.zeros((chunk, d_head), BF16)
    out, cur = [], [list(s) for s in states]
    for h in range(HGRN_HEADS):
        hs = slice(h * d_head, (h + 1) * d_head)
        q_dec, o_intra, kv = [], [], []
        for si, rs in enumerate(subs):
            g = gc_sub[si][:, hs]
            qd = (_silu(z[rs, o0 + h * d_head:o0 + (h + 1) * d_head]) * jnp.exp(g)).astype(BF16)
            kdf = k_sub[si][:, hs] * jnp.exp(-g)
            vb = z[rs, o0 + 2 * dh + h * d_head:o0 + 2 * dh + (h + 1) * d_head].astype(BF16)
            a = jnp.where(causal, _dot_nt(qd, kdf.astype(BF16)), 0.0).astype(BF16)
            o_intra.append(_dot(a, vb))
            k_blocks = jnp.concatenate(
                [jnp.concatenate(
                    [(kdf[j * chunk:(j + 1) * chunk] * decay[si * per_sub + j][:, hs]).astype(BF16)
                     if i == j else zero for i in range(per_sub)], axis=1) for j in range(per_sub)], axis=0)
            kv.append(_dot_tn(vb, k_blocks))
            q_dec.append(qd)
        st_in = []
        for c in range(n_c):
            seq, j = c // chunks_per_seq, c % per_sub
            st_in.append(cur[seq][h])
            cur[seq][h] = st_in[c] * decay[c][:, hs] + kv[c // per_sub][:, j * d_head:(j + 1) * d_head]
        o = []
        for si, rs in enumerate(subs):
            stack = jnp.concatenate([s.astype(BF16) for s in st_in[si * per_sub:(si + 1) * per_sub]], axis=0)
            o_inter = _dot_nt(q_dec[si], stack)
            o.append(o_intra[si] + jnp.concatenate(
                [o_inter[j * chunk:(j + 1) * chunk, j * d_head:(j + 1) * d_head] for j in range(per_sub)],
                axis=0))
        o = jnp.concatenate(o, axis=0)
        on = o * lax.rsqrt(jnp.mean(o * o, axis=-1, keepdims=True) + EPS) * hn[:, hs]
        zg = z[:, o0 + 3 * dh + h * d_head:o0 + 3 * dh + (h + 1) * d_head]
        out.append((on * _silu(zg)).astype(BF16))
    return out, cur


def _mix_kernel(nb, tt, chunk, hgrn_rows,
                x_ref, ng_ref, win_ref, cw_ref, cb_ref, lng_ref, lnb_ref, hlb_ref, hn_ref, wout_ref,
                sconv_ref, shgrn_ref,
                o_ref, oconv_ref, ohgrn_ref,
                zbuf, ubuf, ushift, wtap, sbuf, cat):
    t = pl.program_id(1)
    d_conv = cw_ref.shape[1]
    d_head = sbuf.shape[-1]
    m = nb * tt
    first = HALO - CONV_BUF

    @pl.when((pl.program_id(0) == 0) & (t == 0))
    def _spread_taps():
        wtap[...] = jnp.broadcast_to(cw_ref[...][:, None, :], wtap.shape)

    @pl.when(t == 0)
    def _load_state():
        ubuf[:, 0:first, :] = jnp.zeros((nb, first, d_conv), F32)
        ubuf[:, first:HALO, :] = sconv_ref[...]
        for b in range(nb):
            for h in range(HGRN_HEADS):
                sbuf[b, h] = shgrn_ref[b, h].T

    x = x_ref[...].reshape(m, x_ref.shape[-1])
    zbuf[...] = _dot(_rms(x, ng_ref[...]).astype(BF16), win_ref[...])

    for b in range(nb):
        r = slice(b * tt, (b + 1) * tt)
        ubuf[b, HALO:HALO + tt, :] = _gated(zbuf[r, 0:d_conv], zbuf[r, d_conv:2 * d_conv])
    cb = cb_ref[...]
    lng = lng_ref[...]
    lnb = lnb_ref[...]
    n_shift = ushift.shape[2]
    for r in range(1, SUBLANES):
        ushift[r - 1] = ubuf[:, r:r + n_shift, :]
    for b in range(nb):
        for r0 in range(0, tt, CONV_ROWS):
            acc = jnp.broadcast_to(cb, (CONV_ROWS // SUBLANES, SUBLANES, d_conv))
            for j in range(CONV_WIDTH):
                a8, r = divmod(first + j, SUBLANES)
                lo = r0 + a8 * SUBLANES
                src = ubuf[b, lo:lo + CONV_ROWS, :] if r == 0 else ushift[r - 1, b, lo:lo + CONV_ROWS, :]
                acc = acc + wtap[j] * src.reshape(acc.shape)
            acc = acc.reshape(CONV_ROWS, d_conv)
            mu = jnp.mean(acc, axis=-1, keepdims=True)
            dev = acc - mu
            var = jnp.mean(dev * dev, axis=-1, keepdims=True)
            yc = _silu(dev * lax.rsqrt(var + EPS) * lng + lnb)
            cat[b * tt + r0:b * tt + r0 + CONV_ROWS, 0:d_conv] = yc.astype(BF16)
    new_rows = ubuf[:, tt + first:tt + HALO, :]
    ubuf[:, first:HALO, :] = new_rows

    hlb = hlb_ref[...]
    e = jnp.exp(hlb - jnp.max(hlb, axis=0, keepdims=True))
    lbv = e[0:1, :] / jnp.sum(e, axis=0, keepdims=True)
    shift = chunk.bit_length() - 1
    assert chunk == 1 << shift
    sub = min(m, hgrn_rows)
    row_i = lax.broadcasted_iota(jnp.int32, (sub, sub), 0)
    col_i = lax.broadcasted_iota(jnp.int32, (sub, sub), 1)
    causal = (row_i >= col_i) & (lax.shift_right_logical(row_i, shift) == lax.shift_right_logical(col_i, shift))
    masks = (jnp.where(causal, 1.0, 0.0).astype(BF16), causal)
    heads, state = _hgrn_tile(zbuf, lbv, hn_ref[...], masks,
                              [[sbuf[b, h] for h in range(HGRN_HEADS)] for b in range(nb)],
                              chunk, tt // chunk, sub, d_head)
    for h in range(HGRN_HEADS):
        cat[:, d_conv + h * d_head:d_conv + (h + 1) * d_head] = heads[h]
        for b in range(nb):
            sbuf[b, h] = state[b][h]

    o_ref[...] = (x + _dot(cat[...], wout_ref[...])).reshape(o_ref.shape)

    @pl.when(t == pl.num_programs(1) - 1)
    def _store_state():
        oconv_ref[...] = ubuf[:, first:HALO, :]
        for b in range(nb):
            for h in range(HGRN_HEADS):
                ohgrn_ref[b, h] = sbuf[b, h].T


def _mix_call(x, ng, win, cw, cb, lng, lnb, hlb, hn, wout, sconv, shgrn, nb, tt, hgrn_rows):
    bsz, seq, d = x.shape
    d_conv = cw.shape[1]
    d_head = shgrn.shape[-1]
    chunk = min(CHUNK, seq)
    kern = functools.partial(_mix_kernel, nb, tt, chunk, hgrn_rows)
    xspec = pl.BlockSpec((nb, tt, d), lambda i, t: (i, t, 0))
    cspec = pl.BlockSpec((nb, CONV_BUF, d_conv), lambda i, t: (i, 0, 0))
    sspec = pl.BlockSpec((nb, HGRN_HEADS, d_head, d_head), lambda i, t: (i, 0, 0, 0))
    return pl.pallas_call(
        kern,
        grid=(bsz // nb, seq // tt),
        in_specs=[xspec] + [_const_spec(a.shape) for a in (ng, win, cw, cb, lng, lnb, hlb, hn, wout)]
                 + [cspec, sspec],
        out_specs=[xspec, cspec, sspec],
        out_shape=[jax.ShapeDtypeStruct(x.shape, F32),
                   jax.ShapeDtypeStruct(sconv.shape, F32),
                   jax.ShapeDtypeStruct(shgrn.shape, F32)],
        scratch_shapes=[pltpu.VMEM((nb * tt, win.shape[1]), F32),
                        pltpu.VMEM((nb, HALO + tt, d_conv), F32),
                        pltpu.VMEM((SUBLANES - 1, nb, HALO + tt - SUBLANES, d_conv), F32),
                        pltpu.VMEM((CONV_WIDTH, SUBLANES, d_conv), F32),
                        pltpu.VMEM((nb, HGRN_HEADS, d_head, d_head), F32),
                        pltpu.VMEM((nb * tt, d), BF16)],
        compiler_params=pltpu.CompilerParams(dimension_semantics=("arbitrary", "arbitrary"),
                                             vmem_limit_bytes=VMEM_LIMIT),
        name="mix",
    )(x, ng, win, cw, cb, lng, lnb, hlb, hn, wout, sconv, shgrn)


def _tail_kernel(nb, tt, n_split, kv_native,
                 x_ref, k_ref, v_ref, xg_ref, wq_ref, wo_ref, ng_ref, wg_ref, wu_ref, wd_ref, fg_ref,
                 o_ref, *scratch):
    d = x_ref.shape[-1]
    dh = d // MEM_HEADS
    scale = dh ** -0.5
    ffn_w, w_sem, scratch = scratch[:3], scratch[3], scratch[4:]
    first_step = (pl.program_id(0) == 0) & (pl.program_id(1) == 0)
    w_copies = [pltpu.make_async_copy(src, dst, w_sem.at[j])
                for j, (src, dst) in enumerate(zip((wg_ref, wu_ref, wd_ref), ffn_w))]

    @pl.when(first_step)
    def _start_weights():
        for cp in w_copies:
            cp.start()

    if kv_native:
        kvbuf, sem = scratch
        i, n = pl.program_id(0), pl.num_programs(0)

        def kv_copies(blk, slot):
            return [pltpu.make_async_copy(src.at[pl.ds(blk * nb, nb), :, h, :],
                                          kvbuf.at[slot, j, :, :, h * dh:(h + 1) * dh],
                                          sem.at[slot, j, h])
                    for j, src in enumerate((k_ref, v_ref)) for h in range(MEM_HEADS)]

        def start_all(copies):
            for c, cp in enumerate(copies):
                cp.start(priority=c % 2)

        @pl.when(i == 0)
        def _first_block():
            start_all(kv_copies(0, 0))

        @pl.when(i + 1 < n)
        def _next_block():
            start_all(kv_copies(i + 1, lax.rem(i + 1, 2)))

        slot = lax.rem(i, 2)
        for cp in kv_copies(i, slot):
            cp.wait()
        k_ref, v_ref = kvbuf.at[slot, 0], kvbuf.at[slot, 1]
    kb = [k_ref[b].astype(BF16) for b in range(nb)]
    vb = [v_ref[b].astype(BF16) for b in range(nb)]
    m = nb * tt
    x_all = x_ref[...].reshape(m, d)
    q = _dot(_rms(x_all, xg_ref[...]).astype(BF16), wq_ref[...])
    hsl = [slice(h * dh, (h + 1) * dh) for h in range(MEM_HEADS)]
    qb = (q * scale).astype(BF16)
    stacked = tt * MEM_HEADS <= STACK_ROWS
    scores = []
    for b in range(nb):
        qr = qb[b * tt:(b + 1) * tt]
        if stacked:
            zq = jnp.zeros((tt, dh), BF16)
            q_stack = jnp.concatenate(
                [jnp.concatenate([qr[:, hs] if j == h else zq for j in range(MEM_HEADS)], axis=1)
                 for h, hs in enumerate(hsl)], axis=0)
            scores.append(_dot_nt(q_stack, kb[b]))
        else:
            scores += [_dot_nt(qr[:, hs], kb[b][:, hs]) for hs in hsl]
    sc = jnp.concatenate(scores, axis=0)
    ex = jnp.exp(sc - jnp.max(sc, axis=-1, keepdims=True))
    p = (ex * (1.0 / jnp.sum(ex, axis=-1, keepdims=True))).astype(BF16)
    att = []
    for b in range(nb):
        pb = p[b * MEM_HEADS * tt:(b + 1) * MEM_HEADS * tt]
        if stacked:
            pv = _dot(pb, vb[b])
            heads = [pv[h * tt:(h + 1) * tt, hs] for h, hs in enumerate(hsl)]
        else:
            heads = [_dot(pb[h * tt:(h + 1) * tt], vb[b][:, hs]) for h, hs in enumerate(hsl)]
        att.append(jnp.concatenate(heads, axis=1).astype(BF16))
    xa = x_all + _dot(jnp.concatenate(att, axis=0), wo_ref[...])
    @pl.when(first_step)
    def _weights_ready():
        for cp in w_copies:
            cp.wait()

    wg, wu, wd = ffn_w
    for g0, g1 in _row_groups(m, n_split):
        if nb == 1:
            rows, blk = (0, slice(g0, g1)), (g1 - g0, d)
        else:
            assert g0 % tt == 0 and g1 % tt == 0
            rows, blk = (slice(g0 // tt, g1 // tt),), ((g1 - g0) // tt, tt, d)
        x = _ffn(xa[g0:g1], ng_ref[...], wg[...], wu[...], wd[...])
        o_ref[rows] = _rms(x, fg_ref[...]).reshape(blk)


def _tail_call(x, k, v, xg, wq, wo, ng, wg, wu, wd, fg, nb, tt, n_split):
    bsz, seq, d = x.shape
    kv_native = k.ndim == 4
    kern = functools.partial(_tail_kernel, nb, tt, n_split, kv_native)
    xspec = pl.BlockSpec((nb, tt, d), lambda i, t: (i, t, 0))
    if kv_native:
        assert seq == tt, "the key/value copies are pipelined over the batch axis of the grid"
        kvspec = pl.BlockSpec(memory_space=pl.ANY)
        scratch = [pltpu.VMEM((2, 2, nb, k.shape[1], d), F32), pltpu.SemaphoreType.DMA((2, 2, MEM_HEADS))]
    else:
        kvspec = pl.BlockSpec((nb,) + k.shape[1:], lambda i, t: (i, 0, 0))
        scratch = []
    late = (wg, wu, wd)
    scratch = [pltpu.VMEM(a.shape, a.dtype) for a in late] + [pltpu.SemaphoreType.DMA((len(late),))] + scratch
    hbm = pl.BlockSpec(memory_space=pl.ANY)
    return pl.pallas_call(
        kern,
        grid=(bsz // nb, seq // tt),
        in_specs=[xspec, kvspec, kvspec]
                 + [_const_spec(a.shape) for a in (xg, wq, wo, ng)] + [hbm] * len(late) + [_const_spec(fg.shape)],
        out_specs=xspec,
        out_shape=jax.ShapeDtypeStruct(x.shape, F32),
        scratch_shapes=scratch,
        compiler_params=pltpu.CompilerParams(dimension_semantics=("arbitrary", "arbitrary"),
                                             vmem_limit_bytes=VMEM_LIMIT),
        name="tail",
    )(x, k, v, xg, wq, wo, ng, wg, wu, wd, fg)


def _mix_tail(x, sconv, shgrn, mk, mv, w, fg, mix_tile, tail_tile):
    x, new_conv, new_hgrn = _mix_call(x, w["mix_norm"], w["w_in"], w["conv_dw"], w["conv_dw_b"],
                                      w["conv_ln_g"], w["conv_ln_b"], w["hgrn_lb"], w["hgrn_norm"],
                                      w["w_out"], sconv, shgrn, *mix_tile)
    y = _tail_call(x, mk, mv, w["xattn_norm"], w["xattn_wq"], w["xattn_wo"], w["ffn2_norm"],
                   w["ffn2_w_gate"], w["ffn2_w_up"], w["ffn2_w_down"], fg, *tail_tile)
    return y, new_conv, new_hgrn


def kernel(x_prompt, x_sample, mem_prompt, state_conv, state_hgrn, cache_mem_k, cache_mem_v, ffn1_norm, ffn1_w_gate, ffn1_w_up, ffn1_w_down, mix_norm, w_in, conv_dw, conv_dw_b, conv_ln_g, conv_ln_b, hgrn_lb, hgrn_norm, w_out, mem_norm, mem_wk, mem_wv, xattn_norm, xattn_wq, xattn_wo, ffn2_norm, ffn2_w_gate, ffn2_w_up, ffn2_w_down, final_norm):
    depth = ffn1_norm.shape[0]
    assert depth == 1, "the per-layer forget-gate bound is taken from row 0 of the cumulative softmax"
    bp, seq, d = x_prompt.shape
    bs, dec_seq, _ = x_sample.shape
    n_mem = mem_prompt.shape[1]
    d_hgrn = hgrn_lb.shape[1]
    d_conv = conv_dw.shape[2]
    d_head = d_hgrn // HGRN_HEADS

    def row(a):
        return a.reshape(1, -1)

    fg = row(final_norm)
    (mk, mv, mk_flat, mv_flat), ffn1_w = _memkv_call(
        mem_prompt, row(mem_norm[0]), mem_wk[0], mem_wv[0], 1,
        cast=(ffn1_w_gate[0], ffn1_w_up[0], ffn1_w_down[0]))

    later = {"w_in": w_in[0], "w_out": w_out[0], "xattn_wq": xattn_wq[0], "xattn_wo": xattn_wo[0],
             "ffn2_w_gate": ffn2_w_gate[0], "ffn2_w_up": ffn2_w_up[0], "ffn2_w_down": ffn2_w_down[0]}
    (xp, xs), later_bf16 = _ffn_call(
        (x_prompt.reshape(bp * seq, d), x_sample.reshape(bs * dec_seq, d)), row(ffn1_norm[0]), *ffn1_w,
        tiles=((1024, 4), (512, 2)), cast=tuple(later.values()))
    w = dict(zip(later, later_bf16))
    w.update({
        "mix_norm": row(mix_norm[0]), "conv_dw": conv_dw[0], "conv_dw_b": row(conv_dw_b[0]),
        "conv_ln_g": row(conv_ln_g[0]), "conv_ln_b": row(conv_ln_b[0]),
        "hgrn_lb": hgrn_lb, "hgrn_norm": row(hgrn_norm[0]),
        "xattn_norm": row(xattn_norm[0]), "ffn2_norm": row(ffn2_norm[0]),
    })

    y_p, conv_p, hgrn_p = _mix_tail(
        xp.reshape(bp, seq, d), jnp.zeros((bp, CONV_BUF, d_conv), F32),
        jnp.zeros((bp, HGRN_HEADS, d_head, d_head), F32), mk_flat, mv_flat, w, fg,
        mix_tile=(1, 512, 128), tail_tile=(1, 1024, 4))
    y_s, conv_s, hgrn_s = _mix_tail(
        xs.reshape(bs, dec_seq, d), state_conv[0], state_hgrn[0],
        cache_mem_k[0], cache_mem_v[0], w, fg,
        mix_tile=(8, dec_seq, 256), tail_tile=(4, dec_seq, 1))

    return (y_p, y_s, conv_p[None], hgrn_p[None], mk[None], mv[None],
            conv_s[None], hgrn_s[None])
```
